```python
import jax, jax.numpy as jnp
from jax import lax
import numpy as np

D_MODEL = 1024
BATCH = 8
SEQ = 4096
DEPTH = 1

CONV_CH = D_MODEL
CONV_K = 3
GMLP_CH = D_MODEL
GMLP_GROUPS = 8
GMLP_GROUP_CH = GMLP_CH // GMLP_GROUPS
CHUNK = 128
MEM_LEN = 256
XA_HEADS = 4
XA_HEAD_DIM = D_MODEL // XA_HEADS
XA_WIDTH = XA_HEADS * XA_HEAD_DIM
N_BRANCH = 3
W_IN_SIZES = (CONV_CH, CONV_CH, CONV_CH, GMLP_CH, GMLP_CH, XA_WIDTH, N_BRANCH * D_MODEL)
W_IN_COLS = sum(W_IN_SIZES)
W_IN_SPLITS = tuple(int(s) for s in np.cumsum(W_IN_SIZES)[:-1])
N_EXPERTS = 32
TOP_K = 4
D_FF = D_MODEL
SWIGLU_LIMIT = 7.0
SWIGLU_ALPHA = 1.702
LN_EPS = 1e-5
DEEPNORM_ALPHA = (2 * DEPTH) ** 0.25
DEEPNORM_BETA = (8 * DEPTH) ** -0.25

kernel_name = "hybrid_conv_gmlp_xattn_moe_deepnorm"


def layer_norm(x, g, b):
    xf = x.astype(jnp.float32)
    mu = jnp.mean(xf, axis=-1, keepdims=True)
    xc = xf - mu
    var = jnp.mean(xc * xc, axis=-1, keepdims=True)
    y = xc * lax.rsqrt(var + LN_EPS) * g.astype(jnp.float32) + b.astype(jnp.float32)
    return y.astype(x.dtype)


def causal_depthwise_conv(u, w):
    c = u.shape[-1]
    return lax.conv_general_dilated(
        u, w[:, None, :].astype(u.dtype), window_strides=(1,),
        padding=((CONV_K - 1, 0),), dimension_numbers=("NWC", "WIO", "NWC"),
        feature_group_count=c)


def gmlp_spatial_gate(u, v, ws, bias, ln_g, ln_b):
    bsz, seq, _ = v.shape
    v = layer_norm(v, ln_g, ln_b)
    v = v.reshape(bsz, seq // CHUNK, CHUNK, GMLP_GROUPS, GMLP_GROUP_CH)
    ws_causal = jnp.tril(ws)
    f = jnp.einsum("gts,bnsgc->bntgc", ws_causal.astype(v.dtype), v)
    f = f + bias.T[None, None, :, :, None].astype(v.dtype)
    return u * f.reshape(bsz, seq, GMLP_CH)


def memory_cross_attention(q, mem_n, w_kv):
    bsz, seq, _ = q.shape
    k, v = jnp.split(mem_n @ w_kv, 2, axis=-1)
    q = q.reshape(bsz, seq, XA_HEADS, XA_HEAD_DIM)
    k = k.reshape(bsz, MEM_LEN, XA_HEADS, XA_HEAD_DIM)
    v = v.reshape(bsz, MEM_LEN, XA_HEADS, XA_HEAD_DIM)
    scores = jnp.einsum("bshd,bmhd->bhsm", q, k).astype(jnp.float32) * (XA_HEAD_DIM ** -0.5)
    probs = jax.nn.softmax(scores, axis=-1).astype(v.dtype)
    o = jnp.einsum("bhsm,bmhd->bshd", probs, v)
    return o.reshape(bsz, seq, XA_WIDTH)


def token_mix(h, mem_n, w_in, b_gate, conv_w, gmlp_ws, gmlp_b, gmlp_ln_g, gmlp_ln_b,
              w_kv, w_conv_proj, w_gmlp_proj, w_xa_proj, w_out):
    z = h @ w_in
    cb, cc, ch, gu, gv, q, gates = jnp.split(z, W_IN_SPLITS, axis=-1)
    y_conv = cb * causal_depthwise_conv(cc * ch, conv_w)
    y_gmlp = gmlp_spatial_gate(jax.nn.gelu(gu), jax.nn.gelu(gv), gmlp_ws, gmlp_b, gmlp_ln_g, gmlp_ln_b)
    y_xa = memory_cross_attention(q, mem_n, w_kv)
    g_conv, g_gmlp, g_xa = jnp.split(jax.nn.sigmoid(gates + b_gate), N_BRANCH, axis=-1)
    merged = (g_conv * (y_conv @ w_conv_proj)
              + g_gmlp * (y_gmlp @ w_gmlp_proj)
              + g_xa * (y_xa @ w_xa_proj))
    return merged @ w_out


def moe_ffn(h, w_router, b_router, w_gate_up, b_gate_up, w_down, b_down):
    bsz, seq, d = h.shape
    n_tok = bsz * seq
    hf = h.reshape(n_tok, d)
    logits = (hf @ w_router + b_router).astype(jnp.float32)
    top_vals, top_idx = lax.top_k(logits, TOP_K)
    gate = jax.nn.softmax(top_vals, axis=-1).astype(h.dtype)
    expert_flat = top_idx.reshape(-1)
    order = jnp.argsort(expert_flat)
    sorted_expert = expert_flat[order]
    sorted_token = order // TOP_K
    group_sizes = jnp.bincount(expert_flat, length=N_EXPERTS).astype(jnp.int32)
    xs = hf[sorted_token]
    gu = lax.ragged_dot(xs, w_gate_up, group_sizes) + b_gate_up[sorted_expert]
    g_lin, u_lin = jnp.split(gu, 2, axis=-1)
    g_lin = jnp.minimum(g_lin, SWIGLU_LIMIT)
    u_lin = jnp.clip(u_lin, -SWIGLU_LIMIT, SWIGLU_LIMIT)
    act = (u_lin + 1) * (g_lin * jax.nn.sigmoid(SWIGLU_ALPHA * g_lin))
    eo = lax.ragged_dot(act, w_down, group_sizes) + b_down[sorted_expert]
    eo = eo * gate.reshape(-1)[order][:, None]
    y = jnp.zeros_like(hf).at[sorted_token].add(eo)
    return y.reshape(bsz, seq, d)


def setup_inputs(seed: int = 0) -> dict:
    key = jax.random.key(seed)
    ks = jax.random.split(key, 28)
    L = DEPTH
    beta = DEEPNORM_BETA

    def nrm(k, shape, scale):
        return jax.random.normal(k, shape, jnp.float32) * scale

    w_k = nrm(ks[11], (L, D_MODEL, XA_WIDTH), D_MODEL ** -0.5)
    w_v = nrm(ks[12], (L, D_MODEL, XA_WIDTH), D_MODEL ** -0.5 * beta)
    return {
        "x": nrm(ks[0], (BATCH, SEQ, D_MODEL), 1.0),
        "mem": nrm(ks[1], (BATCH, MEM_LEN, D_MODEL), 1.0),
        "w_in": nrm(ks[2], (L, D_MODEL, W_IN_COLS), D_MODEL ** -0.5),
        "b_gate": nrm(ks[3], (L, N_BRANCH * D_MODEL), 0.01),
        "conv_w": nrm(ks[4], (L, CONV_K, CONV_CH), CONV_K ** -0.5),
        "gmlp_ws": nrm(ks[5], (L, GMLP_GROUPS, CHUNK, CHUNK), CHUNK ** -0.5),
        "gmlp_b": 1.0 + nrm(ks[6], (L, GMLP_GROUPS, CHUNK), 0.01),
        "gmlp_ln_g": 1.0 + nrm(ks[7], (L, GMLP_CH), 0.01),
        "gmlp_ln_b": nrm(ks[8], (L, GMLP_CH), 0.01),
        "mem_ln_g": 1.0 + nrm(ks[9], (L, D_MODEL), 0.01),
        "mem_ln_b": nrm(ks[10], (L, D_MODEL), 0.01),
        "w_kv": jnp.concatenate([w_k, w_v], axis=-1),
        "w_conv_proj": nrm(ks[13], (L, CONV_CH, D_MODEL), CONV_CH ** -0.5 * beta),
        "w_gmlp_proj": nrm(ks[14], (L, GMLP_CH, D_MODEL), GMLP_CH ** -0.5 * beta),
        "w_xa_proj": nrm(ks[15], (L, XA_WIDTH, D_MODEL), XA_WIDTH ** -0.5 * beta),
        "w_out": nrm(ks[16], (L, D_MODEL, D_MODEL), D_MODEL ** -0.5 * beta),
        "ln1_g": 1.0 + nrm(ks[17], (L, D_MODEL), 0.01),
        "ln1_b": nrm(ks[18], (L, D_MODEL), 0.01),
        "w_router": nrm(ks[19], (L, D_MODEL, N_EXPERTS), D_MODEL ** -0.5),
        "b_router": nrm(ks[20], (L, N_EXPERTS), 0.01),
        "w_gate_up": nrm(ks[21], (L, N_EXPERTS, D_MODEL, 2 * D_FF), D_MODEL ** -0.5 * beta),
        "b_gate_up": nrm(ks[22], (L, N_EXPERTS, 2 * D_FF), 0.01),
        "w_down": nrm(ks[23], (L, N_EXPERTS, D_FF, D_MODEL), D_FF ** -0.5 * beta),
        "b_down": nrm(ks[24], (L, N_EXPERTS, D_MODEL), 0.01),
        "ln2_g": 1.0 + nrm(ks[25], (L, D_MODEL), 0.01),
        "ln2_b": nrm(ks[26], (L, D_MODEL), 0.01),
    }


def reference(x, mem, w_in, b_gate, conv_w, gmlp_ws, gmlp_b, gmlp_ln_g, gmlp_ln_b,
              mem_ln_g, mem_ln_b, w_kv, w_conv_proj, w_gmlp_proj, w_xa_proj, w_out,
              ln1_g, ln1_b, w_router, b_router, w_gate_up, b_gate_up, w_down, b_down,
              ln2_g, ln2_b):
    for l in range(DEPTH):
        mem_n = layer_norm(mem, mem_ln_g[l], mem_ln_b[l])
        mix = token_mix(x, mem_n, w_in[l], b_gate[l], conv_w[l], gmlp_ws[l], gmlp_b[l],
                        gmlp_ln_g[l], gmlp_ln_b[l], w_kv[l], w_conv_proj[l],
                        w_gmlp_proj[l], w_xa_proj[l], w_out[l])
        x = layer_norm(DEEPNORM_ALPHA * x + mix, ln1_g[l], ln1_b[l])
        ffn = moe_ffn(x, w_router[l], b_router[l], w_gate_up[l], b_gate_up[l],
                      w_down[l], b_down[l])
        x = layer_norm(DEEPNORM_ALPHA * x + ffn, ln2_g[l], ln2_b[l])
    return x
```

```python
import functools

import jax
import jax.numpy as jnp
import numpy as np
from jax import lax
from jax.experimental import pallas as pl
from jax.experimental.pallas import tpu as pltpu

F32 = jnp.float32
BF16 = jnp.bfloat16
I32 = jnp.int32

D_MODEL = 1024
CHUNK = 128
GMLP_GROUPS = 8
GROUP_CH = D_MODEL // GMLP_GROUPS
MEM_LEN = 256
XA_HEADS = 4
XA_HEAD_DIM = D_MODEL // XA_HEADS
N_EXPERTS = 32
TOP_K = 4
D_FF = D_MODEL
W_IN_COLS = 9 * D_MODEL
SWIGLU_LIMIT = 7.0
SWIGLU_ALPHA = 1.702
LN_EPS = 1e-5
DEEPNORM_ALPHA = 2.0 ** 0.25

LANES = 128
SUBLANES = 8
VMEM_LIMIT_BYTES = 60000 * 1024

TM_MIX = 512
TM_SORT = 256
SEG_ALIGN = SUBLANES
L_SORT = 1280
assert L_SORT >= TOP_K * TM_SORT + N_EXPERTS * (SEG_ALIGN - 1) and L_SORT % 256 == 0
XS_COLS = D_MODEL + LANES
TR = 256
ZROWS = 256


def _layer_norm(x, g, b):
    mu = jnp.mean(x, axis=-1, keepdims=True)
    xc = x - mu
    var = jnp.mean(xc * xc, axis=-1, keepdims=True)
    return xc * lax.rsqrt(var + LN_EPS) * g + b


def _gelu_tanh(x):
    c = np.float32(np.sqrt(2.0 / np.pi))
    return x * (0.5 * (1.0 + jnp.tanh(c * (x + np.float32(0.044715) * (x * x * x)))))


def _sigmoid(x):
    return 1.0 / (1.0 + jnp.exp(-x))


def _dot(a, b):
    return jnp.dot(a, b, preferred_element_type=F32)


def _const_spec(shape):
    n = len(shape)
    return pl.BlockSpec(shape, lambda *_: (0,) * n, pipeline_mode=pl.Buffered(1))


def _kv_kernel(mem_ref, g_ref, b_ref, wkv_ref, k_ref, v_ref):
    mem_n = _layer_norm(mem_ref[0], g_ref[...], b_ref[...])
    kv = _dot(mem_n.astype(BF16), wkv_ref[...])
    k_ref[0] = kv[:, :D_MODEL].astype(BF16)
    v_ref[0] = kv[:, D_MODEL:].astype(BF16)


def _kv_call(mem, g, b, wkv):
    bsz = mem.shape[0]
    return pl.pallas_call(
        _kv_kernel,
        grid=(bsz,),
        in_specs=[
            pl.BlockSpec((1, MEM_LEN, D_MODEL), lambda i: (i, 0, 0)),
            pl.BlockSpec((1, D_MODEL), lambda i: (0, 0)),
            pl.BlockSpec((1, D_MODEL), lambda i: (0, 0)),
            pl.BlockSpec((D_MODEL, 2 * D_MODEL), lambda i: (0, 0)),
        ],
        out_specs=[
            pl.BlockSpec((1, MEM_LEN, D_MODEL), lambda i: (i, 0, 0)),
            pl.BlockSpec((1, MEM_LEN, D_MODEL), lambda i: (i, 0, 0)),
        ],
        out_shape=[
            jax.ShapeDtypeStruct((bsz, MEM_LEN, D_MODEL), BF16),
            jax.ShapeDtypeStruct((bsz, MEM_LEN, D_MODEL), BF16),
        ],
        compiler_params=pltpu.CompilerParams(dimension_semantics=("arbitrary",)),
        name="kv",
    )(mem, g, b, wkv)


def _mix_kernel(x_ref, win_ref, bgate_ref, convw_ref, ws_ref, gbias_ref, glng_ref, glnb_ref,
                k_ref, v_ref, wc_ref, wg_ref, wx_ref, wo_ref, ln1g_ref, ln1b_ref,
                wrh_ref, wrl_ref, br_ref, x1_ref, logit_ref, ubuf_ref):
    tm = x_ref.shape[1]
    d = D_MODEL
    x = x_ref[0]
    xb = x.astype(BF16)

    def proj(i):
        return _dot(xb, win_ref[:, i * d:(i + 1) * d])

    def gate(i):
        return _sigmoid(proj(6 + i) + bgate_ref[:, i * d:(i + 1) * d])

    @pl.when(pl.program_id(1) == 0)
    def _():
        ubuf_ref[0:SUBLANES, :] = jnp.zeros((SUBLANES, d), F32)

    u = proj(1) * proj(2)
    ubuf_ref[SUBLANES:SUBLANES + tm, :] = u
    u1 = ubuf_ref[SUBLANES - 1:SUBLANES - 1 + tm, :]
    u2 = ubuf_ref[SUBLANES - 2:SUBLANES - 2 + tm, :]
    conv = convw_ref[0:1, :] * u2 + convw_ref[1:2, :] * u1 + convw_ref[2:3, :] * u
    ubuf_ref[0:SUBLANES, :] = ubuf_ref[tm:tm + SUBLANES, :]
    y_conv = proj(0) * conv
    merged = gate(0) * _dot(y_conv.astype(BF16), wc_ref[...])

    gu = _gelu_tanh(proj(3))
    gv = _gelu_tanh(proj(4))
    vn = _layer_norm(gv, glng_ref[...], glnb_ref[...]).astype(BF16)
    n_chunks = tm // CHUNK
    row_i = lax.broadcasted_iota(I32, (CHUNK, CHUNK), 0)
    col_i = lax.broadcasted_iota(I32, (CHUNK, CHUNK), 1)
    causal = col_i <= row_i
    f_cols = []
    for g in range(GMLP_GROUPS):
        w_g = jnp.where(causal, ws_ref[g], 0.0).astype(BF16)
        rhs = jnp.concatenate(
            [vn[c * CHUNK:(c + 1) * CHUNK, g * GROUP_CH:(g + 1) * GROUP_CH] for c in range(n_chunks)],
            axis=1)
        fg = _dot(w_g, rhs)
        f_cols.append(jnp.concatenate(
            [fg[:, c * GROUP_CH:(c + 1) * GROUP_CH] for c in range(n_chunks)], axis=0))
    f = jnp.concatenate(f_cols, axis=1)
    gbias = jnp.concatenate([gbias_ref[...]] * n_chunks, axis=0)
    y_gmlp = gu * (f + gbias)
    merged = merged + gate(1) * _dot(y_gmlp.astype(BF16), wg_ref[...])

    q = proj(5).astype(BF16)
    heads = []
    for h in range(XA_HEADS):
        sl = slice(h * XA_HEAD_DIM, (h + 1) * XA_HEAD_DIM)
        s = lax.dot_general(q[:, sl], k_ref[0, :, sl], (((1,), (1,)), ((), ())),
                            preferred_element_type=F32) * np.float32(XA_HEAD_DIM ** -0.5)
        p = jnp.exp(s - jnp.max(s, axis=-1, keepdims=True))
        p = p * (1.0 / jnp.sum(p, axis=-1, keepdims=True))
        heads.append(_dot(p.astype(BF16), v_ref[0, :, sl]))
    y_xa = jnp.concatenate(heads, axis=1)
    merged = merged + gate(2) * _dot(y_xa.astype(BF16), wx_ref[...])

    mix = _dot(merged.astype(BF16), wo_ref[...])
    x1 = _layer_norm(np.float32(DEEPNORM_ALPHA) * x + mix, ln1g_ref[...], ln1b_ref[...])
    x1_ref[0] = x1

    x1h = x1.astype(BF16)
    x1l = (x1 - x1h.astype(F32)).astype(BF16)
    logit_ref[0] = (_dot(x1h, wrh_ref[...]) + _dot(x1l, wrh_ref[...]) + _dot(x1h, wrl_ref[...])
                    + br_ref[...])


def _mix_call(x, win, bgate, convw, ws, gbias, glng, glnb, k, v, wc, wg, wx, wo, ln1g, ln1b,
              wrh, wrl, br):
    bsz, seq, d = x.shape
    tm = min(TM_MIX, seq)
    assert seq % tm == 0 and tm % CHUNK == 0
    tile = lambda b, s: (b, s, 0)
    per_batch = lambda b, s: (b, 0, 0)
    return pl.pallas_call(
        _mix_kernel,
        grid=(bsz, seq // tm),
        in_specs=[
            pl.BlockSpec((1, tm, d), tile),
            _const_spec((d, W_IN_COLS)),
            _const_spec((1, 3 * d)),
            _const_spec((3, d)),
            _const_spec((GMLP_GROUPS, CHUNK, CHUNK)),
            _const_spec((CHUNK, d)),
            _const_spec((1, d)),
            _const_spec((1, d)),
            pl.BlockSpec((1, MEM_LEN, d), per_batch),
            pl.BlockSpec((1, MEM_LEN, d), per_batch),
            _const_spec((d, d)),
            _const_spec((d, d)),
            _const_spec((d, d)),
            _const_spec((d, d)),
            _const_spec((1, d)),
            _const_spec((1, d)),
            _const_spec((d, LANES)),
            _const_spec((d, LANES)),
            _const_spec((1, LANES)),
        ],
        out_specs=[pl.BlockSpec((1, tm, d), tile), pl.BlockSpec((1, tm, LANES), tile)],
        out_shape=[jax.ShapeDtypeStruct((bsz, seq, d), F32),
                   jax.ShapeDtypeStruct((bsz, seq, LANES), F32)],
        scratch_shapes=[pltpu.VMEM((tm + 2 * SUBLANES, d), F32)],
        compiler_params=pltpu.CompilerParams(
            dimension_semantics=("arbitrary", "arbitrary"), vmem_limit_bytes=VMEM_LIMIT_BYTES),
        name="mix",
    )(x, win, bgate, convw, ws, gbias, glng, glnb, k, v, wc, wg, wx, wo, ln1g, ln1b, wrh, wrl, br)


def _route_kernel(logit_ref, pos_ref, gate_ref, cnt_ref):
    tm = logit_ref.shape[0]
    lt = logit_ref[...].T
    e_io = lax.broadcasted_iota(I32, (LANES, tm), 0)
    neg = np.float32(-np.inf)
    work = jnp.where(e_io < N_EXPERTS, lt, neg)
    vals, sels = [], []
    for _ in range(TOP_K):
        m = jnp.max(work, axis=0, keepdims=True)
        idx = jnp.min(jnp.where(work == m, e_io, LANES), axis=0, keepdims=True)
        sel = e_io == idx
        vals.append(m)
        sels.append(sel)
        work = jnp.where(sel, neg, work)
    ex = [jnp.exp(vk - vals[0]) for vk in vals]
    inv = 1.0 / (ex[0] + ex[1] + ex[2] + ex[3])
    gates = [e * inv for e in ex]

    oh = jnp.where(sels[0] | sels[1] | sels[2] | sels[3], 1.0, 0.0).astype(F32)
    cnt = jnp.sum(oh, axis=1, keepdims=True)
    t_r = lax.broadcasted_iota(I32, (tm, tm), 0)
    t_c = lax.broadcasted_iota(I32, (tm, tm), 1)
    before = jnp.where(t_r < t_c, 1.0, 0.0).astype(BF16)
    rank = _dot(oh.astype(BF16), before)
    units = jnp.floor((cnt + np.float32(SEG_ALIGN - 1)) * np.float32(1.0 / SEG_ALIGN))
    units_b = jnp.broadcast_to(units, (LANES, LANES)).astype(BF16)
    e_r = lax.broadcasted_iota(I32, (LANES, LANES), 0)
    e_c = lax.broadcasted_iota(I32, (LANES, LANES), 1)
    below = jnp.where(e_c < e_r, 1.0, 0.0).astype(BF16)
    seg = _dot(below, units_b)[:, 0:1] * np.float32(SEG_ALIGN)
    slot = seg + rank

    pos_rows = [jnp.sum(jnp.where(s, slot, 0.0), axis=0, keepdims=True) for s in sels]
    zero_row = jnp.zeros((1, tm), F32)
    pos_ref[0] = jnp.concatenate(pos_rows + [zero_row] * (SUBLANES - TOP_K), axis=0).astype(I32)
    gate_ref[0] = jnp.concatenate(gates + [zero_row] * (SUBLANES - TOP_K), axis=0)
    cnt_ref[0] = jnp.broadcast_to(cnt, (LANES, LANES)).astype(I32)


def _route_call(logits2d):
    n_tok = logits2d.shape[0]
    nt = n_tok // TM_SORT
    return pl.pallas_call(
        _route_kernel,
        grid=(nt,),
        in_specs=[pl.BlockSpec((TM_SORT, LANES), lambda i: (i, 0))],
        out_specs=[
            pl.BlockSpec((1, SUBLANES, TM_SORT), lambda i: (i, 0, 0)),
            pl.BlockSpec((1, SUBLANES, TM_SORT), lambda i: (i, 0, 0)),
            pl.BlockSpec((1, LANES, LANES), lambda i: (i, 0, 0)),
        ],
        out_shape=[
            jax.ShapeDtypeStruct((nt, SUBLANES, TM_SORT), I32),
            jax.ShapeDtypeStruct((nt, SUBLANES, TM_SORT), F32),
            jax.ShapeDtypeStruct((nt, LANES, LANES), I32),
        ],
        compiler_params=pltpu.CompilerParams(dimension_semantics=("arbitrary",)),
        name="route",
    )(logits2d)


def _chunk_copy(src_ref, src_row, dst_ref, dst_row, rows, sem):
    return pltpu.make_async_copy(
        src_ref.at[pl.ds(src_row, rows)], dst_ref.at[pl.ds(dst_row, rows)], sem)


def _sort_kernel(c8_ref, seg_ref, goff_ref, tot8_ref, pused_ref,
                 x1_ref, pos_ref, gate_ref, xs_hbm, sbuf, zbuf, sem):
    j = pl.program_id(0)
    nt = pl.num_programs(0)
    tm = x1_ref.shape[0]
    pmax = xs_hbm.shape[0]
    j_io = lax.broadcasted_iota(I32, (L_SORT, tm), 0)
    hit = [j_io == pos_ref[0, k:k + 1, :] for k in range(TOP_K)]
    onehot = jnp.where(hit[0] | hit[1] | hit[2] | hit[3], 1.0, 0.0).astype(BF16)
    sbuf[:, 0:D_MODEL] = _dot(onehot, x1_ref[...].astype(BF16))
    gsel = jnp.where(hit[0], gate_ref[0, 0:1, :], 0.0)
    for k in range(1, TOP_K):
        gsel = gsel + jnp.where(hit[k], gate_ref[0, k:k + 1, :], 0.0)
    sbuf[:, D_MODEL:XS_COLS] = jnp.broadcast_to(
        jnp.sum(gsel, axis=1, keepdims=True), (L_SORT, LANES))

    for e in range(N_EXPERTS):
        n_chunks = c8_ref[j * N_EXPERTS + e] // SEG_ALIGN
        src0 = seg_ref[j * N_EXPERTS + e]
        dst0 = goff_ref[j * N_EXPERTS + e]

        def issue(i, carry, src0=src0, dst0=dst0):
            _chunk_copy(sbuf, pl.multiple_of(src0 + i * SEG_ALIGN, SEG_ALIGN),
                        xs_hbm, pl.multiple_of(dst0 + i * SEG_ALIGN, SEG_ALIGN),
                        SEG_ALIGN, sem).start()
            return carry

        lax.fori_loop(0, n_chunks, issue, 0)

    def drain(i, carry):
        _chunk_copy(sbuf, 0, xs_hbm, 0, SEG_ALIGN, sem).wait()
        return carry

    lax.fori_loop(0, tot8_ref[j] // SEG_ALIGN, drain, 0)

    @pl.when(j == nt - 1)
    def _():
        zbuf[...] = jnp.zeros(zbuf.shape, F32)
        p_used = pused_ref[0]
        p_round = jnp.minimum((p_used + ZROWS - 1) // ZROWS * ZROWS, pmax)
        n_small = (p_round - p_used) // SEG_ALIGN
        n_big = (pmax - p_round) // ZROWS

        def small(i, carry):
            cp = _chunk_copy(zbuf, 0, xs_hbm, pl.multiple_of(p_used + i * SEG_ALIGN, SEG_ALIGN),
                             SEG_ALIGN, sem)
            cp.start()
            cp.wait()
            return carry

        def big(i, carry):
            cp = _chunk_copy(zbuf, 0, xs_hbm, pl.multiple_of(p_round + i * ZROWS, ZROWS),
                             ZROWS, sem)
            cp.start()
            cp.wait()
            return carry

        lax.fori_loop(0, n_small, small, 0)
        lax.fori_loop(0, n_big, big, 0)


def _sort_call(c8, seg, goff, tot8, pused, x1, pos, gate, pmax):
    n_tok = x1.shape[0]
    nt = n_tok // TM_SORT
    grid_spec = pltpu.PrefetchScalarGridSpec(
        num_scalar_prefetch=5,
        grid=(nt,),
        in_specs=[
            pl.BlockSpec((TM_SORT, D_MODEL), lambda i, *_: (i, 0)),
            pl.BlockSpec((1, SUBLANES, TM_SORT), lambda i, *_: (i, 0, 0)),
            pl.BlockSpec((1, SUBLANES, TM_SORT), lambda i, *_: (i, 0, 0)),
        ],
        out_specs=pl.BlockSpec(memory_space=pl.ANY),
        scratch_shapes=[
            pltpu.VMEM((L_SORT, XS_COLS), F32),
            pltpu.VMEM((ZROWS, XS_COLS), F32),
            pltpu.SemaphoreType.DMA(()),
        ],
    )
    return pl.pallas_call(
        _sort_kernel,
        grid_spec=grid_spec,
        out_shape=jax.ShapeDtypeStruct((pmax, XS_COLS), F32),
        compiler_params=pltpu.CompilerParams(dimension_semantics=("arbitrary",)),
        name="sort",
    )(c8, seg, goff, tot8, pused, x1, pos, gate)


KIND_NOOP, KIND_FFN, KIND_ZERO = 0, 1, 2


def _expert_kernel(tile_ref, exp_ref, kind_ref, lo_ref, hi_ref, first_ref,
                   xs_ref, wgu_ref, bgu_ref, wd_ref, bd_ref, out_ref):
    w = pl.program_id(0)
    kind = kind_ref[w]

    @pl.when(kind == KIND_ZERO)
    def _():
        out_ref[...] = jnp.zeros(out_ref.shape, F32)

    @pl.when(kind == KIND_FFN)
    def _():
        xs = xs_ref[:, 0:D_MODEL].astype(BF16)
        gate = xs_ref[:, D_MODEL:D_MODEL + 1]
        gu = _dot(xs, wgu_ref[0]) + bgu_ref[0]
        g_lin = jnp.minimum(gu[:, :D_FF], np.float32(SWIGLU_LIMIT))
        u_lin = jnp.clip(gu[:, D_FF:], np.float32(-SWIGLU_LIMIT), np.float32(SWIGLU_LIMIT))
        act = (u_lin + 1.0) * (g_lin * _sigmoid(np.float32(SWIGLU_ALPHA) * g_lin))
        eo = (_dot(act.astype(BF16), wd_ref[0]) + bd_ref[0]) * gate
        rows = lax.broadcasted_iota(I32, (TR, 1), 0)
        mine = (rows >= lo_ref[w]) & (rows < hi_ref[w])

        @pl.when(first_ref[w] == 1)
        def _():
            out_ref[...] = jnp.where(mine, eo, 0.0)

        @pl.when(first_ref[w] == 0)
        def _():
            out_ref[...] = jnp.where(mine, eo, out_ref[...])


def _expert_call(tile, exp, kind, lo, hi, first, xs, wgu, bgu, wd, bd):
    pmax = xs.shape[0]
    n_items = tile.shape[0]
    grid_spec = pltpu.PrefetchScalarGridSpec(
        num_scalar_prefetch=6,
        grid=(n_items,),
        in_specs=[
            pl.BlockSpec((TR, XS_COLS), lambda w, t, e, *_: (t[w], 0)),
            pl.BlockSpec((1, D_MODEL, 2 * D_FF), lambda w, t, e, *_: (e[w], 0, 0)),
            pl.BlockSpec((1, 1, 2 * D_FF), lambda w, t, e, *_: (e[w], 0, 0)),
            pl.BlockSpec((1, D_FF, D_MODEL), lambda w, t, e, *_: (e[w], 0, 0)),
            pl.BlockSpec((1, 1, D_MODEL), lambda w, t, e, *_: (e[w], 0, 0)),
        ],
        out_specs=pl.BlockSpec((TR, D_MODEL), lambda w, t, e, *_: (t[w], 0)),
    )
    return pl.pallas_call(
        _expert_kernel,
        grid_spec=grid_spec,
        out_shape=jax.ShapeDtypeStruct((pmax, D_MODEL), F32),
        compiler_params=pltpu.CompilerParams(
            dimension_semantics=("arbitrary",), vmem_limit_bytes=VMEM_LIMIT_BYTES),
        name="experts",
    )(tile, exp, kind, lo, hi, first, xs, wgu, bgu, wd, bd)


def _combine_kernel(c8_ref, seg_ref, goff_ref, tot8_ref,
                    x1_ref, post_ref, g_ref, b_ref, eo_hbm, out_ref, ebuf, sem):
    j = pl.program_id(0)
    tm = x1_ref.shape[0]

    for e in range(N_EXPERTS):
        n_chunks = c8_ref[j * N_EXPERTS + e] // SEG_ALIGN
        dst0 = seg_ref[j * N_EXPERTS + e]
        src0 = goff_ref[j * N_EXPERTS + e]

        def issue(i, carry, src0=src0, dst0=dst0):
            _chunk_copy(eo_hbm, pl.multiple_of(src0 + i * SEG_ALIGN, SEG_ALIGN),
                        ebuf, pl.multiple_of(dst0 + i * SEG_ALIGN, SEG_ALIGN),
                        SEG_ALIGN, sem).start()
            return carry

        lax.fori_loop(0, n_chunks, issue, 0)

    n_used = tot8_ref[j] // SEG_ALIGN

    def clear(i, carry):
        ebuf[pl.ds(pl.multiple_of(i * SEG_ALIGN, SEG_ALIGN), SEG_ALIGN), :] = jnp.zeros(
            (SEG_ALIGN, D_MODEL), F32)
        return carry

    lax.fori_loop(n_used, L_SORT // SEG_ALIGN, clear, 0)

    def drain(i, carry):
        _chunk_copy(eo_hbm, 0, ebuf, 0, SEG_ALIGN, sem).wait()
        return carry

    lax.fori_loop(0, n_used, drain, 0)

    l_io = lax.broadcasted_iota(I32, (tm, L_SORT), 1)
    hit = l_io == post_ref[0, :, 0:1]
    for k in range(1, TOP_K):
        hit = hit | (l_io == post_ref[0, :, k:k + 1])
    onehot = jnp.where(hit, 1.0, 0.0).astype(BF16)
    y = _dot(onehot, ebuf[...].astype(BF16))
    out_ref[...] = _layer_norm(np.float32(DEEPNORM_ALPHA) * x1_ref[...] + y, g_ref[...], b_ref[...])


def _combine_call(c8, seg, goff, tot8, x1, post, g, b, eo):
    n_tok = x1.shape[0]
    nt = n_tok // TM_SORT
    grid_spec = pltpu.PrefetchScalarGridSpec(
        num_scalar_prefetch=4,
        grid=(nt,),
        in_specs=[
            pl.BlockSpec((TM_SORT, D_MODEL), lambda i, *_: (i, 0)),
            pl.BlockSpec((1, TM_SORT, SUBLANES), lambda i, *_: (i, 0, 0)),
            pl.BlockSpec((1, D_MODEL), lambda i, *_: (0, 0)),
            pl.BlockSpec((1, D_MODEL), lambda i, *_: (0, 0)),
            pl.BlockSpec(memory_space=pl.ANY),
        ],
        out_specs=pl.BlockSpec((TM_SORT, D_MODEL), lambda i, *_: (i, 0)),
        scratch_shapes=[pltpu.VMEM((L_SORT, D_MODEL), F32), pltpu.SemaphoreType.DMA(())],
    )
    return pl.pallas_call(
        _combine_kernel,
        grid_spec=grid_spec,
        out_shape=jax.ShapeDtypeStruct((n_tok, D_MODEL), F32),
        compiler_params=pltpu.CompilerParams(dimension_semantics=("arbitrary",)),
        name="combine",
    )(c8, seg, goff, tot8, x1, post, g, b, eo)


def _routing_metadata(cnt, pmax):
    c8 = (cnt + (SEG_ALIGN - 1)) // SEG_ALIGN * SEG_ALIGN
    seg = jnp.cumsum(c8, axis=1) - c8
    tot8 = jnp.sum(c8, axis=1)
    n8 = jnp.sum(c8, axis=0)
    base = jnp.cumsum(n8) - n8
    goff = base[None, :] + jnp.cumsum(c8, axis=0) - c8
    p_used = jnp.sum(n8)

    n_tiles = pmax // TR
    n_items = n_tiles + N_EXPERTS
    e_lo, e_hi = base, base + n8
    t_first = e_lo // TR
    t_end = (e_hi + TR - 1) // TR
    w_e = jnp.where(n8 > 0, t_end - t_first, 0)
    w_cum = jnp.cumsum(w_e)
    w_start = w_cum - w_e
    w_total = w_cum[-1]
    tiles_used = (p_used + TR - 1) // TR

    w = jnp.arange(n_items, dtype=I32)
    e_w = jnp.minimum(jnp.searchsorted(w_cum, w, side="right"), N_EXPERTS - 1).astype(I32)
    is_ffn = w < w_total
    ffn_tile = t_first[e_w] + (w - w_start[e_w])
    zero_tile = tiles_used + (w - w_total)
    is_zero = (~is_ffn) & (zero_tile < n_tiles)
    last_e = jnp.max(jnp.where(n8 > 0, jnp.arange(N_EXPERTS, dtype=I32), 0))
    tile = jnp.where(is_ffn, ffn_tile, jnp.where(is_zero, zero_tile, n_tiles - 1)).astype(I32)
    exp = jnp.where(is_ffn, e_w, last_e).astype(I32)
    kind = jnp.where(is_ffn, KIND_FFN, jnp.where(is_zero, KIND_ZERO, KIND_NOOP)).astype(I32)
    lo = jnp.clip(e_lo[exp] - tile * TR, 0, TR).astype(I32)
    hi = jnp.clip(e_hi[exp] - tile * TR, 0, TR).astype(I32)
    prev_tile = jnp.concatenate([jnp.full((1,), -1, I32), tile[:-1]])
    first = (tile != prev_tile).astype(I32)
    flat = lambda a: a.reshape(-1).astype(I32)
    return dict(c8=flat(c8), seg=flat(seg), goff=flat(goff), tot8=flat(tot8),
                pused=p_used.reshape(1).astype(I32),
                tile=tile, exp=exp, kind=kind, lo=lo, hi=hi, first=first)


def kernel(x, mem, w_in, b_gate, conv_w, gmlp_ws, gmlp_b, gmlp_ln_g, gmlp_ln_b, mem_ln_g,
           mem_ln_b, w_kv, w_conv_proj, w_gmlp_proj, w_xa_proj, w_out, ln1_g, ln1_b, w_router,
           b_router, w_gate_up, b_gate_up, w_down, b_down, ln2_g, ln2_b):
    bsz, seq, d = x.shape
    n_tok = bsz * seq
    assert d == D_MODEL and n_tok % TM_SORT == 0
    depth = w_in.shape[0]
    for l in range(depth):
        row = lambda a: a[l].reshape(1, -1)
        k, v = _kv_call(mem, row(mem_ln_g), row(mem_ln_b), w_kv[l].astype(BF16))
        gbias = jnp.repeat(gmlp_b[l].T, GROUP_CH, axis=1)
        wr = jnp.pad(w_router[l], ((0, 0), (0, LANES - N_EXPERTS)))
        wrh = wr.astype(BF16)
        wrl = (wr - wrh.astype(F32)).astype(BF16)
        br = jnp.pad(b_router[l], (0, LANES - N_EXPERTS)).reshape(1, LANES)
        x1, logits = _mix_call(
            x, w_in[l].astype(BF16), row(b_gate), conv_w[l], gmlp_ws[l], gbias,
            row(gmlp_ln_g), row(gmlp_ln_b), k, v, w_conv_proj[l].astype(BF16),
            w_gmlp_proj[l].astype(BF16), w_xa_proj[l].astype(BF16), w_out[l].astype(BF16),
            row(ln1_g), row(ln1_b), wrh, wrl, br)
        x1 = x1.reshape(n_tok, d)
        pos, gate, cnt = _route_call(logits.reshape(n_tok, LANES))

        nt = n_tok // TM_SORT
        pmax = TOP_K * n_tok + nt * N_EXPERTS * (SEG_ALIGN - 1)
        pmax = (pmax + ZROWS - 1) // ZROWS * ZROWS
        md = _routing_metadata(cnt[:, :N_EXPERTS, 0], pmax)
        xs = _sort_call(md["c8"], md["seg"], md["goff"], md["tot8"], md["pused"], x1, pos, gate, pmax)
        eo = _expert_call(md["tile"], md["exp"], md["kind"], md["lo"], md["hi"], md["first"], xs,
                          w_gate_up[l].astype(BF16), b_gate_up[l][:, None, :],
                          w_down[l].astype(BF16), b_down[l][:, None, :])
        post = jnp.transpose(pos, (0, 2, 1))
        x = _combine_call(md["c8"], md["seg"], md["goff"], md["tot8"], x1, post,
                          row(ln2_g), row(ln2_b), eo).reshape(bsz, seq, d)
    return x
```

```python
import jax
import jax.numpy as jnp
import numpy as np
from jax import lax
from jax.experimental import pallas as pl
from jax.experimental.pallas import tpu as pltpu

F32 = jnp.float32
BF16 = jnp.bfloat16
I32 = jnp.int32

D_MODEL = 1024
CHUNK = 128
GMLP_GROUPS = 8
GROUP_CH = D_MODEL // GMLP_GROUPS
MEM_LEN = 256
XA_HEADS = 4
XA_HEAD_DIM = D_MODEL // XA_HEADS
N_EXPERTS = 32
TOP_K = 4
D_FF = D_MODEL
W_IN_COLS = 9 * D_MODEL
SWIGLU_LIMIT = 7.0
SWIGLU_ALPHA = 1.702
LN_EPS = 1e-5
DEEPNORM_ALPHA = 2.0 ** 0.25

LANES = 128
SUBLANES = 8
VMEM_LIMIT_BYTES = 60000 * 1024

TM_MIX = 512
TM_SORT = 256
SEG_ALIGN = SUBLANES
L_SORT = 1280
assert L_SORT >= TOP_K * TM_SORT + N_EXPERTS * (SEG_ALIGN - 1) and L_SORT % 256 == 0
N_CHUNKS = L_SORT // SEG_ALIGN
XS_COLS = D_MODEL + LANES
TR = 256
assert L_SORT % TR == 0


def _layer_norm(x, g, b):
    mu = jnp.mean(x, axis=-1, keepdims=True)
    xc = x - mu
    var = jnp.mean(xc * xc, axis=-1, keepdims=True)
    return xc * lax.rsqrt(var + LN_EPS) * g + b


def _gelu_tanh(x):
    c = np.float32(np.sqrt(2.0 / np.pi))
    return x * (0.5 * (1.0 + jnp.tanh(c * (x + np.float32(0.044715) * (x * x * x)))))


def _sigmoid(x):
    return 1.0 / (1.0 + jnp.exp(-x))


def _dot(a, b):
    return jnp.dot(a, b, preferred_element_type=F32)


def _const_spec(shape):
    n = len(shape)
    return pl.BlockSpec(shape, lambda *_: (0,) * n, pipeline_mode=pl.Buffered(1))


def _kv_kernel(mem_ref, g_ref, b_ref, wkv_ref, k_ref, v_ref):
    mem_n = _layer_norm(mem_ref[0], g_ref[...], b_ref[...])
    kv = _dot(mem_n.astype(BF16), wkv_ref[...])
    k_ref[0] = kv[:, :D_MODEL].astype(BF16)
    v_ref[0] = kv[:, D_MODEL:].astype(BF16)


def _kv_call(mem, g, b, wkv):
    bsz = mem.shape[0]
    return pl.pallas_call(
        _kv_kernel,
        grid=(bsz,),
        in_specs=[
            pl.BlockSpec((1, MEM_LEN, D_MODEL), lambda i: (i, 0, 0)),
            pl.BlockSpec((1, D_MODEL), lambda i: (0, 0)),
            pl.BlockSpec((1, D_MODEL), lambda i: (0, 0)),
            pl.BlockSpec((D_MODEL, 2 * D_MODEL), lambda i: (0, 0)),
        ],
        out_specs=[
            pl.BlockSpec((1, MEM_LEN, D_MODEL), lambda i: (i, 0, 0)),
            pl.BlockSpec((1, MEM_LEN, D_MODEL), lambda i: (i, 0, 0)),
        ],
        out_shape=[
            jax.ShapeDtypeStruct((bsz, MEM_LEN, D_MODEL), BF16),
            jax.ShapeDtypeStruct((bsz, MEM_LEN, D_MODEL), BF16),
        ],
        compiler_params=pltpu.CompilerParams(dimension_semantics=("arbitrary",)),
        name="kv",
    )(mem, g, b, wkv)


def _mix_kernel(x_ref, win_ref, bgate_ref, convw_ref, ws_ref, gbias_ref, glng_ref, glnb_ref,
                k_ref, v_ref, wc_ref, wg_ref, wx_ref, wo_ref, ln1g_ref, ln1b_ref,
                wrh_ref, wrl_ref, br_ref, x1_ref, logit_ref, ubuf_ref):
    tm = x_ref.shape[1]
    d = D_MODEL
    x = x_ref[0]
    xb = x.astype(BF16)

    def proj(i):
        return _dot(xb, win_ref[:, i * d:(i + 1) * d])

    def gate(i):
        return _sigmoid(proj(6 + i) + bgate_ref[:, i * d:(i + 1) * d])

    @pl.when(pl.program_id(1) == 0)
    def _():
        ubuf_ref[0:SUBLANES, :] = jnp.zeros((SUBLANES, d), F32)

    u = proj(1) * proj(2)
    ubuf_ref[SUBLANES:SUBLANES + tm, :] = u
    u1 = ubuf_ref[SUBLANES - 1:SUBLANES - 1 + tm, :]
    u2 = ubuf_ref[SUBLANES - 2:SUBLANES - 2 + tm, :]
    conv = convw_ref[0:1, :] * u2 + convw_ref[1:2, :] * u1 + convw_ref[2:3, :] * u
    ubuf_ref[0:SUBLANES, :] = ubuf_ref[tm:tm + SUBLANES, :]
    y_conv = proj(0) * conv
    merged = gate(0) * _dot(y_conv.astype(BF16), wc_ref[...])

    gu = _gelu_tanh(proj(3))
    gv = _gelu_tanh(proj(4))
    vn = _layer_norm(gv, glng_ref[...], glnb_ref[...]).astype(BF16)
    n_chunks = tm // CHUNK
    row_i = lax.broadcasted_iota(I32, (CHUNK, CHUNK), 0)
    col_i = lax.broadcasted_iota(I32, (CHUNK, CHUNK), 1)
    causal = col_i <= row_i
    f_cols = []
    for g in range(GMLP_GROUPS):
        w_g = jnp.where(causal, ws_ref[g], 0.0).astype(BF16)
        rhs = jnp.concatenate(
            [vn[c * CHUNK:(c + 1) * CHUNK, g * GROUP_CH:(g + 1) * GROUP_CH] for c in range(n_chunks)],
            axis=1)
        fg = _dot(w_g, rhs)
        f_cols.append(jnp.concatenate(
            [fg[:, c * GROUP_CH:(c + 1) * GROUP_CH] for c in range(n_chunks)], axis=0))
    f = jnp.concatenate(f_cols, axis=1)
    gbias = jnp.concatenate([gbias_ref[...]] * n_chunks, axis=0)
    y_gmlp = gu * (f + gbias)
    merged = merged + gate(1) * _dot(y_gmlp.astype(BF16), wg_ref[...])

    q = proj(5).astype(BF16)
    heads = []
    for h in range(XA_HEADS):
        sl = slice(h * XA_HEAD_DIM, (h + 1) * XA_HEAD_DIM)
        s = lax.dot_general(q[:, sl], k_ref[0, :, sl], (((1,), (1,)), ((), ())),
                            preferred_element_type=F32) * np.float32(XA_HEAD_DIM ** -0.5)
        p = jnp.exp(s - jnp.max(s, axis=-1, keepdims=True))
        p = p * (1.0 / jnp.sum(p, axis=-1, keepdims=True))
        heads.append(_dot(p.astype(BF16), v_ref[0, :, sl]))
    y_xa = jnp.concatenate(heads, axis=1)
    merged = merged + gate(2) * _dot(y_xa.astype(BF16), wx_ref[...])

    mix = _dot(merged.astype(BF16), wo_ref[...])
    x1 = _layer_norm(np.float32(DEEPNORM_ALPHA) * x + mix, ln1g_ref[...], ln1b_ref[...])
    x1_ref[0] = x1

    x1h = x1.astype(BF16)
    x1l = (x1 - x1h.astype(F32)).astype(BF16)
    logit_ref[0] = (_dot(x1h, wrh_ref[...]) + _dot(x1l, wrh_ref[...]) + _dot(x1h, wrl_ref[...])
                    + br_ref[...])


def _mix_call(x, win, bgate, convw, ws, gbias, glng, glnb, k, v, wc, wg, wx, wo, ln1g, ln1b,
              wrh, wrl, br):
    bsz, seq, d = x.shape
    tm = min(TM_MIX, seq)
    assert seq % tm == 0 and tm % CHUNK == 0
    tile = lambda b, s: (b, s, 0)
    per_batch = lambda b, s: (b, 0, 0)
    return pl.pallas_call(
        _mix_kernel,
        grid=(bsz, seq // tm),
        in_specs=[
            pl.BlockSpec((1, tm, d), tile),
            _const_spec((d, W_IN_COLS)),
            _const_spec((1, 3 * d)),
            _const_spec((3, d)),
            _const_spec((GMLP_GROUPS, CHUNK, CHUNK)),
            _const_spec((CHUNK, d)),
            _const_spec((1, d)),
            _const_spec((1, d)),
            pl.BlockSpec((1, MEM_LEN, d), per_batch),
            pl.BlockSpec((1, MEM_LEN, d), per_batch),
            _const_spec((d, d)),
            _const_spec((d, d)),
            _const_spec((d, d)),
            _const_spec((d, d)),
            _const_spec((1, d)),
            _const_spec((1, d)),
            _const_spec((d, LANES)),
            _const_spec((d, LANES)),
            _const_spec((1, LANES)),
        ],
        out_specs=[pl.BlockSpec((1, tm, d), tile), pl.BlockSpec((1, tm, LANES), tile)],
        out_shape=[jax.ShapeDtypeStruct((bsz, seq, d), F32),
                   jax.ShapeDtypeStruct((bsz, seq, LANES), F32)],
        scratch_shapes=[pltpu.VMEM((tm + 2 * SUBLANES, d), F32)],
        compiler_params=pltpu.CompilerParams(
            dimension_semantics=("arbitrary", "arbitrary"), vmem_limit_bytes=VMEM_LIMIT_BYTES),
        name="mix",
    )(x, win, bgate, convw, ws, gbias, glng, glnb, k, v, wc, wg, wx, wo, ln1g, ln1b, wrh, wrl, br)


def _route_kernel(logit_ref, pos_ref, gate_ref, cnt_ref):
    tm = logit_ref.shape[0]
    lt = logit_ref[...].T
    e_io = lax.broadcasted_iota(I32, (LANES, tm), 0)
    neg = np.float32(-np.inf)
    work = jnp.where(e_io < N_EXPERTS, lt, neg)
    vals, sels = [], []
    for _ in range(TOP_K):
        m = jnp.max(work, axis=0, keepdims=True)
        idx = jnp.min(jnp.where(work == m, e_io, LANES), axis=0, keepdims=True)
        sel = e_io == idx
        vals.append(m)
        sels.append(sel)
        work = jnp.where(sel, neg, work)
    ex = [jnp.exp(vk - vals[0]) for vk in vals]
    inv = 1.0 / (ex[0] + ex[1] + ex[2] + ex[3])
    gates = [e * inv for e in ex]

    oh = jnp.where(sels[0] | sels[1] | sels[2] | sels[3], 1.0, 0.0).astype(F32)
    cnt = jnp.sum(oh, axis=1, keepdims=True)
    t_r = lax.broadcasted_iota(I32, (tm, tm), 0)
    t_c = lax.broadcasted_iota(I32, (tm, tm), 1)
    before = jnp.where(t_r < t_c, 1.0, 0.0).astype(BF16)
    rank = _dot(oh.astype(BF16), before)
    units = jnp.floor((cnt + np.float32(SEG_ALIGN - 1)) * np.float32(1.0 / SEG_ALIGN))
    units_b = jnp.broadcast_to(units, (LANES, LANES)).astype(BF16)
    e_r = lax.broadcasted_iota(I32, (LANES, LANES), 0)
    e_c = lax.broadcasted_iota(I32, (LANES, LANES), 1)
    below = jnp.where(e_c < e_r, 1.0, 0.0).astype(BF16)
    seg = _dot(below, units_b)[:, 0:1] * np.float32(SEG_ALIGN)
    slot = seg + rank

    pos_rows = [jnp.sum(jnp.where(s, slot, 0.0), axis=0, keepdims=True) for s in sels]
    zero_row = jnp.zeros((1, tm), F32)
    pos_ref[0] = jnp.concatenate(pos_rows + [zero_row] * (SUBLANES - TOP_K), axis=0).astype(I32)
    gate_ref[0] = jnp.concatenate(gates + [zero_row] * (SUBLANES - TOP_K), axis=0)
    cnt_ref[0] = jnp.broadcast_to(cnt, (LANES, LANES)).astype(I32)


def _route_call(logits2d):
    n_tok = logits2d.shape[0]
    nt = n_tok // TM_SORT
    return pl.pallas_call(
        _route_kernel,
        grid=(nt,),
        in_specs=[pl.BlockSpec((TM_SORT, LANES), lambda i: (i, 0))],
        out_specs=[
            pl.BlockSpec((1, SUBLANES, TM_SORT), lambda i: (i, 0, 0)),
            pl.BlockSpec((1, SUBLANES, TM_SORT), lambda i: (i, 0, 0)),
            pl.BlockSpec((1, LANES, LANES), lambda i: (i, 0, 0)),
        ],
        out_shape=[
            jax.ShapeDtypeStruct((nt, SUBLANES, TM_SORT), I32),
            jax.ShapeDtypeStruct((nt, SUBLANES, TM_SORT), F32),
            jax.ShapeDtypeStruct((nt, LANES, LANES), I32),
        ],
        compiler_params=pltpu.CompilerParams(dimension_semantics=("arbitrary",)),
        name="route",
    )(logits2d)


def _chunk_copy(src_ref, src_row, dst_ref, dst_row, sem):
    return pltpu.make_async_copy(
        src_ref.at[pl.ds(src_row, SEG_ALIGN)], dst_ref.at[pl.ds(dst_row, SEG_ALIGN)], sem)


def _sort_kernel(slot_ref, x1_ref, pos_ref, gate_ref, xs_hbm, sbuf, sems):
    j = pl.program_id(0)
    nt = pl.num_programs(0)
    tm = x1_ref.shape[0]
    cur = lax.rem(j, 2)
    buf = sbuf.at[cur]
    j_io = lax.broadcasted_iota(I32, (L_SORT, tm), 0)
    hit = [j_io == pos_ref[0, k:k + 1, :] for k in range(TOP_K)]
    onehot = jnp.where(hit[0] | hit[1] | hit[2] | hit[3], 1.0, 0.0).astype(BF16)
    buf[:, 0:D_MODEL] = _dot(onehot, x1_ref[...].astype(BF16))
    gsel = jnp.where(hit[0], gate_ref[0, 0:1, :], 0.0)
    for k in range(1, TOP_K):
        gsel = gsel + jnp.where(hit[k], gate_ref[0, k:k + 1, :], 0.0)
    buf[:, D_MODEL:XS_COLS] = jnp.broadcast_to(
        jnp.sum(gsel, axis=1, keepdims=True), (L_SORT, LANES))

    for i in range(N_CHUNKS):
        dst = pl.multiple_of(slot_ref[j * N_CHUNKS + i] * SEG_ALIGN, SEG_ALIGN)
        _chunk_copy(buf, i * SEG_ALIGN, xs_hbm, dst, sems.at[cur]).start()

    def drain(which):
        for _ in range(N_CHUNKS):
            _chunk_copy(sbuf.at[which], 0, xs_hbm, 0, sems.at[which]).wait()

    @pl.when(j > 0)
    def _():
        drain(1 - cur)

    @pl.when(j == nt - 1)
    def _():
        drain(cur)


def _sort_call(slots, x1, pos, gate):
    n_tok = x1.shape[0]
    nt = n_tok // TM_SORT
    grid_spec = pltpu.PrefetchScalarGridSpec(
        num_scalar_prefetch=1,
        grid=(nt,),
        in_specs=[
            pl.BlockSpec((TM_SORT, D_MODEL), lambda i, *_: (i, 0)),
            pl.BlockSpec((1, SUBLANES, TM_SORT), lambda i, *_: (i, 0, 0)),
            pl.BlockSpec((1, SUBLANES, TM_SORT), lambda i, *_: (i, 0, 0)),
        ],
        out_specs=pl.BlockSpec(memory_space=pl.ANY),
        scratch_shapes=[
            pltpu.VMEM((2, L_SORT, XS_COLS), F32),
            pltpu.SemaphoreType.DMA((2,)),
        ],
    )
    return pl.pallas_call(
        _sort_kernel,
        grid_spec=grid_spec,
        out_shape=jax.ShapeDtypeStruct((nt * L_SORT, XS_COLS), F32),
        compiler_params=pltpu.CompilerParams(
            dimension_semantics=("arbitrary",), vmem_limit_bytes=VMEM_LIMIT_BYTES),
        name="sort",
    )(slots, x1, pos, gate)


KIND_NOOP, KIND_FFN, KIND_ZERO = 0, 1, 2


def _expert_kernel(tile_ref, exp_ref, kind_ref, lo_ref, hi_ref, first_ref, fresh_ref,
                   xs_ref, wgu_ref, bgu_ref, wd_ref, bd_ref, out_ref, wgu_b, wd_b):
    w = pl.program_id(0)
    kind = kind_ref[w]

    @pl.when(kind == KIND_ZERO)
    def _():
        out_ref[...] = jnp.zeros(out_ref.shape, F32)

    @pl.when((kind == KIND_FFN) & (fresh_ref[w] == 1))
    def _():
        wgu_b[...] = wgu_ref[0].astype(BF16)
        wd_b[...] = wd_ref[0].astype(BF16)

    @pl.when(kind == KIND_FFN)
    def _():
        xs = xs_ref[:, 0:D_MODEL].astype(BF16)
        gate = xs_ref[:, D_MODEL:D_MODEL + 1]
        gu = _dot(xs, wgu_b[...]) + bgu_ref[0]
        g_lin = jnp.minimum(gu[:, :D_FF], np.float32(SWIGLU_LIMIT))
        u_lin = jnp.clip(gu[:, D_FF:], np.float32(-SWIGLU_LIMIT), np.float32(SWIGLU_LIMIT))
        act = (u_lin + 1.0) * (g_lin * _sigmoid(np.float32(SWIGLU_ALPHA) * g_lin))
        eo = (_dot(act.astype(BF16), wd_b[...]) + bd_ref[0]) * gate
        rows = lax.broadcasted_iota(I32, (TR, 1), 0)
        mine = (rows >= lo_ref[w]) & (rows < hi_ref[w])

        @pl.when(first_ref[w] == 1)
        def _():
            out_ref[...] = jnp.where(mine, eo, 0.0)

        @pl.when(first_ref[w] == 0)
        def _():
            out_ref[...] = jnp.where(mine, eo, out_ref[...])


def _expert_call(md, xs, wgu, bgu, wd, bd):
    pmax = xs.shape[0]
    n_items = md["tile"].shape[0]
    grid_spec = pltpu.PrefetchScalarGridSpec(
        num_scalar_prefetch=7,
        grid=(n_items,),
        in_specs=[
            pl.BlockSpec((TR, XS_COLS), lambda w, t, e, *_: (t[w], 0)),
            pl.BlockSpec((1, D_MODEL, 2 * D_FF), lambda w, t, e, *_: (e[w], 0, 0)),
            pl.BlockSpec((1, 1, 2 * D_FF), lambda w, t, e, *_: (e[w], 0, 0)),
            pl.BlockSpec((1, D_FF, D_MODEL), lambda w, t, e, *_: (e[w], 0, 0)),
            pl.BlockSpec((1, 1, D_MODEL), lambda w, t, e, *_: (e[w], 0, 0)),
        ],
        out_specs=pl.BlockSpec((TR, D_MODEL), lambda w, t, e, *_: (t[w], 0)),
        scratch_shapes=[pltpu.VMEM((D_MODEL, 2 * D_FF), BF16), pltpu.VMEM((D_FF, D_MODEL), BF16)],
    )
    return pl.pallas_call(
        _expert_kernel,
        grid_spec=grid_spec,
        out_shape=jax.ShapeDtypeStruct((pmax, D_MODEL), F32),
        compiler_params=pltpu.CompilerParams(
            dimension_semantics=("arbitrary",), vmem_limit_bytes=VMEM_LIMIT_BYTES),
        name="experts",
    )(md["tile"], md["exp"], md["kind"], md["lo"], md["hi"], md["first"], md["fresh"],
      xs, wgu, bgu, wd, bd)


def _combine_kernel(slot_ref, x1_ref, post_ref, g_ref, b_ref, eo_hbm, out_ref, ebuf, sems):
    j = pl.program_id(0)
    nt = pl.num_programs(0)
    tm = x1_ref.shape[0]
    cur = lax.rem(j, 2)

    def fetch(tile, which):
        for i in range(N_CHUNKS):
            src = pl.multiple_of(slot_ref[tile * N_CHUNKS + i] * SEG_ALIGN, SEG_ALIGN)
            _chunk_copy(eo_hbm, src, ebuf.at[which], i * SEG_ALIGN, sems.at[which]).start()

    @pl.when(j == 0)
    def _():
        fetch(0, 0)

    @pl.when(j + 1 < nt)
    def _():
        fetch(j + 1, 1 - cur)

    for _ in range(N_CHUNKS):
        _chunk_copy(eo_hbm, 0, ebuf.at[cur], 0, sems.at[cur]).wait()

    l_io = lax.broadcasted_iota(I32, (tm, L_SORT), 1)
    hit = l_io == post_ref[0, :, 0:1]
    for k in range(1, TOP_K):
        hit = hit | (l_io == post_ref[0, :, k:k + 1])
    onehot = jnp.where(hit, 1.0, 0.0).astype(BF16)
    y = _dot(onehot, ebuf[cur].astype(BF16))
    out_ref[...] = _layer_norm(np.float32(DEEPNORM_ALPHA) * x1_ref[...] + y, g_ref[...], b_ref[...])


def _combine_call(slots, x1, post, g, b, eo):
    n_tok = x1.shape[0]
    nt = n_tok // TM_SORT
    grid_spec = pltpu.PrefetchScalarGridSpec(
        num_scalar_prefetch=1,
        grid=(nt,),
        in_specs=[
            pl.BlockSpec((TM_SORT, D_MODEL), lambda i, *_: (i, 0)),
            pl.BlockSpec((1, TM_SORT, SUBLANES), lambda i, *_: (i, 0, 0)),
            pl.BlockSpec((1, D_MODEL), lambda i, *_: (0, 0)),
            pl.BlockSpec((1, D_MODEL), lambda i, *_: (0, 0)),
            pl.BlockSpec(memory_space=pl.ANY),
        ],
        out_specs=pl.BlockSpec((TM_SORT, D_MODEL), lambda i, *_: (i, 0)),
        scratch_shapes=[pltpu.VMEM((2, L_SORT, D_MODEL), F32), pltpu.SemaphoreType.DMA((2,))],
    )
    return pl.pallas_call(
        _combine_kernel,
        grid_spec=grid_spec,
        out_shape=jax.ShapeDtypeStruct((n_tok, D_MODEL), F32),
        compiler_params=pltpu.CompilerParams(
            dimension_semantics=("arbitrary",), vmem_limit_bytes=VMEM_LIMIT_BYTES),
        name="combine",
    )(slots, x1, post, g, b, eo)


def _excl_cumsum(a, axis):
    n = a.shape[axis]
    lower = (jnp.arange(n)[:, None] > jnp.arange(n)[None, :]).astype(F32)
    af = jnp.moveaxis(a.astype(F32), axis, 0).reshape(n, -1)
    out = jnp.dot(lower, af, precision=lax.Precision.HIGHEST)
    out = out.reshape((n,) + tuple(np.delete(np.array(a.shape), axis)))
    return jnp.moveaxis(out, 0, axis).astype(I32)


def _routing_metadata(cnt):
    nt = cnt.shape[0]
    c8 = (cnt + (SEG_ALIGN - 1)) // SEG_ALIGN * SEG_ALIGN
    seg = _excl_cumsum(c8, 1)
    tot8 = jnp.sum(c8, axis=1)
    n8 = jnp.sum(c8, axis=0)
    base = _excl_cumsum(n8, 0)
    goff = base[None, :] + _excl_cumsum(c8, 0)
    p_used = jnp.sum(n8)

    r = (jnp.arange(N_CHUNKS, dtype=I32) * SEG_ALIGN)[None, :, None]
    in_seg = (r >= seg[:, None, :]) & (r < (seg + c8)[:, None, :])
    real_slot = jnp.sum(jnp.where(in_seg, goff[:, None, :] + r - seg[:, None, :], 0), axis=2) // SEG_ALIGN
    used = tot8 // SEG_ALIGN
    n_empty = N_CHUNKS - used
    ci = jnp.arange(N_CHUNKS, dtype=I32)[None, :]
    empty_slot = p_used // SEG_ALIGN + _excl_cumsum(n_empty, 0)[:, None] + (ci - used[:, None])
    slots = jnp.where(ci < used[:, None], real_slot, empty_slot).reshape(-1).astype(I32)

    n_tiles = nt * L_SORT // TR
    n_items = n_tiles + N_EXPERTS
    e_lo, e_hi = base, base + n8
    t_first = e_lo // TR
    t_end = (e_hi + TR - 1) // TR
    w_e = jnp.where(n8 > 0, t_end - t_first, 0)
    w_start = _excl_cumsum(w_e, 0)
    w_cum = w_start + w_e
    w_total = jnp.sum(w_e)
    tiles_used = (p_used + TR - 1) // TR

    w = jnp.arange(n_items, dtype=I32)
    e_w = jnp.minimum(jnp.sum((w[:, None] >= w_cum[None, :]).astype(I32), axis=1), N_EXPERTS - 1)
    is_ffn = w < w_total
    ffn_tile = t_first[e_w] + (w - w_start[e_w])
    zero_tile = tiles_used + (w - w_total)
    is_zero = (~is_ffn) & (zero_tile < n_tiles)
    last_e = jnp.max(jnp.where(n8 > 0, jnp.arange(N_EXPERTS, dtype=I32), 0))
    tile = jnp.where(is_ffn, ffn_tile, jnp.where(is_zero, zero_tile, n_tiles - 1)).astype(I32)
    exp = jnp.where(is_ffn, e_w, last_e).astype(I32)
    kind = jnp.where(is_ffn, KIND_FFN, jnp.where(is_zero, KIND_ZERO, KIND_NOOP)).astype(I32)
    lo = jnp.clip(e_lo[exp] - tile * TR, 0, TR).astype(I32)
    hi = jnp.clip(e_hi[exp] - tile * TR, 0, TR).astype(I32)
    prev_tile = jnp.concatenate([jnp.full((1,), -1, I32), tile[:-1]])
    first = (tile != prev_tile).astype(I32)
    prev_exp = jnp.concatenate([jnp.full((1,), -1, I32), exp[:-1]])
    fresh = (exp != prev_exp).astype(I32)
    return dict(slots=slots, tile=tile, exp=exp, kind=kind, lo=lo, hi=hi, first=first, fresh=fresh)


def kernel(x, mem, w_in, b_gate, conv_w, gmlp_ws, gmlp_b, gmlp_ln_g, gmlp_ln_b, mem_ln_g,
           mem_ln_b, w_kv, w_conv_proj, w_gmlp_proj, w_xa_proj, w_out, ln1_g, ln1_b, w_router,
           b_router, w_gate_up, b_gate_up, w_down, b_down, ln2_g, ln2_b):
    bsz, seq, d = x.shape
    n_tok = bsz * seq
    assert d == D_MODEL and n_tok % TM_SORT == 0
    depth = w_in.shape[0]
    for l in range(depth):
        row = lambda a: a[l].reshape(1, -1)
        k, v = _kv_call(mem, row(mem_ln_g), row(mem_ln_b), w_kv[l].astype(BF16))
        gbias = jnp.repeat(gmlp_b[l].T, GROUP_CH, axis=1)
        wr = jnp.pad(w_router[l], ((0, 0), (0, LANES - N_EXPERTS)))
        wrh = wr.astype(BF16)
        wrl = (wr - wrh.astype(F32)).astype(BF16)
        br = jnp.pad(b_router[l], (0, LANES - N_EXPERTS)).reshape(1, LANES)
        x1, logits = _mix_call(
            x, w_in[l].astype(BF16), row(b_gate), conv_w[l], gmlp_ws[l], gbias,
            row(gmlp_ln_g), row(gmlp_ln_b), k, v, w_conv_proj[l].astype(BF16),
            w_gmlp_proj[l].astype(BF16), w_xa_proj[l].astype(BF16), w_out[l].astype(BF16),
            row(ln1_g), row(ln1_b), wrh, wrl, br)
        x1 = x1.reshape(n_tok, d)
        pos, gate, cnt = _route_call(logits.reshape(n_tok, LANES))
        md = _routing_metadata(cnt[:, :N_EXPERTS, 0])
        xs = _sort_call(md["slots"], x1, pos, gate)
        eo = _expert_call(md, xs, w_gate_up[l], b_gate_up[l][:, None, :],
                          w_down[l], b_down[l][:, None, :])
        post = jnp.transpose(pos, (0, 2, 1))
        x = _combine_call(md["slots"], x1, post, row(ln2_g), row(ln2_b), eo).reshape(bsz, seq, d)
    return x
```

```python
import jax
import jax.numpy as jnp
import numpy as np
from jax import lax
from jax.experimental import pallas as pl
from jax.experimental.pallas import tpu as pltpu

F32 = jnp.float32
BF16 = jnp.bfloat16
I32 = jnp.int32

D_MODEL = 1024
CHUNK = 128
GMLP_GROUPS = 8
GROUP_CH = D_MODEL // GMLP_GROUPS
MEM_LEN = 256
XA_HEADS = 4
XA_HEAD_DIM = D_MODEL // XA_HEADS
N_EXPERTS = 32
TOP_K = 4
D_FF = D_MODEL
W_IN_COLS = 9 * D_MODEL
SWIGLU_LIMIT = 7.0
SWIGLU_ALPHA = 1.702
LN_EPS = 1e-5
DEEPNORM_ALPHA = 2.0 ** 0.25

LANES = 128
SUBLANES = 8
VMEM_LIMIT_BYTES = 60000 * 1024

TM_MIX = 512
TM_SORT = 256
SEG_ALIGN = SUBLANES
L_SORT = 1280
assert L_SORT >= TOP_K * TM_SORT + N_EXPERTS * (SEG_ALIGN - 1) and L_SORT % 256 == 0
N_CHUNKS = L_SORT // SEG_ALIGN
D_PACK = D_MODEL // 2
XS_COLS = D_PACK + LANES
TR = 256
assert L_SORT % TR == 0


def _pack_pair(lo, hi):
    lo_bits = lax.shift_right_logical(lax.bitcast_convert_type(lo, I32), 16)
    return lo_bits | lax.bitcast_convert_type(hi, I32)


def _unpack_bf16(words):
    lo = lax.bitcast_convert_type(lax.shift_left(words, 16), F32)
    hi = lax.bitcast_convert_type(words & np.int32(-65536), F32)
    return jnp.concatenate([lo.astype(BF16), hi.astype(BF16)], axis=1)


def _layer_norm(x, g, b):
    mu = jnp.mean(x, axis=-1, keepdims=True)
    xc = x - mu
    var = jnp.mean(xc * xc, axis=-1, keepdims=True)
    return xc * lax.rsqrt(var + LN_EPS) * g + b


def _gelu_tanh(x):
    c = np.float32(np.sqrt(2.0 / np.pi))
    return x * (0.5 * (1.0 + jnp.tanh(c * (x + np.float32(0.044715) * (x * x * x)))))


def _sigmoid(x):
    return 1.0 / (1.0 + jnp.exp(-x))


def _dot(a, b):
    return jnp.dot(a, b, preferred_element_type=F32)


def _const_spec(shape):
    n = len(shape)
    return pl.BlockSpec(shape, lambda *_: (0,) * n, pipeline_mode=pl.Buffered(1))


def _kv_kernel(mem_ref, g_ref, b_ref, wkv_ref, k_ref, v_ref):
    mem_n = _layer_norm(mem_ref[0], g_ref[...], b_ref[...])
    kv = _dot(mem_n.astype(BF16), wkv_ref[...])
    k_ref[0] = kv[:, :D_MODEL].astype(BF16)
    v_ref[0] = kv[:, D_MODEL:].astype(BF16)


def _kv_call(mem, g, b, wkv):
    bsz = mem.shape[0]
    return pl.pallas_call(
        _kv_kernel,
        grid=(bsz,),
        in_specs=[
            pl.BlockSpec((1, MEM_LEN, D_MODEL), lambda i: (i, 0, 0)),
            pl.BlockSpec((1, D_MODEL), lambda i: (0, 0)),
            pl.BlockSpec((1, D_MODEL), lambda i: (0, 0)),
            pl.BlockSpec((D_MODEL, 2 * D_MODEL), lambda i: (0, 0)),
        ],
        out_specs=[
            pl.BlockSpec((1, MEM_LEN, D_MODEL), lambda i: (i, 0, 0)),
            pl.BlockSpec((1, MEM_LEN, D_MODEL), lambda i: (i, 0, 0)),
        ],
        out_shape=[
            jax.ShapeDtypeStruct((bsz, MEM_LEN, D_MODEL), BF16),
            jax.ShapeDtypeStruct((bsz, MEM_LEN, D_MODEL), BF16),
        ],
        compiler_params=pltpu.CompilerParams(dimension_semantics=("arbitrary",)),
        name="kv",
    )(mem, g, b, wkv)


def _mix_kernel(x_ref, win_ref, bgate_ref, convw_ref, ws_ref, gbias_ref, glng_ref, glnb_ref,
                k_ref, v_ref, wc_ref, wg_ref, wx_ref, wo_ref, ln1g_ref, ln1b_ref,
                wrh_ref, wrl_ref, br_ref, x1_ref, logit_ref, ubuf_ref):
    tm = x_ref.shape[1]
    d = D_MODEL
    x = x_ref[0]
    xb = x.astype(BF16)

    def proj(i):
        return _dot(xb, win_ref[:, i * d:(i + 1) * d])

    def gate(i):
        return _sigmoid(proj(6 + i) + bgate_ref[:, i * d:(i + 1) * d])

    @pl.when(pl.program_id(1) == 0)
    def _():
        ubuf_ref[0:SUBLANES, :] = jnp.zeros((SUBLANES, d), F32)

    u = proj(1) * proj(2)
    ubuf_ref[SUBLANES:SUBLANES + tm, :] = u
    u1 = ubuf_ref[SUBLANES - 1:SUBLANES - 1 + tm, :]
    u2 = ubuf_ref[SUBLANES - 2:SUBLANES - 2 + tm, :]
    conv = convw_ref[0:1, :] * u2 + convw_ref[1:2, :] * u1 + convw_ref[2:3, :] * u
    ubuf_ref[0:SUBLANES, :] = ubuf_ref[tm:tm + SUBLANES, :]
    y_conv = proj(0) * conv
    merged = gate(0) * _dot(y_conv.astype(BF16), wc_ref[...])

    gu = _gelu_tanh(proj(3))
    gv = _gelu_tanh(proj(4))
    vn = _layer_norm(gv, glng_ref[...], glnb_ref[...]).astype(BF16)
    n_chunks = tm // CHUNK
    row_i = lax.broadcasted_iota(I32, (CHUNK, CHUNK), 0)
    col_i = lax.broadcasted_iota(I32, (CHUNK, CHUNK), 1)
    causal = col_i <= row_i
    f_cols = []
    for g in range(GMLP_GROUPS):
        w_g = jnp.where(causal, ws_ref[g], 0.0).astype(BF16)
        rhs = jnp.concatenate(
            [vn[c * CHUNK:(c + 1) * CHUNK, g * GROUP_CH:(g + 1) * GROUP_CH] for c in range(n_chunks)],
            axis=1)
        fg = _dot(w_g, rhs)
        f_cols.append(jnp.concatenate(
            [fg[:, c * GROUP_CH:(c + 1) * GROUP_CH] for c in range(n_chunks)], axis=0))
    f = jnp.concatenate(f_cols, axis=1)
    gbias = jnp.concatenate([gbias_ref[...]] * n_chunks, axis=0)
    y_gmlp = gu * (f + gbias)
    merged = merged + gate(1) * _dot(y_gmlp.astype(BF16), wg_ref[...])

    q = proj(5).astype(BF16)
    heads = []
    for h in range(XA_HEADS):
        sl = slice(h * XA_HEAD_DIM, (h + 1) * XA_HEAD_DIM)
        s = lax.dot_general(q[:, sl], k_ref[0, :, sl], (((1,), (1,)), ((), ())),
                            preferred_element_type=F32) * np.float32(XA_HEAD_DIM ** -0.5)
        p = jnp.exp(s - jnp.max(s, axis=-1, keepdims=True))
        p = p * (1.0 / jnp.sum(p, axis=-1, keepdims=True))
        heads.append(_dot(p.astype(BF16), v_ref[0, :, sl]))
    y_xa = jnp.concatenate(heads, axis=1)
    merged = merged + gate(2) * _dot(y_xa.astype(BF16), wx_ref[...])

    mix = _dot(merged.astype(BF16), wo_ref[...])
    x1 = _layer_norm(np.float32(DEEPNORM_ALPHA) * x + mix, ln1g_ref[...], ln1b_ref[...])
    x1_ref[0] = x1

    x1h = x1.astype(BF16)
    x1l = (x1 - x1h.astype(F32)).astype(BF16)
    logit_ref[0] = (_dot(x1h, wrh_ref[...]) + _dot(x1l, wrh_ref[...]) + _dot(x1h, wrl_ref[...])
                    + br_ref[...])


def _mix_call(x, win, bgate, convw, ws, gbias, glng, glnb, k, v, wc, wg, wx, wo, ln1g, ln1b,
              wrh, wrl, br):
    bsz, seq, d = x.shape
    tm = min(TM_MIX, seq)
    assert seq % tm == 0 and tm % CHUNK == 0
    tile = lambda b, s: (b, s, 0)
    per_batch = lambda b, s: (b, 0, 0)
    return pl.pallas_call(
        _mix_kernel,
        grid=(bsz, seq // tm),
        in_specs=[
            pl.BlockSpec((1, tm, d), tile),
            _const_spec((d, W_IN_COLS)),
            _const_spec((1, 3 * d)),
            _const_spec((3, d)),
            _const_spec((GMLP_GROUPS, CHUNK, CHUNK)),
            _const_spec((CHUNK, d)),
            _const_spec((1, d)),
            _const_spec((1, d)),
            pl.BlockSpec((1, MEM_LEN, d), per_batch),
            pl.BlockSpec((1, MEM_LEN, d), per_batch),
            _const_spec((d, d)),
            _const_spec((d, d)),
            _const_spec((d, d)),
            _const_spec((d, d)),
            _const_spec((1, d)),
            _const_spec((1, d)),
            _const_spec((d, LANES)),
            _const_spec((d, LANES)),
            _const_spec((1, LANES)),
        ],
        out_specs=[pl.BlockSpec((1, tm, d), tile), pl.BlockSpec((1, tm, LANES), tile)],
        out_shape=[jax.ShapeDtypeStruct((bsz, seq, d), F32),
                   jax.ShapeDtypeStruct((bsz, seq, LANES), F32)],
        scratch_shapes=[pltpu.VMEM((tm + 2 * SUBLANES, d), F32)],
        compiler_params=pltpu.CompilerParams(
            dimension_semantics=("arbitrary", "arbitrary"), vmem_limit_bytes=VMEM_LIMIT_BYTES),
        name="mix",
    )(x, win, bgate, convw, ws, gbias, glng, glnb, k, v, wc, wg, wx, wo, ln1g, ln1b, wrh, wrl, br)


def _route_kernel(logit_ref, pos_ref, gate_ref, cnt_ref):
    tm = logit_ref.shape[0]
    lt = logit_ref[...].T
    e_io = lax.broadcasted_iota(I32, (LANES, tm), 0)
    neg = np.float32(-np.inf)
    work = jnp.where(e_io < N_EXPERTS, lt, neg)
    vals, sels = [], []
    for _ in range(TOP_K):
        m = jnp.max(work, axis=0, keepdims=True)
        idx = jnp.min(jnp.where(work == m, e_io, LANES), axis=0, keepdims=True)
        sel = e_io == idx
        vals.append(m)
        sels.append(sel)
        work = jnp.where(sel, neg, work)
    ex = [jnp.exp(vk - vals[0]) for vk in vals]
    inv = 1.0 / (ex[0] + ex[1] + ex[2] + ex[3])
    gates = [e * inv for e in ex]

    oh = jnp.where(sels[0] | sels[1] | sels[2] | sels[3], 1.0, 0.0).astype(F32)
    cnt = jnp.sum(oh, axis=1, keepdims=True)
    t_r = lax.broadcasted_iota(I32, (tm, tm), 0)
    t_c = lax.broadcasted_iota(I32, (tm, tm), 1)
    before = jnp.where(t_r < t_c, 1.0, 0.0).astype(BF16)
    rank = _dot(oh.astype(BF16), before)
    units = jnp.floor((cnt + np.float32(SEG_ALIGN - 1)) * np.float32(1.0 / SEG_ALIGN))
    units_b = jnp.broadcast_to(units, (LANES, LANES)).astype(BF16)
    e_r = lax.broadcasted_iota(I32, (LANES, LANES), 0)
    e_c = lax.broadcasted_iota(I32, (LANES, LANES), 1)
    below = jnp.where(e_c < e_r, 1.0, 0.0).astype(BF16)
    seg = _dot(below, units_b)[:, 0:1] * np.float32(SEG_ALIGN)
    slot = seg + rank

    pos_rows = [jnp.sum(jnp.where(s, slot, 0.0), axis=0, keepdims=True) for s in sels]
    zero_row = jnp.zeros((1, tm), F32)
    pos_ref[0] = jnp.concatenate(pos_rows + [zero_row] * (SUBLANES - TOP_K), axis=0).astype(I32)
    gate_ref[0] = jnp.concatenate(gates + [zero_row] * (SUBLANES - TOP_K), axis=0)
    cnt_ref[0] = jnp.broadcast_to(cnt, (LANES, LANES)).astype(I32)


def _route_call(logits2d):
    n_tok = logits2d.shape[0]
    nt = n_tok // TM_SORT
    return pl.pallas_call(
        _route_kernel,
        grid=(nt,),
        in_specs=[pl.BlockSpec((TM_SORT, LANES), lambda i: (i, 0))],
        out_specs=[
            pl.BlockSpec((1, SUBLANES, TM_SORT), lambda i: (i, 0, 0)),
            pl.BlockSpec((1, SUBLANES, TM_SORT), lambda i: (i, 0, 0)),
            pl.BlockSpec((1, LANES, LANES), lambda i: (i, 0, 0)),
        ],
        out_shape=[
            jax.ShapeDtypeStruct((nt, SUBLANES, TM_SORT), I32),
            jax.ShapeDtypeStruct((nt, SUBLANES, TM_SORT), F32),
            jax.ShapeDtypeStruct((nt, LANES, LANES), I32),
        ],
        compiler_params=pltpu.CompilerParams(dimension_semantics=("arbitrary",)),
        name="route",
    )(logits2d)


def _chunk_copy(src_ref, src_row, dst_ref, dst_row, sem):
    return pltpu.make_async_copy(
        src_ref.at[pl.ds(src_row, SEG_ALIGN)], dst_ref.at[pl.ds(dst_row, SEG_ALIGN)], sem)


def _sort_kernel(slot_ref, x1_ref, pos_ref, gate_ref, xs_hbm, sbuf, sems):
    j = pl.program_id(0)
    nt = pl.num_programs(0)
    tm = x1_ref.shape[0]
    cur = lax.rem(j, 2)
    buf = sbuf.at[cur]
    j_io = lax.broadcasted_iota(I32, (L_SORT, tm), 0)
    hit = [j_io == pos_ref[0, k:k + 1, :] for k in range(TOP_K)]
    onehot = jnp.where(hit[0] | hit[1] | hit[2] | hit[3], 1.0, 0.0).astype(BF16)
    rows = _dot(onehot, x1_ref[...].astype(BF16))
    buf[:, 0:D_PACK] = _pack_pair(rows[:, 0:D_PACK], rows[:, D_PACK:D_MODEL])
    gsel = jnp.where(hit[0], gate_ref[0, 0:1, :], 0.0)
    for k in range(1, TOP_K):
        gsel = gsel + jnp.where(hit[k], gate_ref[0, k:k + 1, :], 0.0)
    gate_col = jnp.sum(gsel, axis=1, keepdims=True)
    buf[:, D_PACK:XS_COLS] = lax.bitcast_convert_type(
        jnp.broadcast_to(gate_col, (L_SORT, LANES)), I32)

    for i in range(N_CHUNKS):
        dst = pl.multiple_of(slot_ref[j * N_CHUNKS + i] * SEG_ALIGN, SEG_ALIGN)
        _chunk_copy(buf, i * SEG_ALIGN, xs_hbm, dst, sems.at[cur]).start()

    def drain(which):
        for _ in range(N_CHUNKS):
            _chunk_copy(sbuf.at[which], 0, xs_hbm, 0, sems.at[which]).wait()

    @pl.when(j > 0)
    def _():
        drain(1 - cur)

    @pl.when(j == nt - 1)
    def _():
        drain(cur)


def _sort_call(slots, x1, pos, gate):
    n_tok = x1.shape[0]
    nt = n_tok // TM_SORT
    grid_spec = pltpu.PrefetchScalarGridSpec(
        num_scalar_prefetch=1,
        grid=(nt,),
        in_specs=[
            pl.BlockSpec((TM_SORT, D_MODEL), lambda i, *_: (i, 0)),
            pl.BlockSpec((1, SUBLANES, TM_SORT), lambda i, *_: (i, 0, 0)),
            pl.BlockSpec((1, SUBLANES, TM_SORT), lambda i, *_: (i, 0, 0)),
        ],
        out_specs=pl.BlockSpec(memory_space=pl.ANY),
        scratch_shapes=[
            pltpu.VMEM((2, L_SORT, XS_COLS), I32),
            pltpu.SemaphoreType.DMA((2,)),
        ],
    )
    return pl.pallas_call(
        _sort_kernel,
        grid_spec=grid_spec,
        out_shape=jax.ShapeDtypeStruct((nt * L_SORT, XS_COLS), I32),
        compiler_params=pltpu.CompilerParams(
            dimension_semantics=("arbitrary",), vmem_limit_bytes=VMEM_LIMIT_BYTES),
        name="sort",
    )(slots, x1, pos, gate)


KIND_NOOP, KIND_FFN, KIND_ZERO = 0, 1, 2


def _expert_kernel(tile_ref, exp_ref, kind_ref, lo_ref, hi_ref, first_ref, fresh_ref,
                   xs_ref, wgu_ref, bgu_ref, wd_ref, bd_ref, out_ref, wgu_b, wd_b):
    w = pl.program_id(0)
    kind = kind_ref[w]

    @pl.when(kind == KIND_ZERO)
    def _():
        out_ref[...] = jnp.zeros(out_ref.shape, I32)

    @pl.when((kind == KIND_FFN) & (fresh_ref[w] == 1))
    def _():
        wgu_b[...] = wgu_ref[0].astype(BF16)
        wd_b[...] = wd_ref[0].astype(BF16)

    @pl.when(kind == KIND_FFN)
    def _():
        xs = _unpack_bf16(xs_ref[:, 0:D_PACK])
        gate = lax.bitcast_convert_type(xs_ref[:, D_PACK:D_PACK + 1], F32)
        gu = _dot(xs, wgu_b[...]) + bgu_ref[0]
        g_lin = jnp.minimum(gu[:, :D_FF], np.float32(SWIGLU_LIMIT))
        u_lin = jnp.clip(gu[:, D_FF:], np.float32(-SWIGLU_LIMIT), np.float32(SWIGLU_LIMIT))
        act = (u_lin + 1.0) * (g_lin * _sigmoid(np.float32(SWIGLU_ALPHA) * g_lin))
        eo = (_dot(act.astype(BF16), wd_b[...]) + bd_ref[0]) * gate
        eo = eo.astype(BF16).astype(F32)
        eo = _pack_pair(eo[:, 0:D_PACK], eo[:, D_PACK:D_MODEL])
        rows = lax.broadcasted_iota(I32, (TR, 1), 0)
        mine = (rows >= lo_ref[w]) & (rows < hi_ref[w])

        @pl.when(first_ref[w] == 1)
        def _():
            out_ref[...] = jnp.where(mine, eo, 0)

        @pl.when(first_ref[w] == 0)
        def _():
            out_ref[...] = jnp.where(mine, eo, out_ref[...])


def _expert_call(md, xs, wgu, bgu, wd, bd):
    pmax = xs.shape[0]
    n_items = md["tile"].shape[0]
    grid_spec = pltpu.PrefetchScalarGridSpec(
        num_scalar_prefetch=7,
        grid=(n_items,),
        in_specs=[
            pl.BlockSpec((TR, XS_COLS), lambda w, t, e, *_: (t[w], 0)),
            pl.BlockSpec((1, D_MODEL, 2 * D_FF), lambda w, t, e, *_: (e[w], 0, 0)),
            pl.BlockSpec((1, 1, 2 * D_FF), lambda w, t, e, *_: (e[w], 0, 0)),
            pl.BlockSpec((1, D_FF, D_MODEL), lambda w, t, e, *_: (e[w], 0, 0)),
            pl.BlockSpec((1, 1, D_MODEL), lambda w, t, e, *_: (e[w], 0, 0)),
        ],
        out_specs=pl.BlockSpec((TR, D_PACK), lambda w, t, e, *_: (t[w], 0)),
        scratch_shapes=[pltpu.VMEM((D_MODEL, 2 * D_FF), BF16), pltpu.VMEM((D_FF, D_MODEL), BF16)],
    )
    return pl.pallas_call(
        _expert_kernel,
        grid_spec=grid_spec,
        out_shape=jax.ShapeDtypeStruct((pmax, D_PACK), I32),
        compiler_params=pltpu.CompilerParams(
            dimension_semantics=("arbitrary",), vmem_limit_bytes=VMEM_LIMIT_BYTES),
        name="experts",
    )(md["tile"], md["exp"], md["kind"], md["lo"], md["hi"], md["first"], md["fresh"],
      xs, wgu, bgu, wd, bd)


def _combine_kernel(slot_ref, x1_ref, post_ref, g_ref, b_ref, eo_hbm, out_ref, ebuf, sems):
    j = pl.program_id(0)
    nt = pl.num_programs(0)
    tm = x1_ref.shape[0]
    cur = lax.rem(j, 2)

    def fetch(tile, which):
        for i in range(N_CHUNKS):
            src = pl.multiple_of(slot_ref[tile * N_CHUNKS + i] * SEG_ALIGN, SEG_ALIGN)
            _chunk_copy(eo_hbm, src, ebuf.at[which], i * SEG_ALIGN, sems.at[which]).start()

    @pl.when(j == 0)
    def _():
        fetch(0, 0)

    @pl.when(j + 1 < nt)
    def _():
        fetch(j + 1, 1 - cur)

    for _ in range(N_CHUNKS):
        _chunk_copy(eo_hbm, 0, ebuf.at[cur], 0, sems.at[cur]).wait()

    l_io = lax.broadcasted_iota(I32, (tm, L_SORT), 1)
    hit = l_io == post_ref[0, :, 0:1]
    for k in range(1, TOP_K):
        hit = hit | (l_io == post_ref[0, :, k:k + 1])
    onehot = jnp.where(hit, 1.0, 0.0).astype(BF16)
    y = _dot(onehot, _unpack_bf16(ebuf[cur]))
    out_ref[...] = _layer_norm(np.float32(DEEPNORM_ALPHA) * x1_ref[...] + y, g_ref[...], b_ref[...])


def _combine_call(slots, x1, post, g, b, eo):
    n_tok = x1.shape[0]
    nt = n_tok // TM_SORT
    grid_spec = pltpu.PrefetchScalarGridSpec(
        num_scalar_prefetch=1,
        grid=(nt,),
        in_specs=[
            pl.BlockSpec((TM_SORT, D_MODEL), lambda i, *_: (i, 0)),
            pl.BlockSpec((1, TM_SORT, SUBLANES), lambda i, *_: (i, 0, 0)),
            pl.BlockSpec((1, D_MODEL), lambda i, *_: (0, 0)),
            pl.BlockSpec((1, D_MODEL), lambda i, *_: (0, 0)),
            pl.BlockSpec(memory_space=pl.ANY),
        ],
        out_specs=pl.BlockSpec((TM_SORT, D_MODEL), lambda i, *_: (i, 0)),
        scratch_shapes=[pltpu.VMEM((2, L_SORT, D_PACK), I32), pltpu.SemaphoreType.DMA((2,))],
    )
    return pl.pallas_call(
        _combine_kernel,
        grid_spec=grid_spec,
        out_shape=jax.ShapeDtypeStruct((n_tok, D_MODEL), F32),
        compiler_params=pltpu.CompilerParams(
            dimension_semantics=("arbitrary",), vmem_limit_bytes=VMEM_LIMIT_BYTES),
        name="combine",
    )(slots, x1, post, g, b, eo)


def _excl_cumsum(a, axis):
    n = a.shape[axis]
    lower = (jnp.arange(n)[:, None] > jnp.arange(n)[None, :]).astype(F32)
    af = jnp.moveaxis(a.astype(F32), axis, 0).reshape(n, -1)
    out = jnp.dot(lower, af, precision=lax.Precision.HIGHEST)
    out = out.reshape((n,) + tuple(np.delete(np.array(a.shape), axis)))
    return jnp.moveaxis(out, 0, axis).astype(I32)


def _routing_metadata(cnt):
    nt = cnt.shape[0]
    c8 = (cnt + (SEG_ALIGN - 1)) // SEG_ALIGN * SEG_ALIGN
    seg = _excl_cumsum(c8, 1)
    tot8 = jnp.sum(c8, axis=1)
    n8 = jnp.sum(c8, axis=0)
    base = _excl_cumsum(n8, 0)
    goff = base[None, :] + _excl_cumsum(c8, 0)
    p_used = jnp.sum(n8)

    r = (jnp.arange(N_CHUNKS, dtype=I32) * SEG_ALIGN)[None, :, None]
    in_seg = (r >= seg[:, None, :]) & (r < (seg + c8)[:, None, :])
    real_slot = jnp.sum(jnp.where(in_seg, goff[:, None, :] + r - seg[:, None, :], 0), axis=2) // SEG_ALIGN
    used = tot8 // SEG_ALIGN
    n_empty = N_CHUNKS - used
    ci = jnp.arange(N_CHUNKS, dtype=I32)[None, :]
    empty_slot = p_used // SEG_ALIGN + _excl_cumsum(n_empty, 0)[:, None] + (ci - used[:, None])
    slots = jnp.where(ci < used[:, None], real_slot, empty_slot).reshape(-1).astype(I32)

    n_tiles = nt * L_SORT // TR
    n_items = n_tiles + N_EXPERTS
    e_lo, e_hi = base, base + n8
    t_first = e_lo // TR
    t_end = (e_hi + TR - 1) // TR
    w_e = jnp.where(n8 > 0, t_end - t_first, 0)
    w_start = _excl_cumsum(w_e, 0)
    w_cum = w_start + w_e
    w_total = jnp.sum(w_e)
    tiles_used = (p_used + TR - 1) // TR

    w = jnp.arange(n_items, dtype=I32)
    e_w = jnp.minimum(jnp.sum((w[:, None] >= w_cum[None, :]).astype(I32), axis=1), N_EXPERTS - 1)
    is_ffn = w < w_total
    ffn_tile = t_first[e_w] + (w - w_start[e_w])
    zero_tile = tiles_used + (w - w_total)
    is_zero = (~is_ffn) & (zero_tile < n_tiles)
    last_e = jnp.max(jnp.where(n8 > 0, jnp.arange(N_EXPERTS, dtype=I32), 0))
    tile = jnp.where(is_ffn, ffn_tile, jnp.where(is_zero, zero_tile, n_tiles - 1)).astype(I32)
    exp = jnp.where(is_ffn, e_w, last_e).astype(I32)
    kind = jnp.where(is_ffn, KIND_FFN, jnp.where(is_zero, KIND_ZERO, KIND_NOOP)).astype(I32)
    lo = jnp.clip(e_lo[exp] - tile * TR, 0, TR).astype(I32)
    hi = jnp.clip(e_hi[exp] - tile * TR, 0, TR).astype(I32)
    prev_tile = jnp.concatenate([jnp.full((1,), -1, I32), tile[:-1]])
    first = (tile != prev_tile).astype(I32)
    prev_exp = jnp.concatenate([jnp.full((1,), -1, I32), exp[:-1]])
    fresh = (exp != prev_exp).astype(I32)
    return dict(slots=slots, tile=tile, exp=exp, kind=kind, lo=lo, hi=hi, first=first, fresh=fresh)


def kernel(x, mem, w_in, b_gate, conv_w, gmlp_ws, gmlp_b, gmlp_ln_g, gmlp_ln_b, mem_ln_g,
           mem_ln_b, w_kv, w_conv_proj, w_gmlp_proj, w_xa_proj, w_out, ln1_g, ln1_b, w_router,
           b_router, w_gate_up, b_gate_up, w_down, b_down, ln2_g, ln2_b):
    bsz, seq, d = x.shape
    n_tok = bsz * seq
    assert d == D_MODEL and n_tok % TM_SORT == 0
    depth = w_in.shape[0]
    for l in range(depth):
        row = lambda a: a[l].reshape(1, -1)
        k, v = _kv_call(mem, row(mem_ln_g), row(mem_ln_b), w_kv[l].astype(BF16))
        gbias = jnp.repeat(gmlp_b[l].T, GROUP_CH, axis=1)
        wr = jnp.pad(w_router[l], ((0, 0), (0, LANES - N_EXPERTS)))
        wrh = wr.astype(BF16)
        wrl = (wr - wrh.astype(F32)).astype(BF16)
        br = jnp.pad(b_router[l], (0, LANES - N_EXPERTS)).reshape(1, LANES)
        x1, logits = _mix_call(
            x, w_in[l].astype(BF16), row(b_gate), conv_w[l], gmlp_ws[l], gbias,
            row(gmlp_ln_g), row(gmlp_ln_b), k, v, w_conv_proj[l].astype(BF16),
            w_gmlp_proj[l].astype(BF16), w_xa_proj[l].astype(BF16), w_out[l].astype(BF16),
            row(ln1_g), row(ln1_b), wrh, wrl, br)
        x1 = x1.reshape(n_tok, d)
        pos, gate, cnt = _route_call(logits.reshape(n_tok, LANES))
        md = _routing_metadata(cnt[:, :N_EXPERTS, 0])
        xs = _sort_call(md["slots"], x1, pos, gate)
        eo = _expert_call(md, xs, w_gate_up[l], b_gate_up[l][:, None, :],
                          w_down[l], b_down[l][:, None, :])
        post = jnp.transpose(pos, (0, 2, 1))
        x = _combine_call(md["slots"], x1, post, row(ln2_g), row(ln2_b), eo).reshape(bsz, seq, d)
    return x
```

```python
import jax
import jax.numpy as jnp
import numpy as np
from jax import lax
from jax.experimental import pallas as pl
from jax.experimental.pallas import tpu as pltpu

F32 = jnp.float32
BF16 = jnp.bfloat16
I32 = jnp.int32

D_MODEL = 1024
CHUNK = 128
GMLP_GROUPS = 8
GROUP_CH = D_MODEL // GMLP_GROUPS
MEM_LEN = 256
XA_HEADS = 4
XA_HEAD_DIM = D_MODEL // XA_HEADS
N_EXPERTS = 32
TOP_K = 4
D_FF = D_MODEL
W_IN_COLS = 9 * D_MODEL
SWIGLU_LIMIT = 7.0
SWIGLU_ALPHA = 1.702
LN_EPS = 1e-5
DEEPNORM_ALPHA = 2.0 ** 0.25

LANES = 128
SUBLANES = 8
VMEM_LIMIT_BYTES = 60000 * 1024

TM_MIX = 512
TM_SORT = 256
SEG_ALIGN = SUBLANES
L_SORT = 1280
assert L_SORT >= TOP_K * TM_SORT + N_EXPERTS * (SEG_ALIGN - 1) and L_SORT % 256 == 0
N_CHUNKS = L_SORT // SEG_ALIGN
D_PACK = D_MODEL // 2
XS_COLS = D_PACK + LANES
TR = 512
TR_SUB = 256
assert TR % TR_SUB == 0


def _pack_pair(lo, hi):
    lo_bits = lax.shift_right_logical(lax.bitcast_convert_type(lo, I32), 16)
    return lo_bits | lax.bitcast_convert_type(hi, I32)


def _unpack_bf16(words):
    lo = lax.bitcast_convert_type(lax.shift_left(words, 16), F32)
    hi = lax.bitcast_convert_type(words & np.int32(-65536), F32)
    return jnp.concatenate([lo.astype(BF16), hi.astype(BF16)], axis=1)


def _layer_norm(x, g, b):
    mu = jnp.mean(x, axis=-1, keepdims=True)
    xc = x - mu
    var = jnp.mean(xc * xc, axis=-1, keepdims=True)
    return xc * lax.rsqrt(var + LN_EPS) * g + b


def _gelu_tanh(x):
    c = np.float32(np.sqrt(2.0 / np.pi))
    return x * (0.5 * (1.0 + jnp.tanh(c * (x + np.float32(0.044715) * (x * x * x)))))


def _sigmoid(x):
    return 1.0 / (1.0 + jnp.exp(-x))


def _dot(a, b):
    return jnp.dot(a, b, preferred_element_type=F32)


def _const_spec(shape):
    n = len(shape)
    return pl.BlockSpec(shape, lambda *_: (0,) * n, pipeline_mode=pl.Buffered(1))


def _kv_kernel(mem_ref, g_ref, b_ref, wkv_ref, k_ref, v_ref):
    mem_n = _layer_norm(mem_ref[0], g_ref[...], b_ref[...])
    kv = _dot(mem_n.astype(BF16), wkv_ref[...])
    k_ref[0] = kv[:, :D_MODEL].astype(BF16)
    v_ref[0] = kv[:, D_MODEL:].astype(BF16)


def _kv_call(mem, g, b, wkv):
    bsz = mem.shape[0]
    return pl.pallas_call(
        _kv_kernel,
        grid=(bsz,),
        in_specs=[
            pl.BlockSpec((1, MEM_LEN, D_MODEL), lambda i: (i, 0, 0)),
            pl.BlockSpec((1, D_MODEL), lambda i: (0, 0)),
            pl.BlockSpec((1, D_MODEL), lambda i: (0, 0)),
            pl.BlockSpec((D_MODEL, 2 * D_MODEL), lambda i: (0, 0)),
        ],
        out_specs=[
            pl.BlockSpec((1, MEM_LEN, D_MODEL), lambda i: (i, 0, 0)),
            pl.BlockSpec((1, MEM_LEN, D_MODEL), lambda i: (i, 0, 0)),
        ],
        out_shape=[
            jax.ShapeDtypeStruct((bsz, MEM_LEN, D_MODEL), BF16),
            jax.ShapeDtypeStruct((bsz, MEM_LEN, D_MODEL), BF16),
        ],
        compiler_params=pltpu.CompilerParams(dimension_semantics=("arbitrary",)),
        name="kv",
    )(mem, g, b, wkv)


def _mix_kernel(x_ref, win_ref, bgate_ref, convw_ref, ws_ref, gbias_ref, glng_ref, glnb_ref,
                k_ref, v_ref, wc_ref, wg_ref, wx_ref, wo_ref, ln1g_ref, ln1b_ref,
                wrh_ref, wrl_ref, br_ref, x1_ref, logit_ref, ubuf_ref):
    tm = x_ref.shape[1]
    d = D_MODEL
    x = x_ref[0]
    xb = x.astype(BF16)

    def proj(i):
        return _dot(xb, win_ref[:, i * d:(i + 1) * d])

    def gate(i):
        return _sigmoid(proj(6 + i) + bgate_ref[:, i * d:(i + 1) * d])

    @pl.when(pl.program_id(1) == 0)
    def _():
        ubuf_ref[0:SUBLANES, :] = jnp.zeros((SUBLANES, d), F32)

    u = proj(1) * proj(2)
    ubuf_ref[SUBLANES:SUBLANES + tm, :] = u
    u1 = ubuf_ref[SUBLANES - 1:SUBLANES - 1 + tm, :]
    u2 = ubuf_ref[SUBLANES - 2:SUBLANES - 2 + tm, :]
    conv = convw_ref[0:1, :] * u2 + convw_ref[1:2, :] * u1 + convw_ref[2:3, :] * u
    ubuf_ref[0:SUBLANES, :] = ubuf_ref[tm:tm + SUBLANES, :]
    y_conv = proj(0) * conv
    merged = gate(0) * _dot(y_conv.astype(BF16), wc_ref[...])

    gu = _gelu_tanh(proj(3))
    gv = _gelu_tanh(proj(4))
    vn = _layer_norm(gv, glng_ref[...], glnb_ref[...]).astype(BF16)
    n_chunks = tm // CHUNK
    row_i = lax.broadcasted_iota(I32, (CHUNK, CHUNK), 0)
    col_i = lax.broadcasted_iota(I32, (CHUNK, CHUNK), 1)
    causal = col_i <= row_i
    f_cols = []
    for g in range(GMLP_GROUPS):
        w_g = jnp.where(causal, ws_ref[g], 0.0).astype(BF16)
        rhs = jnp.concatenate(
            [vn[c * CHUNK:(c + 1) * CHUNK, g * GROUP_CH:(g + 1) * GROUP_CH] for c in range(n_chunks)],
            axis=1)
        fg = _dot(w_g, rhs)
        f_cols.append(jnp.concatenate(
            [fg[:, c * GROUP_CH:(c + 1) * GROUP_CH] for c in range(n_chunks)], axis=0))
    f = jnp.concatenate(f_cols, axis=1)
    gbias = jnp.concatenate([gbias_ref[...]] * n_chunks, axis=0)
    y_gmlp = gu * (f + gbias)
    merged = merged + gate(1) * _dot(y_gmlp.astype(BF16), wg_ref[...])

    q = proj(5).astype(BF16)
    heads = []
    for h in range(XA_HEADS):
        sl = slice(h * XA_HEAD_DIM, (h + 1) * XA_HEAD_DIM)
        s = lax.dot_general(q[:, sl], k_ref[0, :, sl], (((1,), (1,)), ((), ())),
                            preferred_element_type=F32) * np.float32(XA_HEAD_DIM ** -0.5)
        p = jnp.exp(s - jnp.max(s, axis=-1, keepdims=True))
        p = p * (1.0 / jnp.sum(p, axis=-1, keepdims=True))
        heads.append(_dot(p.astype(BF16), v_ref[0, :, sl]))
    y_xa = jnp.concatenate(heads, axis=1)
    merged = merged + gate(2) * _dot(y_xa.astype(BF16), wx_ref[...])

    mix = _dot(merged.astype(BF16), wo_ref[...])
    x1 = _layer_norm(np.float32(DEEPNORM_ALPHA) * x + mix, ln1g_ref[...], ln1b_ref[...])
    x1_ref[0] = x1

    x1h = x1.astype(BF16)
    x1l = (x1 - x1h.astype(F32)).astype(BF16)
    logit_ref[0] = (_dot(x1h, wrh_ref[...]) + _dot(x1l, wrh_ref[...]) + _dot(x1h, wrl_ref[...])
                    + br_ref[...])


def _mix_call(x, win, bgate, convw, ws, gbias, glng, glnb, k, v, wc, wg, wx, wo, ln1g, ln1b,
              wrh, wrl, br):
    bsz, seq, d = x.shape
    tm = min(TM_MIX, seq)
    assert seq % tm == 0 and tm % CHUNK == 0
    tile = lambda b, s: (b, s, 0)
    per_batch = lambda b, s: (b, 0, 0)
    return pl.pallas_call(
        _mix_kernel,
        grid=(bsz, seq // tm),
        in_specs=[
            pl.BlockSpec((1, tm, d), tile),
            _const_spec((d, W_IN_COLS)),
            _const_spec((1, 3 * d)),
            _const_spec((3, d)),
            _const_spec((GMLP_GROUPS, CHUNK, CHUNK)),
            _const_spec((CHUNK, d)),
            _const_spec((1, d)),
            _const_spec((1, d)),
            pl.BlockSpec((1, MEM_LEN, d), per_batch),
            pl.BlockSpec((1, MEM_LEN, d), per_batch),
            _const_spec((d, d)),
            _const_spec((d, d)),
            _const_spec((d, d)),
            _const_spec((d, d)),
            _const_spec((1, d)),
            _const_spec((1, d)),
            _const_spec((d, LANES)),
            _const_spec((d, LANES)),
            _const_spec((1, LANES)),
        ],
        out_specs=[pl.BlockSpec((1, tm, d), tile), pl.BlockSpec((1, tm, LANES), tile)],
        out_shape=[jax.ShapeDtypeStruct((bsz, seq, d), F32),
                   jax.ShapeDtypeStruct((bsz, seq, LANES), F32)],
        scratch_shapes=[pltpu.VMEM((tm + 2 * SUBLANES, d), F32)],
        compiler_params=pltpu.CompilerParams(
            dimension_semantics=("arbitrary", "arbitrary"), vmem_limit_bytes=VMEM_LIMIT_BYTES),
        name="mix",
    )(x, win, bgate, convw, ws, gbias, glng, glnb, k, v, wc, wg, wx, wo, ln1g, ln1b, wrh, wrl, br)


def _route_kernel(logit_ref, pos_ref, gate_ref, cnt_ref):
    tm = logit_ref.shape[0]
    lt = logit_ref[...].T
    e_io = lax.broadcasted_iota(I32, (LANES, tm), 0)
    neg = np.float32(-np.inf)
    work = jnp.where(e_io < N_EXPERTS, lt, neg)
    vals, sels = [], []
    for _ in range(TOP_K):
        m = jnp.max(work, axis=0, keepdims=True)
        idx = jnp.min(jnp.where(work == m, e_io, LANES), axis=0, keepdims=True)
        sel = e_io == idx
        vals.append(m)
        sels.append(sel)
        work = jnp.where(sel, neg, work)
    ex = [jnp.exp(vk - vals[0]) for vk in vals]
    inv = 1.0 / (ex[0] + ex[1] + ex[2] + ex[3])
    gates = [e * inv for e in ex]

    oh = jnp.where(sels[0] | sels[1] | sels[2] | sels[3], 1.0, 0.0).astype(F32)
    cnt = jnp.sum(oh, axis=1, keepdims=True)
    t_r = lax.broadcasted_iota(I32, (tm, tm), 0)
    t_c = lax.broadcasted_iota(I32, (tm, tm), 1)
    before = jnp.where(t_r < t_c, 1.0, 0.0).astype(BF16)
    rank = _dot(oh.astype(BF16), before)
    units = jnp.floor((cnt + np.float32(SEG_ALIGN - 1)) * np.float32(1.0 / SEG_ALIGN))
    units_b = jnp.broadcast_to(units, (LANES, LANES)).astype(BF16)
    e_r = lax.broadcasted_iota(I32, (LANES, LANES), 0)
    e_c = lax.broadcasted_iota(I32, (LANES, LANES), 1)
    below = jnp.where(e_c < e_r, 1.0, 0.0).astype(BF16)
    seg = _dot(below, units_b)[:, 0:1] * np.float32(SEG_ALIGN)
    slot = seg + rank

    pos_rows = [jnp.sum(jnp.where(s, slot, 0.0), axis=0, keepdims=True) for s in sels]
    zero_row = jnp.zeros((1, tm), F32)
    pos_ref[0] = jnp.concatenate(pos_rows + [zero_row] * (SUBLANES - TOP_K), axis=0).astype(I32)
    gate_ref[0] = jnp.concatenate(gates + [zero_row] * (SUBLANES - TOP_K), axis=0)
    cnt_ref[0] = jnp.broadcast_to(cnt, (LANES, LANES)).astype(I32)


def _route_call(logits2d):
    n_tok = logits2d.shape[0]
    nt = n_tok // TM_SORT
    return pl.pallas_call(
        _route_kernel,
        grid=(nt,),
        in_specs=[pl.BlockSpec((TM_SORT, LANES), lambda i: (i, 0))],
        out_specs=[
            pl.BlockSpec((1, SUBLANES, TM_SORT), lambda i: (i, 0, 0)),
            pl.BlockSpec((1, SUBLANES, TM_SORT), lambda i: (i, 0, 0)),
            pl.BlockSpec((1, LANES, LANES), lambda i: (i, 0, 0)),
        ],
        out_shape=[
            jax.ShapeDtypeStruct((nt, SUBLANES, TM_SORT), I32),
            jax.ShapeDtypeStruct((nt, SUBLANES, TM_SORT), F32),
            jax.ShapeDtypeStruct((nt, LANES, LANES), I32),
        ],
        compiler_params=pltpu.CompilerParams(dimension_semantics=("arbitrary",)),
        name="route",
    )(logits2d)


def _chunk_copy(src_ref, src_row, dst_ref, dst_row, sem):
    return pltpu.make_async_copy(
        src_ref.at[pl.ds(src_row, SEG_ALIGN)], dst_ref.at[pl.ds(dst_row, SEG_ALIGN)], sem)


def _sort_kernel(slot_ref, x1_ref, pos_ref, gate_ref, xs_hbm, sbuf, sems):
    j = pl.program_id(0)
    nt = pl.num_programs(0)
    tm = x1_ref.shape[0]
    cur = lax.rem(j, 2)
    buf = sbuf.at[cur]
    j_io = lax.broadcasted_iota(I32, (L_SORT, tm), 0)
    hit = [j_io == pos_ref[0, k:k + 1, :] for k in range(TOP_K)]
    onehot = jnp.where(hit[0] | hit[1] | hit[2] | hit[3], 1.0, 0.0).astype(BF16)
    rows = _dot(onehot, x1_ref[...].astype(BF16))
    buf[:, 0:D_PACK] = _pack_pair(rows[:, 0:D_PACK], rows[:, D_PACK:D_MODEL])
    gsel = jnp.where(hit[0], gate_ref[0, 0:1, :], 0.0)
    for k in range(1, TOP_K):
        gsel = gsel + jnp.where(hit[k], gate_ref[0, k:k + 1, :], 0.0)
    gate_col = jnp.sum(gsel, axis=1, keepdims=True)
    buf[:, D_PACK:XS_COLS] = lax.bitcast_convert_type(
        jnp.broadcast_to(gate_col, (L_SORT, LANES)), I32)

    for i in range(N_CHUNKS):
        dst = pl.multiple_of(slot_ref[j * N_CHUNKS + i] * SEG_ALIGN, SEG_ALIGN)
        _chunk_copy(buf, i * SEG_ALIGN, xs_hbm, dst, sems.at[cur]).start()

    def drain(which):
        for _ in range(N_CHUNKS):
            _chunk_copy(sbuf.at[which], 0, xs_hbm, 0, sems.at[which]).wait()

    @pl.when(j > 0)
    def _():
        drain(1 - cur)

    @pl.when(j == nt - 1)
    def _():
        drain(cur)


def _sort_call(slots, x1, pos, gate):
    n_tok = x1.shape[0]
    nt = n_tok // TM_SORT
    grid_spec = pltpu.PrefetchScalarGridSpec(
        num_scalar_prefetch=1,
        grid=(nt,),
        in_specs=[
            pl.BlockSpec((TM_SORT, D_MODEL), lambda i, *_: (i, 0)),
            pl.BlockSpec((1, SUBLANES, TM_SORT), lambda i, *_: (i, 0, 0)),
            pl.BlockSpec((1, SUBLANES, TM_SORT), lambda i, *_: (i, 0, 0)),
        ],
        out_specs=pl.BlockSpec(memory_space=pl.ANY),
        scratch_shapes=[
            pltpu.VMEM((2, L_SORT, XS_COLS), I32),
            pltpu.SemaphoreType.DMA((2,)),
        ],
    )
    return pl.pallas_call(
        _sort_kernel,
        grid_spec=grid_spec,
        out_shape=jax.ShapeDtypeStruct((nt * L_SORT, XS_COLS), I32),
        compiler_params=pltpu.CompilerParams(
            dimension_semantics=("arbitrary",), vmem_limit_bytes=VMEM_LIMIT_BYTES),
        name="sort",
    )(slots, x1, pos, gate)


KIND_NOOP, KIND_FFN, KIND_ZERO = 0, 1, 2


def _expert_kernel(tile_ref, exp_ref, kind_ref, lo_ref, hi_ref, first_ref, fresh_ref,
                   xs_ref, wgu_ref, bgu_ref, wd_ref, bd_ref, out_ref, wgu_b, wd_b):
    w = pl.program_id(0)
    kind = kind_ref[w]

    @pl.when(kind == KIND_ZERO)
    def _():
        out_ref[...] = jnp.zeros(out_ref.shape, I32)

    @pl.when((kind == KIND_FFN) & (fresh_ref[w] == 1))
    def _():
        wgu_b[...] = wgu_ref[0].astype(BF16)
        wd_b[...] = wd_ref[0].astype(BF16)

    lo = lo_ref[w]
    hi = hi_ref[w]
    first = first_ref[w]
    is_ffn = kind == KIND_FFN
    whole = (lo == 0) & (hi == TR)

    def ffn(r0, nr):
        xs = _unpack_bf16(xs_ref[r0:r0 + nr, 0:D_PACK])
        gate = lax.bitcast_convert_type(xs_ref[r0:r0 + nr, D_PACK:D_PACK + 1], F32)
        gu = _dot(xs, wgu_b[...]) + bgu_ref[0]
        g_lin = jnp.minimum(gu[:, :D_FF], np.float32(SWIGLU_LIMIT))
        u_lin = jnp.clip(gu[:, D_FF:], np.float32(-SWIGLU_LIMIT), np.float32(SWIGLU_LIMIT))
        act = (u_lin + 1.0) * (g_lin * _sigmoid(np.float32(SWIGLU_ALPHA) * g_lin))
        eo = (_dot(act.astype(BF16), wd_b[...]) + bd_ref[0]) * gate
        eo = eo.astype(BF16).astype(F32)
        return _pack_pair(eo[:, 0:D_PACK], eo[:, D_PACK:D_MODEL])

    @pl.when(is_ffn & whole)
    def _():
        out_ref[...] = ffn(0, TR)

    for r0 in range(0, TR, TR_SUB):
        touched = (lo < r0 + TR_SUB) & (hi > r0)
        part = is_ffn & jnp.logical_not(whole)

        @pl.when(part & touched)
        def _(r0=r0):
            eo = ffn(r0, TR_SUB)
            rows = r0 + lax.broadcasted_iota(I32, (TR_SUB, 1), 0)
            mine = (rows >= lo) & (rows < hi)

            @pl.when(first == 1)
            def _():
                out_ref[r0:r0 + TR_SUB, :] = jnp.where(mine, eo, 0)

            @pl.when(first == 0)
            def _():
                out_ref[r0:r0 + TR_SUB, :] = jnp.where(mine, eo, out_ref[r0:r0 + TR_SUB, :])

        @pl.when(part & jnp.logical_not(touched) & (first == 1))
        def _(r0=r0):
            out_ref[r0:r0 + TR_SUB, :] = jnp.zeros((TR_SUB, D_PACK), I32)


def _expert_call(md, xs, wgu, bgu, wd, bd):
    pmax = xs.shape[0]
    n_items = md["tile"].shape[0]
    grid_spec = pltpu.PrefetchScalarGridSpec(
        num_scalar_prefetch=7,
        grid=(n_items,),
        in_specs=[
            pl.BlockSpec((TR, XS_COLS), lambda w, t, e, *_: (t[w], 0)),
            pl.BlockSpec((1, D_MODEL, 2 * D_FF), lambda w, t, e, *_: (e[w], 0, 0)),
            pl.BlockSpec((1, 1, 2 * D_FF), lambda w, t, e, *_: (e[w], 0, 0)),
            pl.BlockSpec((1, D_FF, D_MODEL), lambda w, t, e, *_: (e[w], 0, 0)),
            pl.BlockSpec((1, 1, D_MODEL), lambda w, t, e, *_: (e[w], 0, 0)),
        ],
        out_specs=pl.BlockSpec((TR, D_PACK), lambda w, t, e, *_: (t[w], 0)),
        scratch_shapes=[pltpu.VMEM((D_MODEL, 2 * D_FF), BF16), pltpu.VMEM((D_FF, D_MODEL), BF16)],
    )
    return pl.pallas_call(
        _expert_kernel,
        grid_spec=grid_spec,
        out_shape=jax.ShapeDtypeStruct((pmax, D_PACK), I32),
        compiler_params=pltpu.CompilerParams(
            dimension_semantics=("arbitrary",), vmem_limit_bytes=VMEM_LIMIT_BYTES),
        name="experts",
    )(md["tile"], md["exp"], md["kind"], md["lo"], md["hi"], md["first"], md["fresh"],
      xs, wgu, bgu, wd, bd)


def _combine_kernel(slot_ref, x1_ref, post_ref, g_ref, b_ref, eo_hbm, out_ref, ebuf, sems):
    j = pl.program_id(0)
    nt = pl.num_programs(0)
    tm = x1_ref.shape[0]
    cur = lax.rem(j, 2)

    def fetch(tile, which):
        for i in range(N_CHUNKS):
            src = pl.multiple_of(slot_ref[tile * N_CHUNKS + i] * SEG_ALIGN, SEG_ALIGN)
            _chunk_copy(eo_hbm, src, ebuf.at[which], i * SEG_ALIGN, sems.at[which]).start()

    @pl.when(j == 0)
    def _():
        fetch(0, 0)

    @pl.when(j + 1 < nt)
    def _():
        fetch(j + 1, 1 - cur)

    for _ in range(N_CHUNKS):
        _chunk_copy(eo_hbm, 0, ebuf.at[cur], 0, sems.at[cur]).wait()

    l_io = lax.broadcasted_iota(I32, (tm, L_SORT), 1)
    hit = l_io == post_ref[0, :, 0:1]
    for k in range(1, TOP_K):
        hit = hit | (l_io == post_ref[0, :, k:k + 1])
    onehot = jnp.where(hit, 1.0, 0.0).astype(BF16)
    y = _dot(onehot, _unpack_bf16(ebuf[cur]))
    out_ref[...] = _layer_norm(np.float32(DEEPNORM_ALPHA) * x1_ref[...] + y, g_ref[...], b_ref[...])


def _combine_call(slots, x1, post, g, b, eo):
    n_tok = x1.shape[0]
    nt = n_tok // TM_SORT
    grid_spec = pltpu.PrefetchScalarGridSpec(
        num_scalar_prefetch=1,
        grid=(nt,),
        in_specs=[
            pl.BlockSpec((TM_SORT, D_MODEL), lambda i, *_: (i, 0)),
            pl.BlockSpec((1, TM_SORT, SUBLANES), lambda i, *_: (i, 0, 0)),
            pl.BlockSpec((1, D_MODEL), lambda i, *_: (0, 0)),
            pl.BlockSpec((1, D_MODEL), lambda i, *_: (0, 0)),
            pl.BlockSpec(memory_space=pl.ANY),
        ],
        out_specs=pl.BlockSpec((TM_SORT, D_MODEL), lambda i, *_: (i, 0)),
        scratch_shapes=[pltpu.VMEM((2, L_SORT, D_PACK), I32), pltpu.SemaphoreType.DMA((2,))],
    )
    return pl.pallas_call(
        _combine_kernel,
        grid_spec=grid_spec,
        out_shape=jax.ShapeDtypeStruct((n_tok, D_MODEL), F32),
        compiler_params=pltpu.CompilerParams(
            dimension_semantics=("arbitrary",), vmem_limit_bytes=VMEM_LIMIT_BYTES),
        name="combine",
    )(slots, x1, post, g, b, eo)


def _excl_cumsum(a, axis):
    n = a.shape[axis]
    lower = (jnp.arange(n)[:, None] > jnp.arange(n)[None, :]).astype(F32)
    af = jnp.moveaxis(a.astype(F32), axis, 0).reshape(n, -1)
    out = jnp.dot(lower, af, precision=lax.Precision.HIGHEST)
    out = out.reshape((n,) + tuple(np.delete(np.array(a.shape), axis)))
    return jnp.moveaxis(out, 0, axis).astype(I32)


def _routing_metadata(cnt):
    nt = cnt.shape[0]
    c8 = (cnt + (SEG_ALIGN - 1)) // SEG_ALIGN * SEG_ALIGN
    seg = _excl_cumsum(c8, 1)
    tot8 = jnp.sum(c8, axis=1)
    n8 = jnp.sum(c8, axis=0)
    base = _excl_cumsum(n8, 0)
    goff = base[None, :] + _excl_cumsum(c8, 0)
    p_used = jnp.sum(n8)

    r = (jnp.arange(N_CHUNKS, dtype=I32) * SEG_ALIGN)[None, :, None]
    in_seg = (r >= seg[:, None, :]) & (r < (seg + c8)[:, None, :])
    real_slot = jnp.sum(jnp.where(in_seg, goff[:, None, :] + r - seg[:, None, :], 0), axis=2) // SEG_ALIGN
    used = tot8 // SEG_ALIGN
    n_empty = N_CHUNKS - used
    ci = jnp.arange(N_CHUNKS, dtype=I32)[None, :]
    empty_slot = p_used // SEG_ALIGN + _excl_cumsum(n_empty, 0)[:, None] + (ci - used[:, None])
    slots = jnp.where(ci < used[:, None], real_slot, empty_slot).reshape(-1).astype(I32)

    assert (nt * L_SORT) % TR == 0
    n_tiles = nt * L_SORT // TR
    n_items = n_tiles + N_EXPERTS
    e_lo, e_hi = base, base + n8
    t_first = e_lo // TR
    t_end = (e_hi + TR - 1) // TR
    w_e = jnp.where(n8 > 0, t_end - t_first, 0)
    w_start = _excl_cumsum(w_e, 0)
    w_cum = w_start + w_e
    w_total = jnp.sum(w_e)
    tiles_used = (p_used + TR - 1) // TR

    w = jnp.arange(n_items, dtype=I32)
    e_w = jnp.minimum(jnp.sum((w[:, None] >= w_cum[None, :]).astype(I32), axis=1), N_EXPERTS - 1)
    is_ffn = w < w_total
    ffn_tile = t_first[e_w] + (w - w_start[e_w])
    zero_tile = tiles_used + (w - w_total)
    is_zero = (~is_ffn) & (zero_tile < n_tiles)
    last_e = jnp.max(jnp.where(n8 > 0, jnp.arange(N_EXPERTS, dtype=I32), 0))
    tile = jnp.where(is_ffn, ffn_tile, jnp.where(is_zero, zero_tile, n_tiles - 1)).astype(I32)
    exp = jnp.where(is_ffn, e_w, last_e).astype(I32)
    kind = jnp.where(is_ffn, KIND_FFN, jnp.where(is_zero, KIND_ZERO, KIND_NOOP)).astype(I32)
    lo = jnp.clip(e_lo[exp] - tile * TR, 0, TR).astype(I32)
    hi = jnp.clip(e_hi[exp] - tile * TR, 0, TR).astype(I32)
    prev_tile = jnp.concatenate([jnp.full((1,), -1, I32), tile[:-1]])
    first = (tile != prev_tile).astype(I32)
    prev_exp = jnp.concatenate([jnp.full((1,), -1, I32), exp[:-1]])
    fresh = (exp != prev_exp).astype(I32)
    return dict(slots=slots, tile=tile, exp=exp, kind=kind, lo=lo, hi=hi, first=first, fresh=fresh)


def kernel(x, mem, w_in, b_gate, conv_w, gmlp_ws, gmlp_b, gmlp_ln_g, gmlp_ln_b, mem_ln_g,
           mem_ln_b, w_kv, w_conv_proj, w_gmlp_proj, w_xa_proj, w_out, ln1_g, ln1_b, w_router,
           b_router, w_gate_up, b_gate_up, w_down, b_down, ln2_g, ln2_b):
    bsz, seq, d = x.shape
    n_tok = bsz * seq
    assert d == D_MODEL and n_tok % TM_SORT == 0
    depth = w_in.shape[0]
    for l in range(depth):
        row = lambda a: a[l].reshape(1, -1)
        k, v = _kv_call(mem, row(mem_ln_g), row(mem_ln_b), w_kv[l].astype(BF16))
        gbias = jnp.repeat(gmlp_b[l].T, GROUP_CH, axis=1)
        wr = jnp.pad(w_router[l], ((0, 0), (0, LANES - N_EXPERTS)))
        wrh = wr.astype(BF16)
        wrl = (wr - wrh.astype(F32)).astype(BF16)
        br = jnp.pad(b_router[l], (0, LANES - N_EXPERTS)).reshape(1, LANES)
        x1, logits = _mix_call(
            x, w_in[l].astype(BF16), row(b_gate), conv_w[l], gmlp_ws[l], gbias,
            row(gmlp_ln_g), row(gmlp_ln_b), k, v, w_conv_proj[l].astype(BF16),
            w_gmlp_proj[l].astype(BF16), w_xa_proj[l].astype(BF16), w_out[l].astype(BF16),
            row(ln1_g), row(ln1_b), wrh, wrl, br)
        x1 = x1.reshape(n_tok, d)
        pos, gate, cnt = _route_call(logits.reshape(n_tok, LANES))
        md = _routing_metadata(cnt[:, :N_EXPERTS, 0])
        xs = _sort_call(md["slots"], x1, pos, gate)
        eo = _expert_call(md, xs, w_gate_up[l], b_gate_up[l][:, None, :],
                          w_down[l], b_down[l][:, None, :])
        post = jnp.transpose(pos, (0, 2, 1))
        x = _combine_call(md["slots"], x1, post, row(ln2_g), row(ln2_b), eo).reshape(bsz, seq, d)
    return x
```

```python
import jax
import jax.numpy as jnp
import numpy as np
from jax import lax
from jax.experimental import pallas as pl
from jax.experimental.pallas import tpu as pltpu

F32 = jnp.float32
BF16 = jnp.bfloat16
I32 = jnp.int32
I16 = jnp.int16

D_MODEL = 1024
CHUNK = 128
GMLP_GROUPS = 8
GROUP_CH = D_MODEL // GMLP_GROUPS
MEM_LEN = 256
XA_HEADS = 4
XA_HEAD_DIM = D_MODEL // XA_HEADS
N_EXPERTS = 32
TOP_K = 4
D_FF = D_MODEL
W_IN_COLS = 9 * D_MODEL
SWIGLU_LIMIT = 7.0
SWIGLU_ALPHA = 1.702
LN_EPS = 1e-5
DEEPNORM_ALPHA = 2.0 ** 0.25

LANES = 128
SUBLANES = 8
VMEM_LIMIT_BYTES = 60000 * 1024

TM_MIX = 512
TM_SORT = 256
SEG_ALIGN = SUBLANES
L_SORT = 1280
assert L_SORT >= TOP_K * TM_SORT + N_EXPERTS * (SEG_ALIGN - 1) and L_SORT % 256 == 0
assert L_SORT < 2 ** 15
N_CHUNKS = L_SORT // SEG_ALIGN
D_PACK = D_MODEL // 2
XS_COLS = D_PACK + LANES
TR = 512
TR_SUB = 256
assert TR % TR_SUB == 0


def _pack_pair(lo, hi):
    lo_bits = lax.shift_right_logical(lax.bitcast_convert_type(lo, I32), 16)
    return lo_bits | lax.bitcast_convert_type(hi, I32)


def _unpack_bf16(words):
    lo = lax.bitcast_convert_type(lax.shift_left(words, 16), F32)
    hi = lax.bitcast_convert_type(words & np.int32(-65536), F32)
    return jnp.concatenate([lo.astype(BF16), hi.astype(BF16)], axis=1)


def _layer_norm(x, g, b):
    mu = jnp.mean(x, axis=-1, keepdims=True)
    xc = x - mu
    var = jnp.mean(xc * xc, axis=-1, keepdims=True)
    return xc * lax.rsqrt(var + LN_EPS) * g + b


def _gelu_tanh(x):
    c = np.float32(np.sqrt(2.0 / np.pi))
    return x * (0.5 * (1.0 + jnp.tanh(c * (x + np.float32(0.044715) * (x * x * x)))))


def _sigmoid(x):
    return 1.0 / (1.0 + jnp.exp(-x))


def _dot(a, b):
    return jnp.dot(a, b, preferred_element_type=F32)


def _const_spec(shape):
    n = len(shape)
    return pl.BlockSpec(shape, lambda *_: (0,) * n, pipeline_mode=pl.Buffered(1))


def _kv_kernel(mem_ref, g_ref, b_ref, wkv_ref, k_ref, v_ref):
    mem_n = _layer_norm(mem_ref[0], g_ref[...], b_ref[...])
    kv = _dot(mem_n.astype(BF16), wkv_ref[...])
    k_ref[0] = kv[:, :D_MODEL].astype(BF16)
    v_ref[0] = kv[:, D_MODEL:].astype(BF16)


def _kv_call(mem, g, b, wkv):
    bsz = mem.shape[0]
    return pl.pallas_call(
        _kv_kernel,
        grid=(bsz,),
        in_specs=[
            pl.BlockSpec((1, MEM_LEN, D_MODEL), lambda i: (i, 0, 0)),
            pl.BlockSpec((1, D_MODEL), lambda i: (0, 0)),
            pl.BlockSpec((1, D_MODEL), lambda i: (0, 0)),
            pl.BlockSpec((D_MODEL, 2 * D_MODEL), lambda i: (0, 0)),
        ],
        out_specs=[
            pl.BlockSpec((1, MEM_LEN, D_MODEL), lambda i: (i, 0, 0)),
            pl.BlockSpec((1, MEM_LEN, D_MODEL), lambda i: (i, 0, 0)),
        ],
        out_shape=[
            jax.ShapeDtypeStruct((bsz, MEM_LEN, D_MODEL), BF16),
            jax.ShapeDtypeStruct((bsz, MEM_LEN, D_MODEL), BF16),
        ],
        compiler_params=pltpu.CompilerParams(dimension_semantics=("arbitrary",)),
        name="kv",
    )(mem, g, b, wkv)


def _mix_kernel(x_ref, win_ref, bgate_ref, convw_ref, ws_ref, gbias_ref, glng_ref, glnb_ref,
                k_ref, v_ref, wc_ref, wg_ref, wx_ref, wo_ref, ln1g_ref, ln1b_ref,
                wrh_ref, wrl_ref, br_ref, x1_ref, logit_ref, ubuf_ref):
    tm = x_ref.shape[1]
    d = D_MODEL
    x = x_ref[0]
    xb = x.astype(BF16)

    def proj(i):
        return _dot(xb, win_ref[:, i * d:(i + 1) * d])

    def gate(i):
        return _sigmoid(proj(6 + i) + bgate_ref[:, i * d:(i + 1) * d])

    @pl.when(pl.program_id(1) == 0)
    def _():
        ubuf_ref[0:SUBLANES, :] = jnp.zeros((SUBLANES, d), F32)

    u = proj(1) * proj(2)
    ubuf_ref[SUBLANES:SUBLANES + tm, :] = u
    u1 = ubuf_ref[SUBLANES - 1:SUBLANES - 1 + tm, :]
    u2 = ubuf_ref[SUBLANES - 2:SUBLANES - 2 + tm, :]
    conv = convw_ref[0:1, :] * u2 + convw_ref[1:2, :] * u1 + convw_ref[2:3, :] * u
    ubuf_ref[0:SUBLANES, :] = ubuf_ref[tm:tm + SUBLANES, :]
    y_conv = proj(0) * conv
    merged = gate(0) * _dot(y_conv.astype(BF16), wc_ref[...])

    gu = _gelu_tanh(proj(3))
    gv = _gelu_tanh(proj(4))
    vn = _layer_norm(gv, glng_ref[...], glnb_ref[...]).astype(BF16)
    n_chunks = tm // CHUNK
    row_i = lax.broadcasted_iota(I32, (CHUNK, CHUNK), 0)
    col_i = lax.broadcasted_iota(I32, (CHUNK, CHUNK), 1)
    causal = col_i <= row_i
    f_cols = []
    for g in range(GMLP_GROUPS):
        w_g = jnp.where(causal, ws_ref[g], 0.0).astype(BF16)
        rhs = jnp.concatenate(
            [vn[c * CHUNK:(c + 1) * CHUNK, g * GROUP_CH:(g + 1) * GROUP_CH] for c in range(n_chunks)],
            axis=1)
        fg = _dot(w_g, rhs)
        f_cols.append(jnp.concatenate(
            [fg[:, c * GROUP_CH:(c + 1) * GROUP_CH] for c in range(n_chunks)], axis=0))
    f = jnp.concatenate(f_cols, axis=1)
    gbias = jnp.concatenate([gbias_ref[...]] * n_chunks, axis=0)
    y_gmlp = gu * (f + gbias)
    merged = merged + gate(1) * _dot(y_gmlp.astype(BF16), wg_ref[...])

    q = proj(5).astype(BF16)
    heads = []
    for h in range(XA_HEADS):
        sl = slice(h * XA_HEAD_DIM, (h + 1) * XA_HEAD_DIM)
        s = lax.dot_general(q[:, sl], k_ref[0, :, sl], (((1,), (1,)), ((), ())),
                            preferred_element_type=F32) * np.float32(XA_HEAD_DIM ** -0.5)
        p = jnp.exp(s - jnp.max(s, axis=-1, keepdims=True))
        p = p * (1.0 / jnp.sum(p, axis=-1, keepdims=True))
        heads.append(_dot(p.astype(BF16), v_ref[0, :, sl]))
    y_xa = jnp.concatenate(heads, axis=1)
    merged = merged + gate(2) * _dot(y_xa.astype(BF16), wx_ref[...])

    mix = _dot(merged.astype(BF16), wo_ref[...])
    x1 = _layer_norm(np.float32(DEEPNORM_ALPHA) * x + mix, ln1g_ref[...], ln1b_ref[...])
    x1_ref[0] = x1

    x1h = x1.astype(BF16)
    x1l = (x1 - x1h.astype(F32)).astype(BF16)
    logit_ref[0] = (_dot(x1h, wrh_ref[...]) + _dot(x1l, wrh_ref[...]) + _dot(x1h, wrl_ref[...])
                    + br_ref[...])


def _mix_call(x, win, bgate, convw, ws, gbias, glng, glnb, k, v, wc, wg, wx, wo, ln1g, ln1b,
              wrh, wrl, br):
    bsz, seq, d = x.shape
    tm = min(TM_MIX, seq)
    assert seq % tm == 0 and tm % CHUNK == 0
    tile = lambda b, s: (b, s, 0)
    per_batch = lambda b, s: (b, 0, 0)
    return pl.pallas_call(
        _mix_kernel,
        grid=(bsz, seq // tm),
        in_specs=[
            pl.BlockSpec((1, tm, d), tile),
            _const_spec((d, W_IN_COLS)),
            _const_spec((1, 3 * d)),
            _const_spec((3, d)),
            _const_spec((GMLP_GROUPS, CHUNK, CHUNK)),
            _const_spec((CHUNK, d)),
            _const_spec((1, d)),
            _const_spec((1, d)),
            pl.BlockSpec((1, MEM_LEN, d), per_batch),
            pl.BlockSpec((1, MEM_LEN, d), per_batch),
            _const_spec((d, d)),
            _const_spec((d, d)),
            _const_spec((d, d)),
            _const_spec((d, d)),
            _const_spec((1, d)),
            _const_spec((1, d)),
            _const_spec((d, LANES)),
            _const_spec((d, LANES)),
            _const_spec((1, LANES)),
        ],
        out_specs=[pl.BlockSpec((1, tm, d), tile), pl.BlockSpec((1, tm, LANES), tile)],
        out_shape=[jax.ShapeDtypeStruct((bsz, seq, d), F32),
                   jax.ShapeDtypeStruct((bsz, seq, LANES), F32)],
        scratch_shapes=[pltpu.VMEM((tm + 2 * SUBLANES, d), F32)],
        compiler_params=pltpu.CompilerParams(
            dimension_semantics=("arbitrary", "arbitrary"), vmem_limit_bytes=VMEM_LIMIT_BYTES),
        name="mix",
    )(x, win, bgate, convw, ws, gbias, glng, glnb, k, v, wc, wg, wx, wo, ln1g, ln1b, wrh, wrl, br)


def _route_kernel(logit_ref, pos_ref, gate_ref, cnt_ref):
    tm = logit_ref.shape[0]
    work = logit_ref[...].T[0:N_EXPERTS, :]
    e_io = lax.broadcasted_iota(I32, (N_EXPERTS, tm), 0)
    neg = np.float32(-np.inf)
    vals, sels = [], []
    for _ in range(TOP_K):
        m = jnp.max(work, axis=0, keepdims=True)
        idx = jnp.min(jnp.where(work == m, e_io, LANES), axis=0, keepdims=True)
        sel = e_io == idx
        vals.append(m)
        sels.append(sel)
        work = jnp.where(sel, neg, work)
    ex = [jnp.exp(vk - vals[0]) for vk in vals]
    inv = 1.0 / (ex[0] + ex[1] + ex[2] + ex[3])
    gates = [e * inv for e in ex]

    oh = jnp.where(sels[0] | sels[1] | sels[2] | sels[3], 1.0, 0.0).astype(F32)
    cnt = jnp.sum(oh, axis=1, keepdims=True)
    t_r = lax.broadcasted_iota(I32, (tm, tm), 0)
    t_c = lax.broadcasted_iota(I32, (tm, tm), 1)
    before = jnp.where(t_r < t_c, 1.0, 0.0).astype(BF16)
    rank = _dot(oh.astype(BF16), before)
    units = jnp.floor((cnt + np.float32(SEG_ALIGN - 1)) * np.float32(1.0 / SEG_ALIGN))
    units_b = jnp.concatenate(
        [jnp.broadcast_to(units, (N_EXPERTS, LANES)), jnp.zeros((LANES - N_EXPERTS, LANES), F32)],
        axis=0).astype(BF16)
    e_r = lax.broadcasted_iota(I32, (N_EXPERTS, LANES), 0)
    e_c = lax.broadcasted_iota(I32, (N_EXPERTS, LANES), 1)
    below = jnp.where(e_c < e_r, 1.0, 0.0).astype(BF16)
    seg = _dot(below, units_b)[:, 0:1] * np.float32(SEG_ALIGN)
    slot = seg + rank

    pos_rows = [jnp.sum(jnp.where(s, slot, 0.0), axis=0, keepdims=True) for s in sels]
    zero_row = jnp.zeros((1, tm), F32)
    pos_ref[0] = jnp.concatenate(pos_rows + [zero_row] * (SUBLANES - TOP_K), axis=0).astype(I32)
    gate_ref[0] = jnp.concatenate(gates + [zero_row] * (SUBLANES - TOP_K), axis=0)
    cnt_ref[0] = jnp.broadcast_to(cnt, (N_EXPERTS, LANES)).astype(I32)


def _route_call(logits2d):
    n_tok = logits2d.shape[0]
    nt = n_tok // TM_SORT
    return pl.pallas_call(
        _route_kernel,
        grid=(nt,),
        in_specs=[pl.BlockSpec((TM_SORT, LANES), lambda i: (i, 0))],
        out_specs=[
            pl.BlockSpec((1, SUBLANES, TM_SORT), lambda i: (i, 0, 0)),
            pl.BlockSpec((1, SUBLANES, TM_SORT), lambda i: (i, 0, 0)),
            pl.BlockSpec((1, N_EXPERTS, LANES), lambda i: (i, 0, 0)),
        ],
        out_shape=[
            jax.ShapeDtypeStruct((nt, SUBLANES, TM_SORT), I32),
            jax.ShapeDtypeStruct((nt, SUBLANES, TM_SORT), F32),
            jax.ShapeDtypeStruct((nt, N_EXPERTS, LANES), I32),
        ],
        compiler_params=pltpu.CompilerParams(dimension_semantics=("arbitrary",)),
        name="route",
    )(logits2d)


def _chunk_copy(src_ref, src_row, dst_ref, dst_row, sem):
    return pltpu.make_async_copy(
        src_ref.at[pl.ds(src_row, SEG_ALIGN)], dst_ref.at[pl.ds(dst_row, SEG_ALIGN)], sem)


def _sort_kernel(slot_ref, x1_ref, pos_ref, gate_ref, xs_hbm, sbuf, sems):
    j = pl.program_id(0)
    nt = pl.num_programs(0)
    tm = x1_ref.shape[0]
    cur = lax.rem(j, 2)
    buf = sbuf.at[cur]
    j_io = lax.broadcasted_iota(I32, (L_SORT, tm), 0)
    hit = [j_io == pos_ref[0, k:k + 1, :] for k in range(TOP_K)]
    onehot = jnp.where(hit[0] | hit[1] | hit[2] | hit[3], 1.0, 0.0).astype(BF16)
    rows = _dot(onehot, x1_ref[...].astype(BF16))
    buf[:, 0:D_PACK] = _pack_pair(rows[:, 0:D_PACK], rows[:, D_PACK:D_MODEL])
    gsel = jnp.where(hit[0], gate_ref[0, 0:1, :], 0.0)
    for k in range(1, TOP_K):
        gsel = gsel + jnp.where(hit[k], gate_ref[0, k:k + 1, :], 0.0)
    gate_col = jnp.sum(gsel, axis=1, keepdims=True)
    buf[:, D_PACK:XS_COLS] = lax.bitcast_convert_type(
        jnp.broadcast_to(gate_col, (L_SORT, LANES)), I32)

    for i in range(N_CHUNKS):
        dst = pl.multiple_of(slot_ref[j * N_CHUNKS + i] * SEG_ALIGN, SEG_ALIGN)
        _chunk_copy(buf, i * SEG_ALIGN, xs_hbm, dst, sems.at[cur]).start()

    def drain(which):
        for _ in range(N_CHUNKS):
            _chunk_copy(sbuf.at[which], 0, xs_hbm, 0, sems.at[which]).wait()

    @pl.when(j > 0)
    def _():
        drain(1 - cur)

    @pl.when(j == nt - 1)
    def _():
        drain(cur)


def _sort_call(slots, x1, pos, gate):
    n_tok = x1.shape[0]
    nt = n_tok // TM_SORT
    grid_spec = pltpu.PrefetchScalarGridSpec(
        num_scalar_prefetch=1,
        grid=(nt,),
        in_specs=[
            pl.BlockSpec((TM_SORT, D_MODEL), lambda i, *_: (i, 0)),
            pl.BlockSpec((1, SUBLANES, TM_SORT), lambda i, *_: (i, 0, 0)),
            pl.BlockSpec((1, SUBLANES, TM_SORT), lambda i, *_: (i, 0, 0)),
        ],
        out_specs=pl.BlockSpec(memory_space=pl.ANY),
        scratch_shapes=[
            pltpu.VMEM((2, L_SORT, XS_COLS), I32),
            pltpu.SemaphoreType.DMA((2,)),
        ],
    )
    return pl.pallas_call(
        _sort_kernel,
        grid_spec=grid_spec,
        out_shape=jax.ShapeDtypeStruct((nt * L_SORT, XS_COLS), I32),
        compiler_params=pltpu.CompilerParams(
            dimension_semantics=("arbitrary",), vmem_limit_bytes=VMEM_LIMIT_BYTES),
        name="sort",
    )(slots, x1, pos, gate)


KIND_NOOP, KIND_FFN, KIND_ZERO = 0, 1, 2


def _expert_kernel(tile_ref, exp_ref, kind_ref, lo_ref, hi_ref, first_ref, fresh_ref,
                   xs_ref, wgu_ref, bgu_ref, wd_ref, bd_ref, out_ref, wgu_b, wd_b):
    w = pl.program_id(0)
    kind = kind_ref[w]

    @pl.when(kind == KIND_ZERO)
    def _():
        out_ref[...] = jnp.zeros(out_ref.shape, I32)

    @pl.when((kind == KIND_FFN) & (fresh_ref[w] == 1))
    def _():
        wgu_b[...] = wgu_ref[0].astype(BF16)
        wd_b[...] = wd_ref[0].astype(BF16)

    lo = lo_ref[w]
    hi = hi_ref[w]
    first = first_ref[w]
    is_ffn = kind == KIND_FFN
    whole = (lo == 0) & (hi == TR)

    def ffn(r0, nr):
        xs = _unpack_bf16(xs_ref[r0:r0 + nr, 0:D_PACK])
        gate = lax.bitcast_convert_type(xs_ref[r0:r0 + nr, D_PACK:D_PACK + 1], F32)
        gu = _dot(xs, wgu_b[...]) + bgu_ref[0]
        g_lin = jnp.minimum(gu[:, :D_FF], np.float32(SWIGLU_LIMIT))
        u_lin = jnp.clip(gu[:, D_FF:], np.float32(-SWIGLU_LIMIT), np.float32(SWIGLU_LIMIT))
        act = (u_lin + 1.0) * (g_lin * _sigmoid(np.float32(SWIGLU_ALPHA) * g_lin))
        eo = (_dot(act.astype(BF16), wd_b[...]) + bd_ref[0]) * gate
        eo = eo.astype(BF16).astype(F32)
        return _pack_pair(eo[:, 0:D_PACK], eo[:, D_PACK:D_MODEL])

    @pl.when(is_ffn & whole)
    def _():
        out_ref[...] = ffn(0, TR)

    for r0 in range(0, TR, TR_SUB):
        touched = (lo < r0 + TR_SUB) & (hi > r0)
        part = is_ffn & jnp.logical_not(whole)

        @pl.when(part & touched)
        def _(r0=r0):
            eo = ffn(r0, TR_SUB)
            rows = r0 + lax.broadcasted_iota(I32, (TR_SUB, 1), 0)
            mine = (rows >= lo) & (rows < hi)

            @pl.when(first == 1)
            def _():
                out_ref[r0:r0 + TR_SUB, :] = jnp.where(mine, eo, 0)

            @pl.when(first == 0)
            def _():
                out_ref[r0:r0 + TR_SUB, :] = jnp.where(mine, eo, out_ref[r0:r0 + TR_SUB, :])

        @pl.when(part & jnp.logical_not(touched) & (first == 1))
        def _(r0=r0):
            out_ref[r0:r0 + TR_SUB, :] = jnp.zeros((TR_SUB, D_PACK), I32)


def _expert_call(md, xs, wgu, bgu, wd, bd):
    pmax = xs.shape[0]
    n_items = md["tile"].shape[0]
    grid_spec = pltpu.PrefetchScalarGridSpec(
        num_scalar_prefetch=7,
        grid=(n_items,),
        in_specs=[
            pl.BlockSpec((TR, XS_COLS), lambda w, t, e, *_: (t[w], 0)),
            pl.BlockSpec((1, D_MODEL, 2 * D_FF), lambda w, t, e, *_: (e[w], 0, 0)),
            pl.BlockSpec((1, 1, 2 * D_FF), lambda w, t, e, *_: (e[w], 0, 0)),
            pl.BlockSpec((1, D_FF, D_MODEL), lambda w, t, e, *_: (e[w], 0, 0)),
            pl.BlockSpec((1, 1, D_MODEL), lambda w, t, e, *_: (e[w], 0, 0)),
        ],
        out_specs=pl.BlockSpec((TR, D_PACK), lambda w, t, e, *_: (t[w], 0)),
        scratch_shapes=[pltpu.VMEM((D_MODEL, 2 * D_FF), BF16), pltpu.VMEM((D_FF, D_MODEL), BF16)],
    )
    return pl.pallas_call(
        _expert_kernel,
        grid_spec=grid_spec,
        out_shape=jax.ShapeDtypeStruct((pmax, D_PACK), I32),
        compiler_params=pltpu.CompilerParams(
            dimension_semantics=("arbitrary",), vmem_limit_bytes=VMEM_LIMIT_BYTES),
        name="experts",
    )(md["tile"], md["exp"], md["kind"], md["lo"], md["hi"], md["first"], md["fresh"],
      xs, wgu, bgu, wd, bd)


def _combine_kernel(slot_ref, x1_ref, post_ref, g_ref, b_ref, eo_hbm, out_ref, ebuf, sems):
    j = pl.program_id(0)
    nt = pl.num_programs(0)
    tm = x1_ref.shape[0]
    cur = lax.rem(j, 2)

    def fetch(tile, which):
        for i in range(N_CHUNKS):
            src = pl.multiple_of(slot_ref[tile * N_CHUNKS + i] * SEG_ALIGN, SEG_ALIGN)
            _chunk_copy(eo_hbm, src, ebuf.at[which], i * SEG_ALIGN, sems.at[which]).start()

    @pl.when(j == 0)
    def _():
        fetch(0, 0)

    @pl.when(j + 1 < nt)
    def _():
        fetch(j + 1, 1 - cur)

    for _ in range(N_CHUNKS):
        _chunk_copy(eo_hbm, 0, ebuf.at[cur], 0, sems.at[cur]).wait()

    l_io = lax.broadcasted_iota(I32, (tm, L_SORT), 1).astype(I16)
    post = post_ref[0].astype(I16)
    hit = l_io == post[:, 0:1]
    for k in range(1, TOP_K):
        hit = hit | (l_io == post[:, k:k + 1])
    onehot = jnp.where(hit, jnp.ones((), BF16), jnp.zeros((), BF16))
    y = _dot(onehot, _unpack_bf16(ebuf[cur]))
    out_ref[...] = _layer_norm(np.float32(DEEPNORM_ALPHA) * x1_ref[...] + y, g_ref[...], b_ref[...])


def _combine_call(slots, x1, post, g, b, eo):
    n_tok = x1.shape[0]
    nt = n_tok // TM_SORT
    grid_spec = pltpu.PrefetchScalarGridSpec(
        num_scalar_prefetch=1,
        grid=(nt,),
        in_specs=[
            pl.BlockSpec((TM_SORT, D_MODEL), lambda i, *_: (i, 0)),
            pl.BlockSpec((1, TM_SORT, SUBLANES), lambda i, *_: (i, 0, 0)),
            pl.BlockSpec((1, D_MODEL), lambda i, *_: (0, 0)),
            pl.BlockSpec((1, D_MODEL), lambda i, *_: (0, 0)),
            pl.BlockSpec(memory_space=pl.ANY),
        ],
        out_specs=pl.BlockSpec((TM_SORT, D_MODEL), lambda i, *_: (i, 0)),
        scratch_shapes=[pltpu.VMEM((2, L_SORT, D_PACK), I32), pltpu.SemaphoreType.DMA((2,))],
    )
    return pl.pallas_call(
        _combine_kernel,
        grid_spec=grid_spec,
        out_shape=jax.ShapeDtypeStruct((n_tok, D_MODEL), F32),
        compiler_params=pltpu.CompilerParams(
            dimension_semantics=("arbitrary",), vmem_limit_bytes=VMEM_LIMIT_BYTES),
        name="combine",
    )(slots, x1, post, g, b, eo)


def _excl_cumsum(a, axis):
    n = a.shape[axis]
    lower = (jnp.arange(n)[:, None] > jnp.arange(n)[None, :]).astype(F32)
    af = jnp.moveaxis(a.astype(F32), axis, 0).reshape(n, -1)
    out = jnp.dot(lower, af, precision=lax.Precision.HIGHEST)
    out = out.reshape((n,) + tuple(np.delete(np.array(a.shape), axis)))
    return jnp.moveaxis(out, 0, axis).astype(I32)


def _routing_metadata(cnt):
    nt = cnt.shape[0]
    c8 = (cnt + (SEG_ALIGN - 1)) // SEG_ALIGN * SEG_ALIGN
    seg = _excl_cumsum(c8, 1)
    tot8 = jnp.sum(c8, axis=1)
    n8 = jnp.sum(c8, axis=0)
    base = _excl_cumsum(n8, 0)
    goff = base[None, :] + _excl_cumsum(c8, 0)
    p_used = jnp.sum(n8)

    r = (jnp.arange(N_CHUNKS, dtype=I32) * SEG_ALIGN)[None, :, None]
    in_seg = (r >= seg[:, None, :]) & (r < (seg + c8)[:, None, :])
    real_slot = jnp.sum(jnp.where(in_seg, goff[:, None, :] + r - seg[:, None, :], 0), axis=2) // SEG_ALIGN
    used = tot8 // SEG_ALIGN
    n_empty = N_CHUNKS - used
    ci = jnp.arange(N_CHUNKS, dtype=I32)[None, :]
    empty_slot = p_used // SEG_ALIGN + _excl_cumsum(n_empty, 0)[:, None] + (ci - used[:, None])
    slots = jnp.where(ci < used[:, None], real_slot, empty_slot).reshape(-1).astype(I32)

    assert (nt * L_SORT) % TR == 0
    n_tiles = nt * L_SORT // TR
    n_items = n_tiles + N_EXPERTS
    e_lo, e_hi = base, base + n8
    t_first = e_lo // TR
    t_end = (e_hi + TR - 1) // TR
    w_e = jnp.where(n8 > 0, t_end - t_first, 0)
    w_start = _excl_cumsum(w_e, 0)
    w_cum = w_start + w_e
    w_total = jnp.sum(w_e)
    tiles_used = (p_used + TR - 1) // TR

    w = jnp.arange(n_items, dtype=I32)
    e_ids = jnp.arange(N_EXPERTS, dtype=I32)
    e_w = jnp.minimum(jnp.sum((w[:, None] >= w_cum[None, :]).astype(I32), axis=1), N_EXPERTS - 1)
    is_ffn = w < w_total
    last_e = jnp.max(jnp.where(n8 > 0, e_ids, 0))
    exp = jnp.where(is_ffn, e_w, last_e).astype(I32)
    pick = exp[:, None] == e_ids[None, :]
    look = lambda table: jnp.sum(jnp.where(pick, table[None, :], 0), axis=1)
    ffn_tile = look(t_first) + (w - look(w_start))
    zero_tile = tiles_used + (w - w_total)
    is_zero = (~is_ffn) & (zero_tile < n_tiles)
    tile = jnp.where(is_ffn, ffn_tile, jnp.where(is_zero, zero_tile, n_tiles - 1)).astype(I32)
    kind = jnp.where(is_ffn, KIND_FFN, jnp.where(is_zero, KIND_ZERO, KIND_NOOP)).astype(I32)
    lo = jnp.clip(look(e_lo) - tile * TR, 0, TR).astype(I32)
    hi = jnp.clip(look(e_hi) - tile * TR, 0, TR).astype(I32)
    prev_tile = jnp.concatenate([jnp.full((1,), -1, I32), tile[:-1]])
    first = (tile != prev_tile).astype(I32)
    prev_exp = jnp.concatenate([jnp.full((1,), -1, I32), exp[:-1]])
    fresh = (exp != prev_exp).astype(I32)
    return dict(slots=slots, tile=tile, exp=exp, kind=kind, lo=lo, hi=hi, first=first, fresh=fresh)


def kernel(x, mem, w_in, b_gate, conv_w, gmlp_ws, gmlp_b, gmlp_ln_g, gmlp_ln_b, mem_ln_g,
           mem_ln_b, w_kv, w_conv_proj, w_gmlp_proj, w_xa_proj, w_out, ln1_g, ln1_b, w_router,
           b_router, w_gate_up, b_gate_up, w_down, b_down, ln2_g, ln2_b):
    bsz, seq, d = x.shape
    n_tok = bsz * seq
    assert d == D_MODEL and n_tok % TM_SORT == 0
    depth = w_in.shape[0]
    for l in range(depth):
        row = lambda a: a[l].reshape(1, -1)
        k, v = _kv_call(mem, row(mem_ln_g), row(mem_ln_b), w_kv[l].astype(BF16))
        gbias = jnp.repeat(gmlp_b[l].T, GROUP_CH, axis=1)
        wr = jnp.pad(w_router[l], ((0, 0), (0, LANES - N_EXPERTS)))
        wrh = wr.astype(BF16)
        wrl = (wr - wrh.astype(F32)).astype(BF16)
        br = jnp.pad(b_router[l], (0, LANES - N_EXPERTS)).reshape(1, LANES)
        x1, logits = _mix_call(
            x, w_in[l].astype(BF16), row(b_gate), conv_w[l], gmlp_ws[l], gbias,
            row(gmlp_ln_g), row(gmlp_ln_b), k, v, w_conv_proj[l].astype(BF16),
            w_gmlp_proj[l].astype(BF16), w_xa_proj[l].astype(BF16), w_out[l].astype(BF16),
            row(ln1_g), row(ln1_b), wrh, wrl, br)
        x1 = x1.reshape(n_tok, d)
        pos, gate, cnt = _route_call(logits.reshape(n_tok, LANES))
        md = _routing_metadata(cnt[:, :, 0])
        xs = _sort_call(md["slots"], x1, pos, gate)
        eo = _expert_call(md, xs, w_gate_up[l], b_gate_up[l][:, None, :],
                          w_down[l], b_down[l][:, None, :])
        post = jnp.transpose(pos, (0, 2, 1))
        x = _combine_call(md["slots"], x1, post, row(ln2_g), row(ln2_b), eo).reshape(bsz, seq, d)
    return x
```

```python
import jax
import jax.numpy as jnp
import numpy as np
from jax import lax
from jax.experimental import pallas as pl
from jax.experimental.pallas import tpu as pltpu

F32 = jnp.float32
BF16 = jnp.bfloat16
I32 = jnp.int32
I16 = jnp.int16

D_MODEL = 1024
CHUNK = 128
GMLP_GROUPS = 8
GROUP_CH = D_MODEL // GMLP_GROUPS
MEM_LEN = 256
XA_HEADS = 4
XA_HEAD_DIM = D_MODEL // XA_HEADS
N_EXPERTS = 32
TOP_K = 4
D_FF = D_MODEL
W_IN_COLS = 9 * D_MODEL
SWIGLU_LIMIT = 7.0
SWIGLU_ALPHA = 1.702
LN_EPS = 1e-5
DEEPNORM_ALPHA = 2.0 ** 0.25

LANES = 128
SUBLANES = 8
VMEM_LIMIT_BYTES = 60000 * 1024

TM_MIX = 512
TM_SORT = 256
SEG_ALIGN = SUBLANES
L_SORT = 1280
assert L_SORT >= TOP_K * TM_SORT + N_EXPERTS * (SEG_ALIGN - 1) and L_SORT % 256 == 0
assert L_SORT < 2 ** 15
N_CHUNKS = L_SORT // SEG_ALIGN
D_PACK = D_MODEL // 2
XS_COLS = D_PACK + LANES
TR = 512
TR_SUB = 256
assert TR % TR_SUB == 0


def _pack_pair(lo, hi):
    lo_bits = lax.shift_right_logical(lax.bitcast_convert_type(lo, I32), 16)
    return lo_bits | lax.bitcast_convert_type(hi, I32)


def _unpack_bf16(words):
    lo = lax.bitcast_convert_type(lax.shift_left(words, 16), F32)
    hi = lax.bitcast_convert_type(words & np.int32(-65536), F32)
    return jnp.concatenate([lo.astype(BF16), hi.astype(BF16)], axis=1)


def _layer_norm(x, g, b):
    mu = jnp.mean(x, axis=-1, keepdims=True)
    xc = x - mu
    var = jnp.mean(xc * xc, axis=-1, keepdims=True)
    return xc * lax.rsqrt(var + LN_EPS) * g + b


def _gelu_tanh(x):
    c = np.float32(np.sqrt(2.0 / np.pi))
    return x * (0.5 * (1.0 + jnp.tanh(c * (x + np.float32(0.044715) * (x * x * x)))))


def _sigmoid(x):
    return 1.0 / (1.0 + jnp.exp(-x))


def _dot(a, b):
    return jnp.dot(a, b, preferred_element_type=F32)


def _const_spec(shape):
    n = len(shape)
    return pl.BlockSpec(shape, lambda *_: (0,) * n, pipeline_mode=pl.Buffered(1))


def _kv_kernel(mem_ref, g_ref, b_ref, wkv_ref, k_ref, v_ref):
    mem_n = _layer_norm(mem_ref[0], g_ref[...], b_ref[...])
    kv = _dot(mem_n.astype(BF16), wkv_ref[...])
    k_ref[0] = kv[:, :D_MODEL].astype(BF16)
    v_ref[0] = kv[:, D_MODEL:].astype(BF16)


def _kv_call(mem, g, b, wkv):
    bsz = mem.shape[0]
    return pl.pallas_call(
        _kv_kernel,
        grid=(bsz,),
        in_specs=[
            pl.BlockSpec((1, MEM_LEN, D_MODEL), lambda i: (i, 0, 0)),
            pl.BlockSpec((1, D_MODEL), lambda i: (0, 0)),
            pl.BlockSpec((1, D_MODEL), lambda i: (0, 0)),
            pl.BlockSpec((D_MODEL, 2 * D_MODEL), lambda i: (0, 0)),
        ],
        out_specs=[
            pl.BlockSpec((1, MEM_LEN, D_MODEL), lambda i: (i, 0, 0)),
            pl.BlockSpec((1, MEM_LEN, D_MODEL), lambda i: (i, 0, 0)),
        ],
        out_shape=[
            jax.ShapeDtypeStruct((bsz, MEM_LEN, D_MODEL), BF16),
            jax.ShapeDtypeStruct((bsz, MEM_LEN, D_MODEL), BF16),
        ],
        compiler_params=pltpu.CompilerParams(dimension_semantics=("arbitrary",)),
        name="kv",
    )(mem, g, b, wkv)


def _mix_kernel(x_ref, win_ref, bgate_ref, convw_ref, ws_ref, gbias_ref, glng_ref, glnb_ref,
                k_ref, v_ref, wc_ref, wg_ref, wx_ref, wo_ref, ln1g_ref, ln1b_ref,
                wrh_ref, wrl_ref, br_ref, x1_ref, logit_ref, ubuf_ref):
    tm = x_ref.shape[1]
    d = D_MODEL
    x = x_ref[0]
    xb = x.astype(BF16)

    def proj(i):
        return _dot(xb, win_ref[:, i * d:(i + 1) * d])

    def gate(i):
        return _sigmoid(proj(6 + i) + bgate_ref[:, i * d:(i + 1) * d])

    @pl.when(pl.program_id(1) == 0)
    def _():
        ubuf_ref[0:SUBLANES, :] = jnp.zeros((SUBLANES, d), F32)

    u = proj(1) * proj(2)
    ubuf_ref[SUBLANES:SUBLANES + tm, :] = u
    u1 = ubuf_ref[SUBLANES - 1:SUBLANES - 1 + tm, :]
    u2 = ubuf_ref[SUBLANES - 2:SUBLANES - 2 + tm, :]
    conv = convw_ref[0:1, :] * u2 + convw_ref[1:2, :] * u1 + convw_ref[2:3, :] * u
    ubuf_ref[0:SUBLANES, :] = ubuf_ref[tm:tm + SUBLANES, :]
    y_conv = proj(0) * conv
    merged = gate(0) * _dot(y_conv.astype(BF16), wc_ref[...])

    gu = _gelu_tanh(proj(3))
    gv = _gelu_tanh(proj(4))
    vn = _layer_norm(gv, glng_ref[...], glnb_ref[...]).astype(BF16)
    n_chunks = tm // CHUNK
    row_i = lax.broadcasted_iota(I32, (CHUNK, CHUNK), 0)
    col_i = lax.broadcasted_iota(I32, (CHUNK, CHUNK), 1)
    causal = col_i <= row_i
    f_cols = []
    for g in range(GMLP_GROUPS):
        w_g = jnp.where(causal, ws_ref[g], 0.0).astype(BF16)
        rhs = jnp.concatenate(
            [vn[c * CHUNK:(c + 1) * CHUNK, g * GROUP_CH:(g + 1) * GROUP_CH] for c in range(n_chunks)],
            axis=1)
        fg = _dot(w_g, rhs)
        f_cols.append(jnp.concatenate(
            [fg[:, c * GROUP_CH:(c + 1) * GROUP_CH] for c in range(n_chunks)], axis=0))
    f = jnp.concatenate(f_cols, axis=1)
    gbias = jnp.concatenate([gbias_ref[...]] * n_chunks, axis=0)
    y_gmlp = gu * (f + gbias)
    merged = merged + gate(1) * _dot(y_gmlp.astype(BF16), wg_ref[...])

    q = proj(5).astype(BF16)
    heads = []
    for h in range(XA_HEADS):
        sl = slice(h * XA_HEAD_DIM, (h + 1) * XA_HEAD_DIM)
        s = lax.dot_general(q[:, sl], k_ref[0, :, sl], (((1,), (1,)), ((), ())),
                            preferred_element_type=F32) * np.float32(XA_HEAD_DIM ** -0.5)
        p = jnp.exp(s - jnp.max(s, axis=-1, keepdims=True))
        p = p * (1.0 / jnp.sum(p, axis=-1, keepdims=True))
        heads.append(_dot(p.astype(BF16), v_ref[0, :, sl]))
    y_xa = jnp.concatenate(heads, axis=1)
    merged = merged + gate(2) * _dot(y_xa.astype(BF16), wx_ref[...])

    mix = _dot(merged.astype(BF16), wo_ref[...])
    x1 = _layer_norm(np.float32(DEEPNORM_ALPHA) * x + mix, ln1g_ref[...], ln1b_ref[...])
    x1_ref[0] = x1

    x1h = x1.astype(BF16)
    x1l = (x1 - x1h.astype(F32)).astype(BF16)
    logit_ref[0] = (_dot(x1h, wrh_ref[...]) + _dot(x1l, wrh_ref[...]) + _dot(x1h, wrl_ref[...])
                    + br_ref[...])


def _mix_call(x, win, bgate, convw, ws, gbias, glng, glnb, k, v, wc, wg, wx, wo, ln1g, ln1b,
              wrh, wrl, br):
    bsz, seq, d = x.shape
    tm = min(TM_MIX, seq)
    assert seq % tm == 0 and tm % CHUNK == 0
    tile = lambda b, s: (b, s, 0)
    per_batch = lambda b, s: (b, 0, 0)
    return pl.pallas_call(
        _mix_kernel,
        grid=(bsz, seq // tm),
        in_specs=[
            pl.BlockSpec((1, tm, d), tile),
            _const_spec((d, W_IN_COLS)),
            _const_spec((1, 3 * d)),
            _const_spec((3, d)),
            _const_spec((GMLP_GROUPS, CHUNK, CHUNK)),
            _const_spec((CHUNK, d)),
            _const_spec((1, d)),
            _const_spec((1, d)),
            pl.BlockSpec((1, MEM_LEN, d), per_batch),
            pl.BlockSpec((1, MEM_LEN, d), per_batch),
            _const_spec((d, d)),
            _const_spec((d, d)),
            _const_spec((d, d)),
            _const_spec((d, d)),
            _const_spec((1, d)),
            _const_spec((1, d)),
            _const_spec((d, LANES)),
            _const_spec((d, LANES)),
            _const_spec((1, LANES)),
        ],
        out_specs=[pl.BlockSpec((1, tm, d), tile), pl.BlockSpec((1, tm, LANES), tile)],
        out_shape=[jax.ShapeDtypeStruct((bsz, seq, d), F32),
                   jax.ShapeDtypeStruct((bsz, seq, LANES), F32)],
        scratch_shapes=[pltpu.VMEM((tm + 2 * SUBLANES, d), F32)],
        compiler_params=pltpu.CompilerParams(
            dimension_semantics=("arbitrary", "arbitrary"), vmem_limit_bytes=VMEM_LIMIT_BYTES),
        name="mix",
    )(x, win, bgate, convw, ws, gbias, glng, glnb, k, v, wc, wg, wx, wo, ln1g, ln1b, wrh, wrl, br)


def _route_kernel(logit_ref, pos_ref, gate_ref, cnt_ref):
    tm = logit_ref.shape[0]
    work = logit_ref[...].T[0:N_EXPERTS, :]
    e_io = lax.broadcasted_iota(I32, (N_EXPERTS, tm), 0)
    neg = np.float32(-np.inf)
    vals, sels = [], []
    for _ in range(TOP_K):
        m = jnp.max(work, axis=0, keepdims=True)
        idx = jnp.min(jnp.where(work == m, e_io, LANES), axis=0, keepdims=True)
        sel = e_io == idx
        vals.append(m)
        sels.append(sel)
        work = jnp.where(sel, neg, work)
    ex = [jnp.exp(vk - vals[0]) for vk in vals]
    inv = 1.0 / (ex[0] + ex[1] + ex[2] + ex[3])
    gates = [e * inv for e in ex]

    oh = jnp.where(sels[0] | sels[1] | sels[2] | sels[3], 1.0, 0.0).astype(F32)
    cnt = jnp.sum(oh, axis=1, keepdims=True)
    t_r = lax.broadcasted_iota(I32, (tm, tm), 0)
    t_c = lax.broadcasted_iota(I32, (tm, tm), 1)
    before = jnp.where(t_r < t_c, 1.0, 0.0).astype(BF16)
    rank = _dot(oh.astype(BF16), before)
    units = jnp.floor((cnt + np.float32(SEG_ALIGN - 1)) * np.float32(1.0 / SEG_ALIGN))
    units_b = jnp.concatenate(
        [jnp.broadcast_to(units, (N_EXPERTS, LANES)), jnp.zeros((LANES - N_EXPERTS, LANES), F32)],
        axis=0).astype(BF16)
    e_r = lax.broadcasted_iota(I32, (N_EXPERTS, LANES), 0)
    e_c = lax.broadcasted_iota(I32, (N_EXPERTS, LANES), 1)
    below = jnp.where(e_c < e_r, 1.0, 0.0).astype(BF16)
    seg = _dot(below, units_b)[:, 0:1] * np.float32(SEG_ALIGN)
    slot = seg + rank

    pos_rows = [jnp.sum(jnp.where(s, slot, 0.0), axis=0, keepdims=True) for s in sels]
    zero_row = jnp.zeros((1, tm), F32)
    pos_ref[0] = jnp.concatenate(pos_rows + [zero_row] * (SUBLANES - TOP_K), axis=0).astype(I32)
    gate_ref[0] = jnp.concatenate(gates + [zero_row] * (SUBLANES - TOP_K), axis=0)
    cnt_ref[0] = jnp.broadcast_to(cnt, (N_EXPERTS, LANES)).astype(I32)


def _route_call(logits2d):
    n_tok = logits2d.shape[0]
    nt = n_tok // TM_SORT
    return pl.pallas_call(
        _route_kernel,
        grid=(nt,),
        in_specs=[pl.BlockSpec((TM_SORT, LANES), lambda i: (i, 0))],
        out_specs=[
            pl.BlockSpec((1, SUBLANES, TM_SORT), lambda i: (i, 0, 0)),
            pl.BlockSpec((1, SUBLANES, TM_SORT), lambda i: (i, 0, 0)),
            pl.BlockSpec((1, N_EXPERTS, LANES), lambda i: (i, 0, 0)),
        ],
        out_shape=[
            jax.ShapeDtypeStruct((nt, SUBLANES, TM_SORT), I32),
            jax.ShapeDtypeStruct((nt, SUBLANES, TM_SORT), F32),
            jax.ShapeDtypeStruct((nt, N_EXPERTS, LANES), I32),
        ],
        compiler_params=pltpu.CompilerParams(dimension_semantics=("arbitrary",)),
        name="route",
    )(logits2d)


def _chunk_copy(src_ref, src_row, dst_ref, dst_row, sem):
    return pltpu.make_async_copy(
        src_ref.at[pl.ds(src_row, SEG_ALIGN)], dst_ref.at[pl.ds(dst_row, SEG_ALIGN)], sem)


def _sort_kernel(slot_ref, x1_ref, pos_ref, gate_ref, xs_hbm, sbuf, sems):
    j = pl.program_id(0)
    nt = pl.num_programs(0)
    tm = x1_ref.shape[0]
    cur = lax.rem(j, 2)
    buf = sbuf.at[cur]
    j_io = lax.broadcasted_iota(I32, (L_SORT, tm), 0)
    hit = [j_io == pos_ref[0, k:k + 1, :] for k in range(TOP_K)]
    onehot = jnp.where(hit[0] | hit[1] | hit[2] | hit[3], 1.0, 0.0).astype(BF16)
    rows = _dot(onehot, x1_ref[...].astype(BF16))
    buf[:, 0:D_PACK] = _pack_pair(rows[:, 0:D_PACK], rows[:, D_PACK:D_MODEL])
    gsel = jnp.where(hit[0], gate_ref[0, 0:1, :], 0.0)
    for k in range(1, TOP_K):
        gsel = gsel + jnp.where(hit[k], gate_ref[0, k:k + 1, :], 0.0)
    gate_col = jnp.sum(gsel, axis=1, keepdims=True)
    buf[:, D_PACK:XS_COLS] = lax.bitcast_convert_type(
        jnp.broadcast_to(gate_col, (L_SORT, LANES)), I32)

    for i in range(N_CHUNKS):
        dst = pl.multiple_of(slot_ref[j * N_CHUNKS + i] * SEG_ALIGN, SEG_ALIGN)
        _chunk_copy(buf, i * SEG_ALIGN, xs_hbm, dst, sems.at[cur]).start()

    def drain(which):
        for _ in range(N_CHUNKS):
            _chunk_copy(sbuf.at[which], 0, xs_hbm, 0, sems.at[which]).wait()

    @pl.when(j > 0)
    def _():
        drain(1 - cur)

    @pl.when(j == nt - 1)
    def _():
        drain(cur)


def _sort_call(slots, x1, pos, gate):
    n_tok = x1.shape[0]
    nt = n_tok // TM_SORT
    grid_spec = pltpu.PrefetchScalarGridSpec(
        num_scalar_prefetch=1,
        grid=(nt,),
        in_specs=[
            pl.BlockSpec((TM_SORT, D_MODEL), lambda i, *_: (i, 0)),
            pl.BlockSpec((1, SUBLANES, TM_SORT), lambda i, *_: (i, 0, 0)),
            pl.BlockSpec((1, SUBLANES, TM_SORT), lambda i, *_: (i, 0, 0)),
        ],
        out_specs=pl.BlockSpec(memory_space=pl.ANY),
        scratch_shapes=[
            pltpu.VMEM((2, L_SORT, XS_COLS), I32),
            pltpu.SemaphoreType.DMA((2,)),
        ],
    )
    return pl.pallas_call(
        _sort_kernel,
        grid_spec=grid_spec,
        out_shape=jax.ShapeDtypeStruct((nt * L_SORT, XS_COLS), I32),
        compiler_params=pltpu.CompilerParams(
            dimension_semantics=("arbitrary",), vmem_limit_bytes=VMEM_LIMIT_BYTES),
        name="sort",
    )(slots, x1, pos, gate)


KIND_NOOP, KIND_FFN, KIND_ZERO = 0, 1, 2


def _expert_kernel(tile_ref, exp_ref, kind_ref, lo_ref, hi_ref, first_ref, fresh_ref, wslot_ref,
                   next_ref, xs_ref, wgu_hbm, bgu_ref, wd_hbm, bd_ref, out_ref,
                   wgu_b, wd_b, stage_gu, stage_d, wsems):
    w = pl.program_id(0)
    kind = kind_ref[w]

    @pl.when(kind == KIND_ZERO)
    def _():
        out_ref[...] = jnp.zeros(out_ref.shape, I32)

    def weight_copies(e, s):
        return (pltpu.make_async_copy(wgu_hbm.at[e], stage_gu.at[s], wsems.at[0, s]),
                pltpu.make_async_copy(wd_hbm.at[e], stage_d.at[s], wsems.at[1, s]))

    @pl.when((kind == KIND_FFN) & (fresh_ref[w] == 1))
    def _():
        e = exp_ref[w]
        s = wslot_ref[w]

        @pl.when(w == 0)
        def _():
            for cp in weight_copies(e, s):
                cp.start()

        for cp in weight_copies(e, s):
            cp.wait()
        wgu_b[...] = stage_gu[s].astype(BF16)
        wd_b[...] = stage_d[s].astype(BF16)

        @pl.when(next_ref[w] >= 0)
        def _():
            for cp in weight_copies(next_ref[w], 1 - s):
                cp.start()

    lo = lo_ref[w]
    hi = hi_ref[w]
    first = first_ref[w]
    is_ffn = kind == KIND_FFN
    whole = (lo == 0) & (hi == TR)

    def ffn(r0, nr):
        xs = _unpack_bf16(xs_ref[r0:r0 + nr, 0:D_PACK])
        gate = lax.bitcast_convert_type(xs_ref[r0:r0 + nr, D_PACK:D_PACK + 1], F32)
        gu = _dot(xs, wgu_b[...]) + bgu_ref[0]
        g_lin = jnp.minimum(gu[:, :D_FF], np.float32(SWIGLU_LIMIT))
        u_lin = jnp.clip(gu[:, D_FF:], np.float32(-SWIGLU_LIMIT), np.float32(SWIGLU_LIMIT))
        act = (u_lin + 1.0) * (g_lin * _sigmoid(np.float32(SWIGLU_ALPHA) * g_lin))
        eo = (_dot(act.astype(BF16), wd_b[...]) + bd_ref[0]) * gate
        eo = eo.astype(BF16).astype(F32)
        return _pack_pair(eo[:, 0:D_PACK], eo[:, D_PACK:D_MODEL])

    @pl.when(is_ffn & whole)
    def _():
        out_ref[...] = ffn(0, TR)

    for r0 in range(0, TR, TR_SUB):
        touched = (lo < r0 + TR_SUB) & (hi > r0)
        part = is_ffn & jnp.logical_not(whole)

        @pl.when(part & touched)
        def _(r0=r0):
            eo = ffn(r0, TR_SUB)
            rows = r0 + lax.broadcasted_iota(I32, (TR_SUB, 1), 0)
            mine = (rows >= lo) & (rows < hi)

            @pl.when(first == 1)
            def _():
                out_ref[r0:r0 + TR_SUB, :] = jnp.where(mine, eo, 0)

            @pl.when(first == 0)
            def _():
                out_ref[r0:r0 + TR_SUB, :] = jnp.where(mine, eo, out_ref[r0:r0 + TR_SUB, :])

        @pl.when(part & jnp.logical_not(touched) & (first == 1))
        def _(r0=r0):
            out_ref[r0:r0 + TR_SUB, :] = jnp.zeros((TR_SUB, D_PACK), I32)


def _expert_call(md, xs, wgu, bgu, wd, bd):
    pmax = xs.shape[0]
    n_items = md["tile"].shape[0]
    grid_spec = pltpu.PrefetchScalarGridSpec(
        num_scalar_prefetch=9,
        grid=(n_items,),
        in_specs=[
            pl.BlockSpec((TR, XS_COLS), lambda w, t, e, *_: (t[w], 0)),
            pl.BlockSpec(memory_space=pl.ANY),
            pl.BlockSpec((1, 1, 2 * D_FF), lambda w, t, e, *_: (e[w], 0, 0)),
            pl.BlockSpec(memory_space=pl.ANY),
            pl.BlockSpec((1, 1, D_MODEL), lambda w, t, e, *_: (e[w], 0, 0)),
        ],
        out_specs=pl.BlockSpec((TR, D_PACK), lambda w, t, e, *_: (t[w], 0)),
        scratch_shapes=[
            pltpu.VMEM((D_MODEL, 2 * D_FF), BF16),
            pltpu.VMEM((D_FF, D_MODEL), BF16),
            pltpu.VMEM((2, D_MODEL, 2 * D_FF), F32),
            pltpu.VMEM((2, D_FF, D_MODEL), F32),
            pltpu.SemaphoreType.DMA((2, 2)),
        ],
    )
    return pl.pallas_call(
        _expert_kernel,
        grid_spec=grid_spec,
        out_shape=jax.ShapeDtypeStruct((pmax, D_PACK), I32),
        compiler_params=pltpu.CompilerParams(
            dimension_semantics=("arbitrary",), vmem_limit_bytes=VMEM_LIMIT_BYTES),
        name="experts",
    )(md["tile"], md["exp"], md["kind"], md["lo"], md["hi"], md["first"], md["fresh"],
      md["wslot"], md["next"], xs, wgu, bgu, wd, bd)


def _combine_kernel(slot_ref, x1_ref, post_ref, g_ref, b_ref, eo_hbm, out_ref, ebuf, sems):
    j = pl.program_id(0)
    nt = pl.num_programs(0)
    tm = x1_ref.shape[0]
    cur = lax.rem(j, 2)

    def fetch(tile, which):
        for i in range(N_CHUNKS):
            src = pl.multiple_of(slot_ref[tile * N_CHUNKS + i] * SEG_ALIGN, SEG_ALIGN)
            _chunk_copy(eo_hbm, src, ebuf.at[which], i * SEG_ALIGN, sems.at[which]).start()

    @pl.when(j == 0)
    def _():
        fetch(0, 0)

    @pl.when(j + 1 < nt)
    def _():
        fetch(j + 1, 1 - cur)

    for _ in range(N_CHUNKS):
        _chunk_copy(eo_hbm, 0, ebuf.at[cur], 0, sems.at[cur]).wait()

    l_io = lax.broadcasted_iota(I32, (tm, L_SORT), 1).astype(I16)
    post = post_ref[0].astype(I16)
    hit = l_io == post[:, 0:1]
    for k in range(1, TOP_K):
        hit = hit | (l_io == post[:, k:k + 1])
    onehot = jnp.where(hit, jnp.ones((), BF16), jnp.zeros((), BF16))
    y = _dot(onehot, _unpack_bf16(ebuf[cur]))
    out_ref[...] = _layer_norm(np.float32(DEEPNORM_ALPHA) * x1_ref[...] + y, g_ref[...], b_ref[...])


def _combine_call(slots, x1, post, g, b, eo):
    n_tok = x1.shape[0]
    nt = n_tok // TM_SORT
    grid_spec = pltpu.PrefetchScalarGridSpec(
        num_scalar_prefetch=1,
        grid=(nt,),
        in_specs=[
            pl.BlockSpec((TM_SORT, D_MODEL), lambda i, *_: (i, 0)),
            pl.BlockSpec((1, TM_SORT, SUBLANES), lambda i, *_: (i, 0, 0)),
            pl.BlockSpec((1, D_MODEL), lambda i, *_: (0, 0)),
            pl.BlockSpec((1, D_MODEL), lambda i, *_: (0, 0)),
            pl.BlockSpec(memory_space=pl.ANY),
        ],
        out_specs=pl.BlockSpec((TM_SORT, D_MODEL), lambda i, *_: (i, 0)),
        scratch_shapes=[pltpu.VMEM((2, L_SORT, D_PACK), I32), pltpu.SemaphoreType.DMA((2,))],
    )
    return pl.pallas_call(
        _combine_kernel,
        grid_spec=grid_spec,
        out_shape=jax.ShapeDtypeStruct((n_tok, D_MODEL), F32),
        compiler_params=pltpu.CompilerParams(
            dimension_semantics=("arbitrary",), vmem_limit_bytes=VMEM_LIMIT_BYTES),
        name="combine",
    )(slots, x1, post, g, b, eo)


def _excl_cumsum(a, axis):
    n = a.shape[axis]
    lower = (jnp.arange(n)[:, None] > jnp.arange(n)[None, :]).astype(F32)
    af = jnp.moveaxis(a.astype(F32), axis, 0).reshape(n, -1)
    out = jnp.dot(lower, af, precision=lax.Precision.HIGHEST)
    out = out.reshape((n,) + tuple(np.delete(np.array(a.shape), axis)))
    return jnp.moveaxis(out, 0, axis).astype(I32)


def _routing_metadata(cnt):
    nt = cnt.shape[0]
    c8 = (cnt + (SEG_ALIGN - 1)) // SEG_ALIGN * SEG_ALIGN
    seg = _excl_cumsum(c8, 1)
    tot8 = jnp.sum(c8, axis=1)
    n8 = jnp.sum(c8, axis=0)
    base = _excl_cumsum(n8, 0)
    goff = base[None, :] + _excl_cumsum(c8, 0)
    p_used = jnp.sum(n8)

    r = (jnp.arange(N_CHUNKS, dtype=I32) * SEG_ALIGN)[None, :, None]
    in_seg = (r >= seg[:, None, :]) & (r < (seg + c8)[:, None, :])
    real_slot = jnp.sum(jnp.where(in_seg, goff[:, None, :] + r - seg[:, None, :], 0), axis=2) // SEG_ALIGN
    used = tot8 // SEG_ALIGN
    n_empty = N_CHUNKS - used
    ci = jnp.arange(N_CHUNKS, dtype=I32)[None, :]
    empty_slot = p_used // SEG_ALIGN + _excl_cumsum(n_empty, 0)[:, None] + (ci - used[:, None])
    slots = jnp.where(ci < used[:, None], real_slot, empty_slot).reshape(-1).astype(I32)

    assert (nt * L_SORT) % TR == 0
    n_tiles = nt * L_SORT // TR
    n_items = n_tiles + N_EXPERTS
    e_lo, e_hi = base, base + n8
    t_first = e_lo // TR
    t_end = (e_hi + TR - 1) // TR
    w_e = jnp.where(n8 > 0, t_end - t_first, 0)
    w_start = _excl_cumsum(w_e, 0)
    w_cum = w_start + w_e
    w_total = jnp.sum(w_e)
    tiles_used = (p_used + TR - 1) // TR

    w = jnp.arange(n_items, dtype=I32)
    e_ids = jnp.arange(N_EXPERTS, dtype=I32)
    e_w = jnp.minimum(jnp.sum((w[:, None] >= w_cum[None, :]).astype(I32), axis=1), N_EXPERTS - 1)
    is_ffn = w < w_total
    last_e = jnp.max(jnp.where(n8 > 0, e_ids, 0))
    exp = jnp.where(is_ffn, e_w, last_e).astype(I32)
    pick = exp[:, None] == e_ids[None, :]
    look = lambda table: jnp.sum(jnp.where(pick, table[None, :], 0), axis=1)
    ffn_tile = look(t_first) + (w - look(w_start))
    zero_tile = tiles_used + (w - w_total)
    is_zero = (~is_ffn) & (zero_tile < n_tiles)
    tile = jnp.where(is_ffn, ffn_tile, jnp.where(is_zero, zero_tile, n_tiles - 1)).astype(I32)
    kind = jnp.where(is_ffn, KIND_FFN, jnp.where(is_zero, KIND_ZERO, KIND_NOOP)).astype(I32)
    lo = jnp.clip(look(e_lo) - tile * TR, 0, TR).astype(I32)
    hi = jnp.clip(look(e_hi) - tile * TR, 0, TR).astype(I32)
    prev_tile = jnp.concatenate([jnp.full((1,), -1, I32), tile[:-1]])
    first = (tile != prev_tile).astype(I32)
    prev_exp = jnp.concatenate([jnp.full((1,), -1, I32), exp[:-1]])
    fresh = (exp != prev_exp).astype(I32)
    used_e = n8 > 0
    later = (e_ids[None, :] > e_ids[:, None]) & used_e[None, :]
    order_e = jnp.sum(((e_ids[None, :] < e_ids[:, None]) & used_e[None, :]).astype(I32), axis=1)
    next_e = jnp.min(jnp.where(later, e_ids[None, :], N_EXPERTS), axis=1)
    next_e = jnp.where(next_e == N_EXPERTS, -1, next_e)
    wslot = (look(order_e) % 2).astype(I32)
    nxt = look(next_e).astype(I32)
    return dict(slots=slots, tile=tile, exp=exp, kind=kind, lo=lo, hi=hi, first=first, fresh=fresh,
                wslot=wslot, next=nxt)


def kernel(x, mem, w_in, b_gate, conv_w, gmlp_ws, gmlp_b, gmlp_ln_g, gmlp_ln_b, mem_ln_g,
           mem_ln_b, w_kv, w_conv_proj, w_gmlp_proj, w_xa_proj, w_out, ln1_g, ln1_b, w_router,
           b_router, w_gate_up, b_gate_up, w_down, b_down, ln2_g, ln2_b):
    bsz, seq, d = x.shape
    n_tok = bsz * seq
    assert d == D_MODEL and n_tok % TM_SORT == 0
    depth = w_in.shape[0]
    for l in range(depth):
        row = lambda a: a[l].reshape(1, -1)
        k, v = _kv_call(mem, row(mem_ln_g), row(mem_ln_b), w_kv[l].astype(BF16))
        gbias = jnp.repeat(gmlp_b[l].T, GROUP_CH, axis=1)
        wr = jnp.pad(w_router[l], ((0, 0), (0, LANES - N_EXPERTS)))
        wrh = wr.astype(BF16)
        wrl = (wr - wrh.astype(F32)).astype(BF16)
        br = jnp.pad(b_router[l], (0, LANES - N_EXPERTS)).reshape(1, LANES)
        x1, logits = _mix_call(
            x, w_in[l].astype(BF16), row(b_gate), conv_w[l], gmlp_ws[l], gbias,
            row(gmlp_ln_g), row(gmlp_ln_b), k, v, w_conv_proj[l].astype(BF16),
            w_gmlp_proj[l].astype(BF16), w_xa_proj[l].astype(BF16), w_out[l].astype(BF16),
            row(ln1_g), row(ln1_b), wrh, wrl, br)
        x1 = x1.reshape(n_tok, d)
        pos, gate, cnt = _route_call(logits.reshape(n_tok, LANES))
        md = _routing_metadata(cnt[:, :, 0])
        xs = _sort_call(md["slots"], x1, pos, gate)
        eo = _expert_call(md, xs, w_gate_up[l], b_gate_up[l][:, None, :],
                          w_down[l], b_down[l][:, None, :])
        post = jnp.transpose(pos, (0, 2, 1))
        x = _combine_call(md["slots"], x1, post, row(ln2_g), row(ln2_b), eo).reshape(bsz, seq, d)
    return x
```

```python
import jax
import jax.numpy as jnp
import numpy as np
from jax import lax
from jax.experimental import pallas as pl
from jax.experimental.pallas import tpu as pltpu

F32 = jnp.float32
BF16 = jnp.bfloat16
I32 = jnp.int32
I16 = jnp.int16

D_MODEL = 1024
CHUNK = 128
GMLP_GROUPS = 8
GROUP_CH = D_MODEL // GMLP_GROUPS
MEM_LEN = 256
XA_HEADS = 4
XA_HEAD_DIM = D_MODEL // XA_HEADS
N_EXPERTS = 32
TOP_K = 4
D_FF = D_MODEL
W_IN_COLS = 9 * D_MODEL
SWIGLU_LIMIT = 7.0
SWIGLU_ALPHA = 1.702
LN_EPS = 1e-5
DEEPNORM_ALPHA = 2.0 ** 0.25

LANES = 128
SUBLANES = 8
VMEM_LIMIT_BYTES = 60000 * 1024

TM_MIX = 512
TM_SORT = 256
SEG_ALIGN = SUBLANES
L_SORT = 1280
assert L_SORT >= TOP_K * TM_SORT + N_EXPERTS * (SEG_ALIGN - 1) and L_SORT % 256 == 0
assert L_SORT < 2 ** 15
N_CHUNKS = L_SORT // SEG_ALIGN
D_PACK = D_MODEL // 2
XS_COLS = D_PACK + LANES
TR = 512
TR_SUB = 256
assert TR % TR_SUB == 0


def _pack_pair(lo, hi):
    lo_bits = lax.shift_right_logical(lax.bitcast_convert_type(lo, I32), 16)
    return lo_bits | lax.bitcast_convert_type(hi, I32)


def _unpack_bf16(words):
    lo = lax.bitcast_convert_type(lax.shift_left(words, 16), F32)
    hi = lax.bitcast_convert_type(words & np.int32(-65536), F32)
    return jnp.concatenate([lo.astype(BF16), hi.astype(BF16)], axis=1)


def _layer_norm(x, g, b):
    mu = jnp.mean(x, axis=-1, keepdims=True)
    xc = x - mu
    var = jnp.mean(xc * xc, axis=-1, keepdims=True)
    return xc * lax.rsqrt(var + LN_EPS) * g + b


def _gelu_tanh(x):
    c = np.float32(np.sqrt(2.0 / np.pi))
    ca = np.float32(np.sqrt(2.0 / np.pi) * 0.044715)
    half = 0.5 * x
    return half + half * jnp.tanh(x * (c + ca * (x * x)))


def _sigmoid(x):
    return 1.0 / (1.0 + jnp.exp(-x))


def _dot(a, b):
    return jnp.dot(a, b, preferred_element_type=F32)


def _const_spec(shape):
    n = len(shape)
    return pl.BlockSpec(shape, lambda *_: (0,) * n, pipeline_mode=pl.Buffered(1))


def _kv_kernel(mem_ref, g_ref, b_ref, wkv_ref, k_ref, v_ref):
    mem_n = _layer_norm(mem_ref[0], g_ref[...], b_ref[...])
    kv = _dot(mem_n.astype(BF16), wkv_ref[...])
    k_ref[0] = kv[:, :D_MODEL].astype(BF16)
    v_ref[0] = kv[:, D_MODEL:].astype(BF16)


def _kv_call(mem, g, b, wkv):
    bsz = mem.shape[0]
    return pl.pallas_call(
        _kv_kernel,
        grid=(bsz,),
        in_specs=[
            pl.BlockSpec((1, MEM_LEN, D_MODEL), lambda i: (i, 0, 0)),
            pl.BlockSpec((1, D_MODEL), lambda i: (0, 0)),
            pl.BlockSpec((1, D_MODEL), lambda i: (0, 0)),
            pl.BlockSpec((D_MODEL, 2 * D_MODEL), lambda i: (0, 0)),
        ],
        out_specs=[
            pl.BlockSpec((1, MEM_LEN, D_MODEL), lambda i: (i, 0, 0)),
            pl.BlockSpec((1, MEM_LEN, D_MODEL), lambda i: (i, 0, 0)),
        ],
        out_shape=[
            jax.ShapeDtypeStruct((bsz, MEM_LEN, D_MODEL), BF16),
            jax.ShapeDtypeStruct((bsz, MEM_LEN, D_MODEL), BF16),
        ],
        compiler_params=pltpu.CompilerParams(dimension_semantics=("arbitrary",)),
        name="kv",
    )(mem, g, b, wkv)


def _mix_kernel(x_ref, win_ref, bgate_ref, convw_ref, ws_ref, gbias_ref, glng_ref, glnb_ref,
                k_ref, v_ref, wc_ref, wg_ref, wx_ref, wo_ref, ln1g_ref, ln1b_ref,
                wrh_ref, wrl_ref, br_ref, x1_ref, logit_ref, halo_ref):
    tm = x_ref.shape[1]
    d = D_MODEL
    x = x_ref[0]
    xb = x.astype(BF16)

    def proj(i):
        return _dot(xb, win_ref[:, i * d:(i + 1) * d])

    def gate(i):
        return _sigmoid(proj(6 + i) + bgate_ref[:, i * d:(i + 1) * d])

    @pl.when(pl.program_id(1) == 0)
    def _():
        halo_ref[...] = jnp.zeros((SUBLANES, d), F32)

    u = proj(1) * proj(2)
    halo = halo_ref[...]
    halo_ref[...] = u[tm - SUBLANES:tm, :]
    head_row = lax.broadcasted_iota(I32, (SUBLANES, d), 0)

    def shifted(k):
        body = pltpu.roll(u, k, axis=0)
        head = jnp.where(head_row < k, pltpu.roll(halo, k, axis=0), body[0:SUBLANES])
        return jnp.concatenate([head, body[SUBLANES:]], axis=0)

    conv = convw_ref[0:1, :] * shifted(2) + convw_ref[1:2, :] * shifted(1) + convw_ref[2:3, :] * u
    gu_lin = proj(3)
    gv_lin = proj(4)
    y_conv = (proj(0) * conv).astype(BF16)
    q = proj(5).astype(BF16)
    scores = []
    for h in range(XA_HEADS):
        sl = slice(h * XA_HEAD_DIM, (h + 1) * XA_HEAD_DIM)
        scores.append(lax.dot_general(q[:, sl], k_ref[0, :, sl], (((1,), (1,)), ((), ())),
                                      preferred_element_type=F32))
    g_conv = gate(0)
    g_gmlp = gate(1)
    merged = g_conv * _dot(y_conv, wc_ref[...])

    gu = _gelu_tanh(gu_lin)
    gv = _gelu_tanh(gv_lin)
    vn = _layer_norm(gv, glng_ref[...], glnb_ref[...]).astype(BF16)
    n_chunks = tm // CHUNK
    row_i = lax.broadcasted_iota(I32, (CHUNK, CHUNK), 0)
    col_i = lax.broadcasted_iota(I32, (CHUNK, CHUNK), 1)
    causal = col_i <= row_i
    f_cols = []
    for g in range(GMLP_GROUPS):
        w_g = jnp.where(causal, ws_ref[g], 0.0).astype(BF16)
        rhs = jnp.concatenate(
            [vn[c * CHUNK:(c + 1) * CHUNK, g * GROUP_CH:(g + 1) * GROUP_CH] for c in range(n_chunks)],
            axis=1)
        fg = _dot(w_g, rhs)
        f_cols.append(jnp.concatenate(
            [fg[:, c * GROUP_CH:(c + 1) * GROUP_CH] for c in range(n_chunks)], axis=0))
    f = jnp.concatenate(f_cols, axis=1)
    gbias = jnp.concatenate([gbias_ref[...]] * n_chunks, axis=0)
    y_gmlp = gu * (f + gbias)
    merged = merged + g_gmlp * _dot(y_gmlp.astype(BF16), wg_ref[...])

    heads = []
    for h in range(XA_HEADS):
        sl = slice(h * XA_HEAD_DIM, (h + 1) * XA_HEAD_DIM)
        s = scores[h] * np.float32(XA_HEAD_DIM ** -0.5)
        p = jnp.exp(s - jnp.max(s, axis=-1, keepdims=True))
        p = p * (1.0 / jnp.sum(p, axis=-1, keepdims=True))
        heads.append(_dot(p.astype(BF16), v_ref[0, :, sl]))
    y_xa = jnp.concatenate(heads, axis=1)
    merged = merged + gate(2) * _dot(y_xa.astype(BF16), wx_ref[...])

    mix = _dot(merged.astype(BF16), wo_ref[...])
    x1 = _layer_norm(np.float32(DEEPNORM_ALPHA) * x + mix, ln1g_ref[...], ln1b_ref[...])
    x1_ref[0] = x1

    x1h = x1.astype(BF16)
    x1l = (x1 - x1h.astype(F32)).astype(BF16)
    logit_ref[0] = (_dot(x1h, wrh_ref[...]) + _dot(x1l, wrh_ref[...]) + _dot(x1h, wrl_ref[...])
                    + br_ref[...])


def _mix_call(x, win, bgate, convw, ws, gbias, glng, glnb, k, v, wc, wg, wx, wo, ln1g, ln1b,
              wrh, wrl, br):
    bsz, seq, d = x.shape
    tm = min(TM_MIX, seq)
    assert seq % tm == 0 and tm % CHUNK == 0
    tile = lambda b, s: (b, s, 0)
    per_batch = lambda b, s: (b, 0, 0)
    return pl.pallas_call(
        _mix_kernel,
        grid=(bsz, seq // tm),
        in_specs=[
            pl.BlockSpec((1, tm, d), tile),
            _const_spec((d, W_IN_COLS)),
            _const_spec((1, 3 * d)),
            _const_spec((3, d)),
            _const_spec((GMLP_GROUPS, CHUNK, CHUNK)),
            _const_spec((CHUNK, d)),
            _const_spec((1, d)),
            _const_spec((1, d)),
            pl.BlockSpec((1, MEM_LEN, d), per_batch),
            pl.BlockSpec((1, MEM_LEN, d), per_batch),
            _const_spec((d, d)),
            _const_spec((d, d)),
            _const_spec((d, d)),
            _const_spec((d, d)),
            _const_spec((1, d)),
            _const_spec((1, d)),
            _const_spec((d, LANES)),
            _const_spec((d, LANES)),
            _const_spec((1, LANES)),
        ],
        out_specs=[pl.BlockSpec((1, tm, d), tile), pl.BlockSpec((1, tm, LANES), tile)],
        out_shape=[jax.ShapeDtypeStruct((bsz, seq, d), F32),
                   jax.ShapeDtypeStruct((bsz, seq, LANES), F32)],
        scratch_shapes=[pltpu.VMEM((SUBLANES, d), F32)],
        compiler_params=pltpu.CompilerParams(
            dimension_semantics=("arbitrary", "arbitrary"), vmem_limit_bytes=VMEM_LIMIT_BYTES),
        name="mix",
    )(x, win, bgate, convw, ws, gbias, glng, glnb, k, v, wc, wg, wx, wo, ln1g, ln1b, wrh, wrl, br)


def _route_kernel(logit_ref, pos_ref, gate_ref, cnt_ref):
    tm = logit_ref.shape[0]
    work = logit_ref[...].T[0:N_EXPERTS, :]
    e_io = lax.broadcasted_iota(I32, (N_EXPERTS, tm), 0)
    neg = np.float32(-np.inf)
    vals, sels = [], []
    for _ in range(TOP_K):
        m = jnp.max(work, axis=0, keepdims=True)
        idx = jnp.min(jnp.where(work == m, e_io, LANES), axis=0, keepdims=True)
        sel = e_io == idx
        vals.append(m)
        sels.append(sel)
        work = jnp.where(sel, neg, work)
    ex = [jnp.exp(vk - vals[0]) for vk in vals]
    inv = 1.0 / (ex[0] + ex[1] + ex[2] + ex[3])
    gates = [e * inv for e in ex]

    oh = jnp.where(sels[0] | sels[1] | sels[2] | sels[3], 1.0, 0.0).astype(F32)
    cnt = jnp.sum(oh, axis=1, keepdims=True)
    t_r = lax.broadcasted_iota(I32, (tm, tm), 0)
    t_c = lax.broadcasted_iota(I32, (tm, tm), 1)
    before = jnp.where(t_r < t_c, 1.0, 0.0).astype(BF16)
    rank = _dot(oh.astype(BF16), before)
    units = jnp.floor((cnt + np.float32(SEG_ALIGN - 1)) * np.float32(1.0 / SEG_ALIGN))
    units_b = jnp.concatenate(
        [jnp.broadcast_to(units, (N_EXPERTS, LANES)), jnp.zeros((LANES - N_EXPERTS, LANES), F32)],
        axis=0).astype(BF16)
    e_r = lax.broadcasted_iota(I32, (N_EXPERTS, LANES), 0)
    e_c = lax.broadcasted_iota(I32, (N_EXPERTS, LANES), 1)
    below = jnp.where(e_c < e_r, 1.0, 0.0).astype(BF16)
    seg = _dot(below, units_b)[:, 0:1] * np.float32(SEG_ALIGN)
    slot = seg + rank

    pos_rows = [jnp.sum(jnp.where(s, slot, 0.0), axis=0, keepdims=True) for s in sels]
    zero_row = jnp.zeros((1, tm), F32)
    pos_ref[0] = jnp.concatenate(pos_rows + [zero_row] * (SUBLANES - TOP_K), axis=0).astype(I32)
    gate_ref[0] = jnp.concatenate(gates + [zero_row] * (SUBLANES - TOP_K), axis=0)
    cnt_ref[0] = jnp.broadcast_to(cnt, (N_EXPERTS, LANES)).astype(I32)


def _route_call(logits2d):
    n_tok = logits2d.shape[0]
    nt = n_tok // TM_SORT
    return pl.pallas_call(
        _route_kernel,
        grid=(nt,),
        in_specs=[pl.BlockSpec((TM_SORT, LANES), lambda i: (i, 0))],
        out_specs=[
            pl.BlockSpec((1, SUBLANES, TM_SORT), lambda i: (i, 0, 0)),
            pl.BlockSpec((1, SUBLANES, TM_SORT), lambda i: (i, 0, 0)),
            pl.BlockSpec((1, N_EXPERTS, LANES), lambda i: (i, 0, 0)),
        ],
        out_shape=[
            jax.ShapeDtypeStruct((nt, SUBLANES, TM_SORT), I32),
            jax.ShapeDtypeStruct((nt, SUBLANES, TM_SORT), F32),
            jax.ShapeDtypeStruct((nt, N_EXPERTS, LANES), I32),
        ],
        compiler_params=pltpu.CompilerParams(dimension_semantics=("arbitrary",)),
        name="route",
    )(logits2d)


def _chunk_copy(src_ref, src_row, dst_ref, dst_row, sem):
    return pltpu.make_async_copy(
        src_ref.at[pl.ds(src_row, SEG_ALIGN)], dst_ref.at[pl.ds(dst_row, SEG_ALIGN)], sem)


def _sort_kernel(slot_ref, x1_ref, pos_ref, gate_ref, xs_hbm, sbuf, sems):
    j = pl.program_id(0)
    nt = pl.num_programs(0)
    tm = x1_ref.shape[0]
    cur = lax.rem(j, 2)
    buf = sbuf.at[cur]
    j_io = lax.broadcasted_iota(I32, (L_SORT, tm), 0)
    hit = [j_io == pos_ref[0, k:k + 1, :] for k in range(TOP_K)]
    onehot = jnp.where(hit[0] | hit[1] | hit[2] | hit[3], 1.0, 0.0).astype(BF16)
    rows = _dot(onehot, x1_ref[...].astype(BF16))
    buf[:, 0:D_PACK] = _pack_pair(rows[:, 0:D_PACK], rows[:, D_PACK:D_MODEL])
    gsel = jnp.where(hit[0], gate_ref[0, 0:1, :], 0.0)
    for k in range(1, TOP_K):
        gsel = gsel + jnp.where(hit[k], gate_ref[0, k:k + 1, :], 0.0)
    gate_col = jnp.sum(gsel, axis=1, keepdims=True)
    buf[:, D_PACK:XS_COLS] = lax.bitcast_convert_type(
        jnp.broadcast_to(gate_col, (L_SORT, LANES)), I32)

    for i in range(N_CHUNKS):
        dst = pl.multiple_of(slot_ref[j * N_CHUNKS + i] * SEG_ALIGN, SEG_ALIGN)
        _chunk_copy(buf, i * SEG_ALIGN, xs_hbm, dst, sems.at[cur]).start()

    def drain(which):
        for _ in range(N_CHUNKS):
            _chunk_copy(sbuf.at[which], 0, xs_hbm, 0, sems.at[which]).wait()

    @pl.when(j > 0)
    def _():
        drain(1 - cur)

    @pl.when(j == nt - 1)
    def _():
        drain(cur)


def _sort_call(slots, x1, pos, gate):
    n_tok = x1.shape[0]
    nt = n_tok // TM_SORT
    grid_spec = pltpu.PrefetchScalarGridSpec(
        num_scalar_prefetch=1,
        grid=(nt,),
        in_specs=[
            pl.BlockSpec((TM_SORT, D_MODEL), lambda i, *_: (i, 0)),
            pl.BlockSpec((1, SUBLANES, TM_SORT), lambda i, *_: (i, 0, 0)),
            pl.BlockSpec((1, SUBLANES, TM_SORT), lambda i, *_: (i, 0, 0)),
        ],
        out_specs=pl.BlockSpec(memory_space=pl.ANY),
        scratch_shapes=[
            pltpu.VMEM((2, L_SORT, XS_COLS), I32),
            pltpu.SemaphoreType.DMA((2,)),
        ],
    )
    return pl.pallas_call(
        _sort_kernel,
        grid_spec=grid_spec,
        out_shape=jax.ShapeDtypeStruct((nt * L_SORT, XS_COLS), I32),
        compiler_params=pltpu.CompilerParams(
            dimension_semantics=("arbitrary",), vmem_limit_bytes=VMEM_LIMIT_BYTES),
        name="sort",
    )(slots, x1, pos, gate)


KIND_NOOP, KIND_FFN, KIND_ZERO = 0, 1, 2


def _expert_kernel(tile_ref, exp_ref, kind_ref, lo_ref, hi_ref, first_ref, fresh_ref, wslot_ref,
                   next_ref, xs_ref, wgu_hbm, bgu_ref, wd_hbm, bd_ref, out_ref,
                   wgu_b, wd_b, stage_gu, stage_d, wsems):
    w = pl.program_id(0)
    kind = kind_ref[w]

    @pl.when(kind == KIND_ZERO)
    def _():
        out_ref[...] = jnp.zeros(out_ref.shape, I32)

    def weight_copies(e, s):
        return (pltpu.make_async_copy(wgu_hbm.at[e], stage_gu.at[s], wsems.at[0, s]),
                pltpu.make_async_copy(wd_hbm.at[e], stage_d.at[s], wsems.at[1, s]))

    @pl.when((kind == KIND_FFN) & (fresh_ref[w] == 1))
    def _():
        e = exp_ref[w]
        s = wslot_ref[w]

        @pl.when(w == 0)
        def _():
            for cp in weight_copies(e, s):
                cp.start()

        for cp in weight_copies(e, s):
            cp.wait()
        wgu_b[...] = stage_gu[s].astype(BF16)
        wd_b[...] = stage_d[s].astype(BF16)

        @pl.when(next_ref[w] >= 0)
        def _():
            for cp in weight_copies(next_ref[w], 1 - s):
                cp.start()

    lo = lo_ref[w]
    hi = hi_ref[w]
    first = first_ref[w]
    is_ffn = kind == KIND_FFN
    whole = (lo == 0) & (hi == TR)

    def ffn(r0, nr):
        xs = _unpack_bf16(xs_ref[r0:r0 + nr, 0:D_PACK])
        gate = lax.bitcast_convert_type(xs_ref[r0:r0 + nr, D_PACK:D_PACK + 1], F32)
        gu = _dot(xs, wgu_b[...]) + bgu_ref[0]
        g_lin = jnp.minimum(gu[:, :D_FF], np.float32(SWIGLU_LIMIT))
        u_lin = jnp.clip(gu[:, D_FF:], np.float32(-SWIGLU_LIMIT), np.float32(SWIGLU_LIMIT))
        act = (u_lin + 1.0) * (g_lin * _sigmoid(np.float32(SWIGLU_ALPHA) * g_lin))
        eo = (_dot(act.astype(BF16), wd_b[...]) + bd_ref[0]) * gate
        eo = eo.astype(BF16).astype(F32)
        return _pack_pair(eo[:, 0:D_PACK], eo[:, D_PACK:D_MODEL])

    @pl.when(is_ffn & whole)
    def _():
        out_ref[...] = ffn(0, TR)

    for r0 in range(0, TR, TR_SUB):
        touched = (lo < r0 + TR_SUB) & (hi > r0)
        part = is_ffn & jnp.logical_not(whole)

        @pl.when(part & touched)
        def _(r0=r0):
            eo = ffn(r0, TR_SUB)
            rows = r0 + lax.broadcasted_iota(I32, (TR_SUB, 1), 0)
            mine = (rows >= lo) & (rows < hi)

            @pl.when(first == 1)
            def _():
                out_ref[r0:r0 + TR_SUB, :] = jnp.where(mine, eo, 0)

            @pl.when(first == 0)
            def _():
                out_ref[r0:r0 + TR_SUB, :] = jnp.where(mine, eo, out_ref[r0:r0 + TR_SUB, :])

        @pl.when(part & jnp.logical_not(touched) & (first == 1))
        def _(r0=r0):
            out_ref[r0:r0 + TR_SUB, :] = jnp.zeros((TR_SUB, D_PACK), I32)


def _expert_call(md, xs, wgu, bgu, wd, bd):
    pmax = xs.shape[0]
    n_items = md["tile"].shape[0]
    grid_spec = pltpu.PrefetchScalarGridSpec(
        num_scalar_prefetch=9,
        grid=(n_items,),
        in_specs=[
            pl.BlockSpec((TR, XS_COLS), lambda w, t, e, *_: (t[w], 0)),
            pl.BlockSpec(memory_space=pl.ANY),
            pl.BlockSpec((1, 1, 2 * D_FF), lambda w, t, e, *_: (e[w], 0, 0)),
            pl.BlockSpec(memory_space=pl.ANY),
            pl.BlockSpec((1, 1, D_MODEL), lambda w, t, e, *_: (e[w], 0, 0)),
        ],
        out_specs=pl.BlockSpec((TR, D_PACK), lambda w, t, e, *_: (t[w], 0)),
        scratch_shapes=[
            pltpu.VMEM((D_MODEL, 2 * D_FF), BF16),
            pltpu.VMEM((D_FF, D_MODEL), BF16),
            pltpu.VMEM((2, D_MODEL, 2 * D_FF), F32),
            pltpu.VMEM((2, D_FF, D_MODEL), F32),
            pltpu.SemaphoreType.DMA((2, 2)),
        ],
    )
    return pl.pallas_call(
        _expert_kernel,
        grid_spec=grid_spec,
        out_shape=jax.ShapeDtypeStruct((pmax, D_PACK), I32),
        compiler_params=pltpu.CompilerParams(
            dimension_semantics=("arbitrary",), vmem_limit_bytes=VMEM_LIMIT_BYTES),
        name="experts",
    )(md["tile"], md["exp"], md["kind"], md["lo"], md["hi"], md["first"], md["fresh"],
      md["wslot"], md["next"], xs, wgu, bgu, wd, bd)


def _combine_kernel(slot_ref, x1_ref, post_ref, g_ref, b_ref, eo_hbm, out_ref, ebuf, sems):
    j = pl.program_id(0)
    nt = pl.num_programs(0)
    tm = x1_ref.shape[0]
    cur = lax.rem(j, 2)

    def fetch(tile, which):
        for i in range(N_CHUNKS):
            src = pl.multiple_of(slot_ref[tile * N_CHUNKS + i] * SEG_ALIGN, SEG_ALIGN)
            _chunk_copy(eo_hbm, src, ebuf.at[which], i * SEG_ALIGN, sems.at[which]).start()

    @pl.when(j == 0)
    def _():
        fetch(0, 0)

    @pl.when(j + 1 < nt)
    def _():
        fetch(j + 1, 1 - cur)

    for _ in range(N_CHUNKS):
        _chunk_copy(eo_hbm, 0, ebuf.at[cur], 0, sems.at[cur]).wait()

    l_io = lax.broadcasted_iota(I32, (tm, L_SORT), 1).astype(I16)
    post = post_ref[0].astype(I16)
    hit = l_io == post[:, 0:1]
    for k in range(1, TOP_K):
        hit = hit | (l_io == post[:, k:k + 1])
    onehot = jnp.where(hit, jnp.ones((), BF16), jnp.zeros((), BF16))
    y = _dot(onehot, _unpack_bf16(ebuf[cur]))
    out_ref[...] = _layer_norm(np.float32(DEEPNORM_ALPHA) * x1_ref[...] + y, g_ref[...], b_ref[...])


def _combine_call(slots, x1, post, g, b, eo):
    n_tok = x1.shape[0]
    nt = n_tok // TM_SORT
    grid_spec = pltpu.PrefetchScalarGridSpec(
        num_scalar_prefetch=1,
        grid=(nt,),
        in_specs=[
            pl.BlockSpec((TM_SORT, D_MODEL), lambda i, *_: (i, 0)),
            pl.BlockSpec((1, TM_SORT, SUBLANES), lambda i, *_: (i, 0, 0)),
            pl.BlockSpec((1, D_MODEL), lambda i, *_: (0, 0)),
            pl.BlockSpec((1, D_MODEL), lambda i, *_: (0, 0)),
            pl.BlockSpec(memory_space=pl.ANY),
        ],
        out_specs=pl.BlockSpec((TM_SORT, D_MODEL), lambda i, *_: (i, 0)),
        scratch_shapes=[pltpu.VMEM((2, L_SORT, D_PACK), I32), pltpu.SemaphoreType.DMA((2,))],
    )
    return pl.pallas_call(
        _combine_kernel,
        grid_spec=grid_spec,
        out_shape=jax.ShapeDtypeStruct((n_tok, D_MODEL), F32),
        compiler_params=pltpu.CompilerParams(
            dimension_semantics=("arbitrary",), vmem_limit_bytes=VMEM_LIMIT_BYTES),
        name="combine",
    )(slots, x1, post, g, b, eo)


def _excl_cumsum(a, axis):
    n = a.shape[axis]
    lower = (jnp.arange(n)[:, None] > jnp.arange(n)[None, :]).astype(F32)
    af = jnp.moveaxis(a.astype(F32), axis, 0).reshape(n, -1)
    out = jnp.dot(lower, af, precision=lax.Precision.HIGHEST)
    out = out.reshape((n,) + tuple(np.delete(np.array(a.shape), axis)))
    return jnp.moveaxis(out, 0, axis).astype(I32)


def _routing_metadata(cnt):
    nt = cnt.shape[0]
    c8 = (cnt + (SEG_ALIGN - 1)) // SEG_ALIGN * SEG_ALIGN
    seg = _excl_cumsum(c8, 1)
    tot8 = jnp.sum(c8, axis=1)
    n8 = jnp.sum(c8, axis=0)
    base = _excl_cumsum(n8, 0)
    goff = base[None, :] + _excl_cumsum(c8, 0)
    p_used = jnp.sum(n8)

    r = (jnp.arange(N_CHUNKS, dtype=I32) * SEG_ALIGN)[None, :, None]
    in_seg = (r >= seg[:, None, :]) & (r < (seg + c8)[:, None, :])
    real_slot = jnp.sum(jnp.where(in_seg, goff[:, None, :] + r - seg[:, None, :], 0), axis=2) // SEG_ALIGN
    used = tot8 // SEG_ALIGN
    n_empty = N_CHUNKS - used
    ci = jnp.arange(N_CHUNKS, dtype=I32)[None, :]
    empty_slot = p_used // SEG_ALIGN + _excl_cumsum(n_empty, 0)[:, None] + (ci - used[:, None])
    slots = jnp.where(ci < used[:, None], real_slot, empty_slot).reshape(-1).astype(I32)

    assert (nt * L_SORT) % TR == 0
    n_tiles = nt * L_SORT // TR
    n_items = n_tiles + N_EXPERTS
    e_lo, e_hi = base, base + n8
    t_first = e_lo // TR
    t_end = (e_hi + TR - 1) // TR
    w_e = jnp.where(n8 > 0, t_end - t_first, 0)
    w_start = _excl_cumsum(w_e, 0)
    w_cum = w_start + w_e
    w_total = jnp.sum(w_e)
    tiles_used = (p_used + TR - 1) // TR

    w = jnp.arange(n_items, dtype=I32)
    e_ids = jnp.arange(N_EXPERTS, dtype=I32)
    e_w = jnp.minimum(jnp.sum((w[:, None] >= w_cum[None, :]).astype(I32), axis=1), N_EXPERTS - 1)
    is_ffn = w < w_total
    last_e = jnp.max(jnp.where(n8 > 0, e_ids, 0))
    exp = jnp.where(is_ffn, e_w, last_e).astype(I32)
    pick = exp[:, None] == e_ids[None, :]
    look = lambda table: jnp.sum(jnp.where(pick, table[None, :], 0), axis=1)
    ffn_tile = look(t_first) + (w - look(w_start))
    zero_tile = tiles_used + (w - w_total)
    is_zero = (~is_ffn) & (zero_tile < n_tiles)
    tile = jnp.where(is_ffn, ffn_tile, jnp.where(is_zero, zero_tile, n_tiles - 1)).astype(I32)
    kind = jnp.where(is_ffn, KIND_FFN, jnp.where(is_zero, KIND_ZERO, KIND_NOOP)).astype(I32)
    lo = jnp.clip(look(e_lo) - tile * TR, 0, TR).astype(I32)
    hi = jnp.clip(look(e_hi) - tile * TR, 0, TR).astype(I32)
    prev_tile = jnp.concatenate([jnp.full((1,), -1, I32), tile[:-1]])
    first = (tile != prev_tile).astype(I32)
    prev_exp = jnp.concatenate([jnp.full((1,), -1, I32), exp[:-1]])
    fresh = (exp != prev_exp).astype(I32)
    used_e = n8 > 0
    later = (e_ids[None, :] > e_ids[:, None]) & used_e[None, :]
    order_e = jnp.sum(((e_ids[None, :] < e_ids[:, None]) & used_e[None, :]).astype(I32), axis=1)
    next_e = jnp.min(jnp.where(later, e_ids[None, :], N_EXPERTS), axis=1)
    next_e = jnp.where(next_e == N_EXPERTS, -1, next_e)
    wslot = (look(order_e) % 2).astype(I32)
    nxt = look(next_e).astype(I32)
    return dict(slots=slots, tile=tile, exp=exp, kind=kind, lo=lo, hi=hi, first=first, fresh=fresh,
                wslot=wslot, next=nxt)


def kernel(x, mem, w_in, b_gate, conv_w, gmlp_ws, gmlp_b, gmlp_ln_g, gmlp_ln_b, mem_ln_g,
           mem_ln_b, w_kv, w_conv_proj, w_gmlp_proj, w_xa_proj, w_out, ln1_g, ln1_b, w_router,
           b_router, w_gate_up, b_gate_up, w_down, b_down, ln2_g, ln2_b):
    bsz, seq, d = x.shape
    n_tok = bsz * seq
    assert d == D_MODEL and n_tok % TM_SORT == 0
    depth = w_in.shape[0]
    for l in range(depth):
        row = lambda a: a[l].reshape(1, -1)
        k, v = _kv_call(mem, row(mem_ln_g), row(mem_ln_b), w_kv[l].astype(BF16))
        gbias = jnp.repeat(gmlp_b[l].T, GROUP_CH, axis=1)
        wr = jnp.pad(w_router[l], ((0, 0), (0, LANES - N_EXPERTS)))
        wrh = wr.astype(BF16)
        wrl = (wr - wrh.astype(F32)).astype(BF16)
        br = jnp.pad(b_router[l], (0, LANES - N_EXPERTS)).reshape(1, LANES)
        x1, logits = _mix_call(
            x, w_in[l].astype(BF16), row(b_gate), conv_w[l], gmlp_ws[l], gbias,
            row(gmlp_ln_g), row(gmlp_ln_b), k, v, w_conv_proj[l].astype(BF16),
            w_gmlp_proj[l].astype(BF16), w_xa_proj[l].astype(BF16), w_out[l].astype(BF16),
            row(ln1_g), row(ln1_b), wrh, wrl, br)
        x1 = x1.reshape(n_tok, d)
        pos, gate, cnt = _route_call(logits.reshape(n_tok, LANES))
        md = _routing_metadata(cnt[:, :, 0])
        xs = _sort_call(md["slots"], x1, pos, gate)
        eo = _expert_call(md, xs, w_gate_up[l], b_gate_up[l][:, None, :],
                          w_down[l], b_down[l][:, None, :])
        post = jnp.transpose(pos, (0, 2, 1))
        x = _combine_call(md["slots"], x1, post, row(ln2_g), row(ln2_b), eo).reshape(bsz, seq, d)
    return x
```

```python
import jax
import jax.numpy as jnp
import numpy as np
from jax import lax
from jax.experimental import pallas as pl
from jax.experimental.pallas import tpu as pltpu

F32 = jnp.float32
BF16 = jnp.bfloat16
I32 = jnp.int32
I16 = jnp.int16

D_MODEL = 1024
CHUNK = 128
GMLP_GROUPS = 8
GROUP_CH = D_MODEL // GMLP_GROUPS
MEM_LEN = 256
XA_HEADS = 4
XA_HEAD_DIM = D_MODEL // XA_HEADS
N_EXPERTS = 32
TOP_K = 4
D_FF = D_MODEL
W_IN_COLS = 9 * D_MODEL
SWIGLU_LIMIT = 7.0
SWIGLU_ALPHA = 1.702
LN_EPS = 1e-5
DEEPNORM_ALPHA = 2.0 ** 0.25

LANES = 128
SUBLANES = 8
VMEM_LIMIT_BYTES = 60000 * 1024
DMA_PRIORITIES = 2

TM_MIX = 512
TM_SORT = 512
SEG_ALIGN = SUBLANES
MXU_DIM = 256
L_SORT = -(-(TOP_K * TM_SORT + N_EXPERTS * (SEG_ALIGN - 1)) // MXU_DIM) * MXU_DIM
assert L_SORT >= TOP_K * TM_SORT + N_EXPERTS * (SEG_ALIGN - 1) and L_SORT % MXU_DIM == 0
assert L_SORT < 2 ** 15
N_CHUNKS = L_SORT // SEG_ALIGN
D_PACK = D_MODEL // 2
XS_COLS = D_PACK + LANES
GTAB_EID = 3 * TOP_K
TR = 512
TR_SUB = 256
assert TR % TR_SUB == 0


def _pack_pair(lo, hi):
    lo_bits = lax.shift_right_logical(lax.bitcast_convert_type(lo, I32), 16)
    return lo_bits | lax.bitcast_convert_type(hi, I32)


def _unpack_bf16(words):
    lo = lax.bitcast_convert_type(lax.shift_left(words, 16), F32)
    hi = lax.bitcast_convert_type(words & np.int32(-65536), F32)
    return jnp.concatenate([lo.astype(BF16), hi.astype(BF16)], axis=1)


def _layer_norm(x, g, b):
    mu = jnp.mean(x, axis=-1, keepdims=True)
    xc = x - mu
    var = jnp.mean(xc * xc, axis=-1, keepdims=True)
    return xc * lax.rsqrt(var + LN_EPS) * g + b


def _gelu_tanh(x):
    c = np.float32(np.sqrt(2.0 / np.pi))
    ca = np.float32(np.sqrt(2.0 / np.pi) * 0.044715)
    half = 0.5 * x
    return half + half * jnp.tanh(x * (c + ca * (x * x)))


def _sigmoid(x):
    return 1.0 / (1.0 + jnp.exp(-x))


def _dot(a, b):
    return jnp.dot(a, b, preferred_element_type=F32)


def _const_spec(shape):
    n = len(shape)
    return pl.BlockSpec(shape, lambda *_: (0,) * n, pipeline_mode=pl.Buffered(1))


def _kv_kernel(mem_ref, g_ref, b_ref, wkv_ref, k_ref, v_ref):
    mem_n = _layer_norm(mem_ref[0], g_ref[...], b_ref[...])
    kv = _dot(mem_n.astype(BF16), wkv_ref[...])
    k_ref[0] = kv[:, :D_MODEL].astype(BF16)
    v_ref[0] = kv[:, D_MODEL:].astype(BF16)


def _kv_call(mem, g, b, wkv):
    bsz = mem.shape[0]
    return pl.pallas_call(
        _kv_kernel,
        grid=(bsz,),
        in_specs=[
            pl.BlockSpec((1, MEM_LEN, D_MODEL), lambda i: (i, 0, 0)),
            pl.BlockSpec((1, D_MODEL), lambda i: (0, 0)),
            pl.BlockSpec((1, D_MODEL), lambda i: (0, 0)),
            pl.BlockSpec((D_MODEL, 2 * D_MODEL), lambda i: (0, 0)),
        ],
        out_specs=[
            pl.BlockSpec((1, MEM_LEN, D_MODEL), lambda i: (i, 0, 0)),
            pl.BlockSpec((1, MEM_LEN, D_MODEL), lambda i: (i, 0, 0)),
        ],
        out_shape=[
            jax.ShapeDtypeStruct((bsz, MEM_LEN, D_MODEL), BF16),
            jax.ShapeDtypeStruct((bsz, MEM_LEN, D_MODEL), BF16),
        ],
        compiler_params=pltpu.CompilerParams(dimension_semantics=("arbitrary",)),
        name="kv",
    )(mem, g, b, wkv)


def _mix_kernel(x_ref, win_ref, bgate_ref, convw_ref, ws_ref, gbias_ref, glng_ref, glnb_ref,
                k_ref, v_ref, wc_ref, wg_ref, wx_ref, wo_ref, ln1g_ref, ln1b_ref,
                wrh_ref, wrl_ref, br_ref, x1_ref, logit_ref, halo_ref):
    tm = x_ref.shape[1]
    d = D_MODEL
    x = x_ref[0]
    xb = x.astype(BF16)

    def proj(i):
        return _dot(xb, win_ref[:, i * d:(i + 1) * d])

    def gate(i):
        return _sigmoid(proj(6 + i) + bgate_ref[:, i * d:(i + 1) * d])

    @pl.when(pl.program_id(1) == 0)
    def _():
        halo_ref[...] = jnp.zeros((SUBLANES, d), F32)

    u = proj(1) * proj(2)
    halo = halo_ref[...]
    halo_ref[...] = u[tm - SUBLANES:tm, :]
    head_row = lax.broadcasted_iota(I32, (SUBLANES, d), 0)

    def shifted(k):
        body = pltpu.roll(u, k, axis=0)
        head = jnp.where(head_row < k, pltpu.roll(halo, k, axis=0), body[0:SUBLANES])
        return jnp.concatenate([head, body[SUBLANES:]], axis=0)

    conv = convw_ref[0:1, :] * shifted(2) + convw_ref[1:2, :] * shifted(1) + convw_ref[2:3, :] * u
    gu_lin = proj(3)
    gv_lin = proj(4)
    y_conv = (proj(0) * conv).astype(BF16)
    q = proj(5).astype(BF16)
    scores = []
    for h in range(XA_HEADS):
        sl = slice(h * XA_HEAD_DIM, (h + 1) * XA_HEAD_DIM)
        scores.append(lax.dot_general(q[:, sl], k_ref[0, :, sl], (((1,), (1,)), ((), ())),
                                      preferred_element_type=F32))
    g_conv = gate(0)
    g_gmlp = gate(1)
    merged = g_conv * _dot(y_conv, wc_ref[...])

    gu = _gelu_tanh(gu_lin)
    gv = _gelu_tanh(gv_lin)
    vn = _layer_norm(gv, glng_ref[...], glnb_ref[...]).astype(BF16)
    n_chunks = tm // CHUNK
    row_i = lax.broadcasted_iota(I32, (CHUNK, CHUNK), 0)
    col_i = lax.broadcasted_iota(I32, (CHUNK, CHUNK), 1)
    causal = col_i <= row_i
    f_cols = []
    for g in range(GMLP_GROUPS):
        w_g = jnp.where(causal, ws_ref[g], 0.0).astype(BF16)
        rhs = jnp.concatenate(
            [vn[c * CHUNK:(c + 1) * CHUNK, g * GROUP_CH:(g + 1) * GROUP_CH] for c in range(n_chunks)],
            axis=1)
        fg = _dot(w_g, rhs)
        f_cols.append(jnp.concatenate(
            [fg[:, c * GROUP_CH:(c + 1) * GROUP_CH] for c in range(n_chunks)], axis=0))
    f = jnp.concatenate(f_cols, axis=1)
    gbias = jnp.concatenate([gbias_ref[...]] * n_chunks, axis=0)
    y_gmlp = gu * (f + gbias)
    merged = merged + g_gmlp * _dot(y_gmlp.astype(BF16), wg_ref[...])

    heads = []
    for h in range(XA_HEADS):
        sl = slice(h * XA_HEAD_DIM, (h + 1) * XA_HEAD_DIM)
        s = scores[h] * np.float32(XA_HEAD_DIM ** -0.5)
        p = jnp.exp(s - jnp.max(s, axis=-1, keepdims=True))
        p = p * (1.0 / jnp.sum(p, axis=-1, keepdims=True))
        heads.append(_dot(p.astype(BF16), v_ref[0, :, sl]))
    y_xa = jnp.concatenate(heads, axis=1)
    merged = merged + gate(2) * _dot(y_xa.astype(BF16), wx_ref[...])

    mix = _dot(merged.astype(BF16), wo_ref[...])
    x1 = _layer_norm(np.float32(DEEPNORM_ALPHA) * x + mix, ln1g_ref[...], ln1b_ref[...])
    x1_ref[0] = x1

    x1h = x1.astype(BF16)
    x1l = (x1 - x1h.astype(F32)).astype(BF16)
    logit_ref[0] = (_dot(x1h, wrh_ref[...]) + _dot(x1l, wrh_ref[...]) + _dot(x1h, wrl_ref[...])
                    + br_ref[...])


def _mix_call(x, win, bgate, convw, ws, gbias, glng, glnb, k, v, wc, wg, wx, wo, ln1g, ln1b,
              wrh, wrl, br):
    bsz, seq, d = x.shape
    tm = min(TM_MIX, seq)
    assert seq % tm == 0 and tm % CHUNK == 0
    tile = lambda b, s: (b, s, 0)
    per_batch = lambda b, s: (b, 0, 0)
    return pl.pallas_call(
        _mix_kernel,
        grid=(bsz, seq // tm),
        in_specs=[
            pl.BlockSpec((1, tm, d), tile),
            _const_spec((d, W_IN_COLS)),
            _const_spec((1, 3 * d)),
            _const_spec((3, d)),
            _const_spec((GMLP_GROUPS, CHUNK, CHUNK)),
            _const_spec((CHUNK, d)),
            _const_spec((1, d)),
            _const_spec((1, d)),
            pl.BlockSpec((1, MEM_LEN, d), per_batch),
            pl.BlockSpec((1, MEM_LEN, d), per_batch),
            _const_spec((d, d)),
            _const_spec((d, d)),
            _const_spec((d, d)),
            _const_spec((d, d)),
            _const_spec((1, d)),
            _const_spec((1, d)),
            _const_spec((d, LANES)),
            _const_spec((d, LANES)),
            _const_spec((1, LANES)),
        ],
        out_specs=[pl.BlockSpec((1, tm, d), tile), pl.BlockSpec((1, tm, LANES), tile)],
        out_shape=[jax.ShapeDtypeStruct((bsz, seq, d), F32),
                   jax.ShapeDtypeStruct((bsz, seq, LANES), F32)],
        scratch_shapes=[pltpu.VMEM((SUBLANES, d), F32)],
        compiler_params=pltpu.CompilerParams(
            dimension_semantics=("arbitrary", "arbitrary"), vmem_limit_bytes=VMEM_LIMIT_BYTES),
        name="mix",
    )(x, win, bgate, convw, ws, gbias, glng, glnb, k, v, wc, wg, wx, wo, ln1g, ln1b, wrh, wrl, br)


def _split3(v):
    hi = v.astype(BF16).astype(F32)
    mid = (v - hi).astype(BF16).astype(F32)
    return hi, mid, (v - hi - mid).astype(BF16).astype(F32)


def _route_kernel(logit_ref, pos_ref, gtab_ref, cnt_ref):
    tm = logit_ref.shape[0]
    work = logit_ref[...].T[0:N_EXPERTS, :]
    e_io = lax.broadcasted_iota(I32, (N_EXPERTS, tm), 0)
    neg = np.float32(-np.inf)
    vals, sels, eids = [], [], []
    for _ in range(TOP_K):
        m = jnp.max(work, axis=0, keepdims=True)
        idx = jnp.min(jnp.where(work == m, e_io, LANES), axis=0, keepdims=True)
        sel = e_io == idx
        vals.append(m)
        sels.append(sel)
        eids.append(idx.astype(F32))
        work = jnp.where(sel, neg, work)
    ex = [jnp.exp(vk - vals[0]) for vk in vals]
    inv = 1.0 / (ex[0] + ex[1] + ex[2] + ex[3])
    gates = [e * inv for e in ex]

    oh = jnp.where(sels[0] | sels[1] | sels[2] | sels[3], 1.0, 0.0).astype(F32)
    cnt = jnp.sum(oh, axis=1, keepdims=True)
    t_r = lax.broadcasted_iota(I32, (tm, tm), 0)
    t_c = lax.broadcasted_iota(I32, (tm, tm), 1)
    before = jnp.where(t_r < t_c, 1.0, 0.0).astype(BF16)
    rank = _dot(oh.astype(BF16), before)
    units = jnp.floor((cnt + np.float32(SEG_ALIGN - 1)) * np.float32(1.0 / SEG_ALIGN))
    units_b = jnp.concatenate(
        [jnp.broadcast_to(units, (N_EXPERTS, LANES)), jnp.zeros((LANES - N_EXPERTS, LANES), F32)],
        axis=0).astype(BF16)
    e_r = lax.broadcasted_iota(I32, (N_EXPERTS, LANES), 0)
    e_c = lax.broadcasted_iota(I32, (N_EXPERTS, LANES), 1)
    below = jnp.where(e_c < e_r, 1.0, 0.0).astype(BF16)
    seg = _dot(below, units_b)[:, 0:1] * np.float32(SEG_ALIGN)
    slot = seg + rank

    pos_rows = [jnp.sum(jnp.where(s, slot, 0.0), axis=0, keepdims=True) for s in sels]
    zero_row = jnp.zeros((1, tm), F32)
    pos_ref[0] = jnp.concatenate(pos_rows + [zero_row] * (SUBLANES - TOP_K), axis=0).astype(I32)
    parts = [p for g in gates for p in _split3(g)]
    table = jnp.concatenate(parts + eids + [jnp.zeros((LANES - len(parts) - TOP_K, tm), F32)], axis=0)
    gtab_ref[...] = table.T.astype(BF16)
    cnt_ref[0] = jnp.broadcast_to(cnt, (N_EXPERTS, LANES)).astype(I32)


def _route_call(logits2d):
    n_tok = logits2d.shape[0]
    nt = n_tok // TM_SORT
    return pl.pallas_call(
        _route_kernel,
        grid=(nt,),
        in_specs=[pl.BlockSpec((TM_SORT, LANES), lambda i: (i, 0))],
        out_specs=[
            pl.BlockSpec((1, SUBLANES, TM_SORT), lambda i: (i, 0, 0)),
            pl.BlockSpec((TM_SORT, LANES), lambda i: (i, 0)),
            pl.BlockSpec((1, N_EXPERTS, LANES), lambda i: (i, 0, 0)),
        ],
        out_shape=[
            jax.ShapeDtypeStruct((nt, SUBLANES, TM_SORT), I32),
            jax.ShapeDtypeStruct((n_tok, LANES), BF16),
            jax.ShapeDtypeStruct((nt, N_EXPERTS, LANES), I32),
        ],
        compiler_params=pltpu.CompilerParams(dimension_semantics=("arbitrary",)),
        name="route",
    )(logits2d)


def _chunk_rows(buf, i):
    return buf.at[pl.ds(i * SEG_ALIGN, SEG_ALIGN)]


def _sort_kernel(slot_ref, x1_ref, pos_ref, gtab_ref, xs_hbm, sbuf, sems):
    j = pl.program_id(0)
    nt = pl.num_programs(0)
    tm = x1_ref.shape[0]
    cur = lax.rem(j, 2)
    buf = sbuf.at[cur]
    j_io = lax.broadcasted_iota(I32, (L_SORT, tm), 0).astype(I16)
    pos = pos_ref[0].astype(I16)
    hit = j_io == pos[0:1, :]
    for k in range(1, TOP_K):
        hit = hit | (j_io == pos[k:k + 1, :])
    onehot = jnp.where(hit, jnp.ones((), BF16), jnp.zeros((), BF16))
    rows = _dot(onehot, jnp.concatenate([x1_ref[...].astype(BF16), gtab_ref[...]], axis=1))
    buf[:, 0:D_PACK] = _pack_pair(rows[:, 0:D_PACK], rows[:, D_PACK:D_MODEL])
    buf[:, D_PACK:XS_COLS] = lax.bitcast_convert_type(rows[:, D_MODEL:D_MODEL + LANES], I32)

    for i in range(N_CHUNKS):
        pltpu.make_async_copy(
            _chunk_rows(buf, i), xs_hbm.at[slot_ref[j * N_CHUNKS + i]], sems.at[cur],
        ).start(priority=i % DMA_PRIORITIES)

    def drain(which):
        for _ in range(N_CHUNKS):
            pltpu.make_async_copy(
                _chunk_rows(sbuf.at[which], 0), xs_hbm.at[0], sems.at[which]).wait()

    @pl.when(j > 0)
    def _():
        drain(1 - cur)

    @pl.when(j == nt - 1)
    def _():
        drain(cur)


def _sort_call(slots, x1, pos, gtab):
    n_tok = x1.shape[0]
    nt = n_tok // TM_SORT
    grid_spec = pltpu.PrefetchScalarGridSpec(
        num_scalar_prefetch=1,
        grid=(nt,),
        in_specs=[
            pl.BlockSpec((TM_SORT, D_MODEL), lambda i, *_: (i, 0)),
            pl.BlockSpec((1, SUBLANES, TM_SORT), lambda i, *_: (i, 0, 0)),
            pl.BlockSpec((TM_SORT, LANES), lambda i, *_: (i, 0)),
        ],
        out_specs=pl.BlockSpec(memory_space=pl.ANY),
        scratch_shapes=[
            pltpu.VMEM((2, L_SORT, XS_COLS), I32),
            pltpu.SemaphoreType.DMA((2,)),
        ],
    )
    return pl.pallas_call(
        _sort_kernel,
        grid_spec=grid_spec,
        out_shape=jax.ShapeDtypeStruct((nt * N_CHUNKS, SEG_ALIGN, XS_COLS), I32),
        compiler_params=pltpu.CompilerParams(
            dimension_semantics=("arbitrary",), vmem_limit_bytes=VMEM_LIMIT_BYTES),
        name="sort",
    )(slots, x1, pos, gtab)


KIND_NOOP, KIND_FFN, KIND_ZERO = 0, 1, 2


def _expert_kernel(tile_ref, exp_ref, kind_ref, lo_ref, hi_ref, first_ref, fresh_ref, wslot_ref,
                   next_ref, xs_ref, wgu_hbm, bgu_ref, wd_hbm, bd_ref, out_ref,
                   wgu_b, wd_b, stage_gu, stage_d, wsems):
    w = pl.program_id(0)
    kind = kind_ref[w]

    @pl.when(kind == KIND_ZERO)
    def _():
        out_ref[...] = jnp.zeros(out_ref.shape, I32)

    def weight_copies(e, s):
        return (pltpu.make_async_copy(wgu_hbm.at[e], stage_gu.at[s], wsems.at[0, s]),
                pltpu.make_async_copy(wd_hbm.at[e], stage_d.at[s], wsems.at[1, s]))

    @pl.when((kind == KIND_FFN) & (fresh_ref[w] == 1))
    def _():
        e = exp_ref[w]
        s = wslot_ref[w]

        @pl.when(w == 0)
        def _():
            for cp in weight_copies(e, s):
                cp.start()

        for cp in weight_copies(e, s):
            cp.wait()
        wgu_b[...] = stage_gu[s].astype(BF16)
        wd_b[...] = stage_d[s].astype(BF16)

        @pl.when(next_ref[w] >= 0)
        def _():
            for cp in weight_copies(next_ref[w], 1 - s):
                cp.start()

    lo = lo_ref[w]
    hi = hi_ref[w]
    first = first_ref[w]
    is_ffn = kind == KIND_FFN
    whole = (lo == 0) & (hi == TR)

    def ffn(r0, nr):
        xs = _unpack_bf16(xs_ref[r0:r0 + nr, 0:D_PACK])
        gtab = lax.bitcast_convert_type(xs_ref[r0:r0 + nr, D_PACK:XS_COLS], F32)
        this_e = exp_ref[w].astype(F32)
        gate = jnp.zeros((nr, 1), F32)
        for k in range(TOP_K):
            g_k = gtab[:, 3 * k:3 * k + 1] + gtab[:, 3 * k + 1:3 * k + 2] + gtab[:, 3 * k + 2:3 * k + 3]
            gate = gate + jnp.where(gtab[:, GTAB_EID + k:GTAB_EID + k + 1] == this_e, g_k, 0.0)
        gu = _dot(xs, wgu_b[...]) + bgu_ref[0]
        g_lin = jnp.minimum(gu[:, :D_FF], np.float32(SWIGLU_LIMIT))
        u_lin = jnp.clip(gu[:, D_FF:], np.float32(-SWIGLU_LIMIT), np.float32(SWIGLU_LIMIT))
        act = (u_lin + 1.0) * (g_lin * _sigmoid(np.float32(SWIGLU_ALPHA) * g_lin))
        eo = (_dot(act.astype(BF16), wd_b[...]) + bd_ref[0]) * gate
        eo = eo.astype(BF16).astype(F32)
        return _pack_pair(eo[:, 0:D_PACK], eo[:, D_PACK:D_MODEL])

    @pl.when(is_ffn & whole)
    def _():
        out_ref[...] = ffn(0, TR)

    for r0 in range(0, TR, TR_SUB):
        touched = (lo < r0 + TR_SUB) & (hi > r0)
        part = is_ffn & jnp.logical_not(whole)

        @pl.when(part & touched)
        def _(r0=r0):
            eo = ffn(r0, TR_SUB)
            rows = r0 + lax.broadcasted_iota(I32, (TR_SUB, 1), 0)
            mine = (rows >= lo) & (rows < hi)

            @pl.when(first == 1)
            def _():
                out_ref[r0:r0 + TR_SUB, :] = jnp.where(mine, eo, 0)

            @pl.when(first == 0)
            def _():
                out_ref[r0:r0 + TR_SUB, :] = jnp.where(mine, eo, out_ref[r0:r0 + TR_SUB, :])

        @pl.when(part & jnp.logical_not(touched) & (first == 1))
        def _(r0=r0):
            out_ref[r0:r0 + TR_SUB, :] = jnp.zeros((TR_SUB, D_PACK), I32)


def _expert_call(md, xs, wgu, bgu, wd, bd):
    pmax = xs.shape[0]
    n_items = md["tile"].shape[0]
    grid_spec = pltpu.PrefetchScalarGridSpec(
        num_scalar_prefetch=9,
        grid=(n_items,),
        in_specs=[
            pl.BlockSpec((TR, XS_COLS), lambda w, t, e, *_: (t[w], 0)),
            pl.BlockSpec(memory_space=pl.ANY),
            pl.BlockSpec((1, 1, 2 * D_FF), lambda w, t, e, *_: (e[w], 0, 0)),
            pl.BlockSpec(memory_space=pl.ANY),
            pl.BlockSpec((1, 1, D_MODEL), lambda w, t, e, *_: (e[w], 0, 0)),
        ],
        out_specs=pl.BlockSpec((TR, D_PACK), lambda w, t, e, *_: (t[w], 0)),
        scratch_shapes=[
            pltpu.VMEM((D_MODEL, 2 * D_FF), BF16),
            pltpu.VMEM((D_FF, D_MODEL), BF16),
            pltpu.VMEM((2, D_MODEL, 2 * D_FF), F32),
            pltpu.VMEM((2, D_FF, D_MODEL), F32),
            pltpu.SemaphoreType.DMA((2, 2)),
        ],
    )
    return pl.pallas_call(
        _expert_kernel,
        grid_spec=grid_spec,
        out_shape=jax.ShapeDtypeStruct((pmax, D_PACK), I32),
        compiler_params=pltpu.CompilerParams(
            dimension_semantics=("arbitrary",), vmem_limit_bytes=VMEM_LIMIT_BYTES),
        name="experts",
    )(md["tile"], md["exp"], md["kind"], md["lo"], md["hi"], md["first"], md["fresh"],
      md["wslot"], md["next"], xs, wgu, bgu, wd, bd)


def _combine_kernel(slot_ref, x1_ref, post_ref, g_ref, b_ref, eo_hbm, out_ref, ebuf, sems):
    j = pl.program_id(0)
    nt = pl.num_programs(0)
    tm = x1_ref.shape[0]
    cur = lax.rem(j, 2)

    def fetch(tile, which):
        for i in range(N_CHUNKS):
            pltpu.make_async_copy(eo_hbm.at[slot_ref[tile * N_CHUNKS + i]],
                                  _chunk_rows(ebuf.at[which], i), sems.at[which],
                                  ).start(priority=i % DMA_PRIORITIES)

    @pl.when(j == 0)
    def _():
        fetch(0, 0)

    @pl.when(j + 1 < nt)
    def _():
        fetch(j + 1, 1 - cur)

    for _ in range(N_CHUNKS):
        pltpu.make_async_copy(eo_hbm.at[0], _chunk_rows(ebuf.at[cur], 0), sems.at[cur]).wait()

    l_io = lax.broadcasted_iota(I32, (tm, L_SORT), 1).astype(I16)
    post = post_ref[0].astype(I16)
    hit = l_io == post[:, 0:1]
    for k in range(1, TOP_K):
        hit = hit | (l_io == post[:, k:k + 1])
    onehot = jnp.where(hit, jnp.ones((), BF16), jnp.zeros((), BF16))
    y = _dot(onehot, _unpack_bf16(ebuf[cur]))
    out_ref[...] = _layer_norm(np.float32(DEEPNORM_ALPHA) * x1_ref[...] + y, g_ref[...], b_ref[...])


def _combine_call(slots, x1, post, g, b, eo):
    n_tok = x1.shape[0]
    nt = n_tok // TM_SORT
    grid_spec = pltpu.PrefetchScalarGridSpec(
        num_scalar_prefetch=1,
        grid=(nt,),
        in_specs=[
            pl.BlockSpec((TM_SORT, D_MODEL), lambda i, *_: (i, 0)),
            pl.BlockSpec((1, TM_SORT, SUBLANES), lambda i, *_: (i, 0, 0)),
            pl.BlockSpec((1, D_MODEL), lambda i, *_: (0, 0)),
            pl.BlockSpec((1, D_MODEL), lambda i, *_: (0, 0)),
            pl.BlockSpec(memory_space=pl.ANY),
        ],
        out_specs=pl.BlockSpec((TM_SORT, D_MODEL), lambda i, *_: (i, 0)),
        scratch_shapes=[pltpu.VMEM((2, L_SORT, D_PACK), I32), pltpu.SemaphoreType.DMA((2,))],
    )
    return pl.pallas_call(
        _combine_kernel,
        grid_spec=grid_spec,
        out_shape=jax.ShapeDtypeStruct((n_tok, D_MODEL), F32),
        compiler_params=pltpu.CompilerParams(
            dimension_semantics=("arbitrary",), vmem_limit_bytes=VMEM_LIMIT_BYTES),
        name="combine",
    )(slots, x1, post, g, b, eo)


def _excl_cumsum(a, axis):
    n = a.shape[axis]
    lower = (jnp.arange(n)[:, None] > jnp.arange(n)[None, :]).astype(F32)
    af = jnp.moveaxis(a.astype(F32), axis, 0).reshape(n, -1)
    out = jnp.dot(lower, af, precision=lax.Precision.HIGHEST)
    out = out.reshape((n,) + tuple(np.delete(np.array(a.shape), axis)))
    return jnp.moveaxis(out, 0, axis).astype(I32)


def _routing_metadata(cnt):
    nt = cnt.shape[0]
    c8 = (cnt + (SEG_ALIGN - 1)) // SEG_ALIGN * SEG_ALIGN
    seg = _excl_cumsum(c8, 1)
    tot8 = jnp.sum(c8, axis=1)
    n8 = jnp.sum(c8, axis=0)
    base = _excl_cumsum(n8, 0)
    goff = base[None, :] + _excl_cumsum(c8, 0)
    p_used = jnp.sum(n8)

    r = (jnp.arange(N_CHUNKS, dtype=I32) * SEG_ALIGN)[None, :, None]
    in_seg = (r >= seg[:, None, :]) & (r < (seg + c8)[:, None, :])
    real_slot = jnp.sum(jnp.where(in_seg, goff[:, None, :] + r - seg[:, None, :], 0), axis=2) // SEG_ALIGN
    used = tot8 // SEG_ALIGN
    n_empty = N_CHUNKS - used
    ci = jnp.arange(N_CHUNKS, dtype=I32)[None, :]
    empty_slot = p_used // SEG_ALIGN + _excl_cumsum(n_empty, 0)[:, None] + (ci - used[:, None])
    slots = jnp.where(ci < used[:, None], real_slot, empty_slot).reshape(-1).astype(I32)

    assert (nt * L_SORT) % TR == 0
    n_tiles = nt * L_SORT // TR
    n_items = n_tiles + N_EXPERTS
    e_lo, e_hi = base, base + n8
    t_first = e_lo // TR
    t_end = (e_hi + TR - 1) // TR
    w_e = jnp.where(n8 > 0, t_end - t_first, 0)
    w_start = _excl_cumsum(w_e, 0)
    w_cum = w_start + w_e
    w_total = jnp.sum(w_e)
    tiles_used = (p_used + TR - 1) // TR

    w = jnp.arange(n_items, dtype=I32)
    e_ids = jnp.arange(N_EXPERTS, dtype=I32)
    e_w = jnp.minimum(jnp.sum((w[:, None] >= w_cum[None, :]).astype(I32), axis=1), N_EXPERTS - 1)
    is_ffn = w < w_total
    last_e = jnp.max(jnp.where(n8 > 0, e_ids, 0))
    exp = jnp.where(is_ffn, e_w, last_e).astype(I32)
    pick = exp[:, None] == e_ids[None, :]
    look = lambda table: jnp.sum(jnp.where(pick, table[None, :], 0), axis=1)
    ffn_tile = look(t_first) + (w - look(w_start))
    zero_tile = tiles_used + (w - w_total)
    is_zero = (~is_ffn) & (zero_tile < n_tiles)
    tile = jnp.where(is_ffn, ffn_tile, jnp.where(is_zero, zero_tile, n_tiles - 1)).astype(I32)
    kind = jnp.where(is_ffn, KIND_FFN, jnp.where(is_zero, KIND_ZERO, KIND_NOOP)).astype(I32)
    lo = jnp.clip(look(e_lo) - tile * TR, 0, TR).astype(I32)
    hi = jnp.clip(look(e_hi) - tile * TR, 0, TR).astype(I32)
    prev_tile = jnp.concatenate([jnp.full((1,), -1, I32), tile[:-1]])
    first = (tile != prev_tile).astype(I32)
    prev_exp = jnp.concatenate([jnp.full((1,), -1, I32), exp[:-1]])
    fresh = (exp != prev_exp).astype(I32)
    used_e = n8 > 0
    later = (e_ids[None, :] > e_ids[:, None]) & used_e[None, :]
    order_e = jnp.sum(((e_ids[None, :] < e_ids[:, None]) & used_e[None, :]).astype(I32), axis=1)
    next_e = jnp.min(jnp.where(later, e_ids[None, :], N_EXPERTS), axis=1)
    next_e = jnp.where(next_e == N_EXPERTS, -1, next_e)
    wslot = (look(order_e) % 2).astype(I32)
    nxt = look(next_e).astype(I32)
    return dict(slots=slots, tile=tile, exp=exp, kind=kind, lo=lo, hi=hi, first=first, fresh=fresh,
                wslot=wslot, next=nxt)


def kernel(x, mem, w_in, b_gate, conv_w, gmlp_ws, gmlp_b, gmlp_ln_g, gmlp_ln_b, mem_ln_g,
           mem_ln_b, w_kv, w_conv_proj, w_gmlp_proj, w_xa_proj, w_out, ln1_g, ln1_b, w_router,
           b_router, w_gate_up, b_gate_up, w_down, b_down, ln2_g, ln2_b):
    bsz, seq, d = x.shape
    n_tok = bsz * seq
    assert d == D_MODEL and n_tok % TM_SORT == 0
    depth = w_in.shape[0]
    for l in range(depth):
        row = lambda a: a[l].reshape(1, -1)
        k, v = _kv_call(mem, row(mem_ln_g), row(mem_ln_b), w_kv[l].astype(BF16))
        gbias = jnp.repeat(gmlp_b[l].T, GROUP_CH, axis=1)
        wr = jnp.pad(w_router[l], ((0, 0), (0, LANES - N_EXPERTS)))
        wrh = wr.astype(BF16)
        wrl = (wr - wrh.astype(F32)).astype(BF16)
        br = jnp.pad(b_router[l], (0, LANES - N_EXPERTS)).reshape(1, LANES)
        x1, logits = _mix_call(
            x, w_in[l].astype(BF16), row(b_gate), conv_w[l], gmlp_ws[l], gbias,
            row(gmlp_ln_g), row(gmlp_ln_b), k, v, w_conv_proj[l].astype(BF16),
            w_gmlp_proj[l].astype(BF16), w_xa_proj[l].astype(BF16), w_out[l].astype(BF16),
            row(ln1_g), row(ln1_b), wrh, wrl, br)
        x1 = x1.reshape(n_tok, d)
        pos, gtab, cnt = _route_call(logits.reshape(n_tok, LANES))
        md = _routing_metadata(cnt[:, :, 0])
        xs = _sort_call(md["slots"], x1, pos, gtab)
        eo = _expert_call(md, xs.reshape(-1, XS_COLS), w_gate_up[l], b_gate_up[l][:, None, :],
                          w_down[l], b_down[l][:, None, :])
        post = jnp.transpose(pos, (0, 2, 1))
        x = _combine_call(md["slots"], x1, post, row(ln2_g), row(ln2_b),
                          eo.reshape(-1, SEG_ALIGN, D_PACK)).reshape(bsz, seq, d)
    return x
```

```python
import jax
import jax.numpy as jnp
import numpy as np
from jax import lax
from jax.experimental import pallas as pl
from jax.experimental.pallas import tpu as pltpu

F32 = jnp.float32
BF16 = jnp.bfloat16
I32 = jnp.int32
I16 = jnp.int16

D_MODEL = 1024
CHUNK = 128
GMLP_GROUPS = 8
GROUP_CH = D_MODEL // GMLP_GROUPS
MEM_LEN = 256
XA_HEADS = 4
XA_HEAD_DIM = D_MODEL // XA_HEADS
N_EXPERTS = 32
TOP_K = 4
D_FF = D_MODEL
W_IN_COLS = 9 * D_MODEL
SWIGLU_LIMIT = 7.0
SWIGLU_ALPHA = 1.702
LN_EPS = 1e-5
DEEPNORM_ALPHA = 2.0 ** 0.25

LANES = 128
SUBLANES = 8
VMEM_LIMIT_BYTES = 60000 * 1024

TM_MIX = 512
TM_SORT = 512
SEG_ALIGN = SUBLANES
MXU_DIM = 256
L_SORT = -(-(TOP_K * TM_SORT + N_EXPERTS * (SEG_ALIGN - 1)) // MXU_DIM) * MXU_DIM
assert L_SORT >= TOP_K * TM_SORT + N_EXPERTS * (SEG_ALIGN - 1) and L_SORT % MXU_DIM == 0
assert L_SORT < 2 ** 15
N_CHUNKS = L_SORT // SEG_ALIGN
D_PACK = D_MODEL // 2
XS_COLS = D_PACK + LANES
GTAB_EID = 3 * TOP_K
TR = 512
TR_SUB = 256
assert TR % TR_SUB == 0


def _pack_pair(lo, hi):
    lo_bits = lax.shift_right_logical(lax.bitcast_convert_type(lo, I32), 16)
    return lo_bits | lax.bitcast_convert_type(hi, I32)


def _unpack_bf16(words):
    lo = lax.bitcast_convert_type(lax.shift_left(words, 16), F32)
    hi = lax.bitcast_convert_type(words & np.int32(-65536), F32)
    return jnp.concatenate([lo.astype(BF16), hi.astype(BF16)], axis=1)


def _layer_norm(x, g, b):
    mu = jnp.mean(x, axis=-1, keepdims=True)
    xc = x - mu
    var = jnp.mean(xc * xc, axis=-1, keepdims=True)
    return xc * lax.rsqrt(var + LN_EPS) * g + b


def _gelu_tanh(x):
    c = np.float32(np.sqrt(2.0 / np.pi))
    ca = np.float32(np.sqrt(2.0 / np.pi) * 0.044715)
    half = 0.5 * x
    return half + half * jnp.tanh(x * (c + ca * (x * x)))


def _sigmoid(x):
    return 1.0 / (1.0 + jnp.exp(-x))


def _dot(a, b):
    return jnp.dot(a, b, preferred_element_type=F32)


def _const_spec(shape):
    n = len(shape)
    return pl.BlockSpec(shape, lambda *_: (0,) * n, pipeline_mode=pl.Buffered(1))


def _kv_kernel(mem_ref, g_ref, b_ref, wkv_ref, k_ref, v_ref):
    mem_n = _layer_norm(mem_ref[0], g_ref[...], b_ref[...])
    kv = _dot(mem_n.astype(BF16), wkv_ref[...])
    k_ref[0] = kv[:, :D_MODEL].astype(BF16)
    v_ref[0] = kv[:, D_MODEL:].astype(BF16)


def _kv_call(mem, g, b, wkv):
    bsz = mem.shape[0]
    return pl.pallas_call(
        _kv_kernel,
        grid=(bsz,),
        in_specs=[
            pl.BlockSpec((1, MEM_LEN, D_MODEL), lambda i: (i, 0, 0)),
            pl.BlockSpec((1, D_MODEL), lambda i: (0, 0)),
            pl.BlockSpec((1, D_MODEL), lambda i: (0, 0)),
            pl.BlockSpec((D_MODEL, 2 * D_MODEL), lambda i: (0, 0)),
        ],
        out_specs=[
            pl.BlockSpec((1, MEM_LEN, D_MODEL), lambda i: (i, 0, 0)),
            pl.BlockSpec((1, MEM_LEN, D_MODEL), lambda i: (i, 0, 0)),
        ],
        out_shape=[
            jax.ShapeDtypeStruct((bsz, MEM_LEN, D_MODEL), BF16),
            jax.ShapeDtypeStruct((bsz, MEM_LEN, D_MODEL), BF16),
        ],
        compiler_params=pltpu.CompilerParams(dimension_semantics=("arbitrary",)),
        name="kv",
    )(mem, g, b, wkv)


def _mix_kernel(x_ref, win_ref, bgate_ref, convw_ref, ws_ref, gbias_ref, glng_ref, glnb_ref,
                k_ref, v_ref, wc_ref, wg_ref, wx_ref, wo_ref, ln1g_ref, ln1b_ref,
                wr_pair_ref, wr_top_ref, br_ref, x1_ref, logit_ref, halo_ref):
    tm = x_ref.shape[1]
    d = D_MODEL
    x = x_ref[0]
    xb = x.astype(BF16)

    def proj(i):
        return _dot(xb, win_ref[:, i * d:(i + 1) * d])

    def gate(i):
        return _sigmoid(proj(6 + i) + bgate_ref[:, i * d:(i + 1) * d])

    @pl.when(pl.program_id(1) == 0)
    def _():
        halo_ref[...] = jnp.zeros((SUBLANES, d), F32)

    u = proj(1) * proj(2)
    halo = halo_ref[...]
    halo_ref[...] = u[tm - SUBLANES:tm, :]
    head_row = lax.broadcasted_iota(I32, (SUBLANES, d), 0)

    def shifted(k):
        body = pltpu.roll(u, k, axis=0)
        head = jnp.where(head_row < k, pltpu.roll(halo, k, axis=0), body[0:SUBLANES])
        return jnp.concatenate([head, body[SUBLANES:]], axis=0)

    conv = convw_ref[0:1, :] * shifted(2) + convw_ref[1:2, :] * shifted(1) + convw_ref[2:3, :] * u
    gu_lin = proj(3)
    gv_lin = proj(4)
    y_conv = (proj(0) * conv).astype(BF16)
    q = proj(5).astype(BF16)
    scores = []
    for h in range(XA_HEADS):
        sl = slice(h * XA_HEAD_DIM, (h + 1) * XA_HEAD_DIM)
        scores.append(lax.dot_general(q[:, sl], k_ref[0, :, sl], (((1,), (1,)), ((), ())),
                                      preferred_element_type=F32))
    g_conv = gate(0)
    g_gmlp = gate(1)
    merged = g_conv * _dot(y_conv, wc_ref[...])

    gu = _gelu_tanh(gu_lin)
    gv = _gelu_tanh(gv_lin)
    vn = _layer_norm(gv, glng_ref[...], glnb_ref[...]).astype(BF16)
    n_chunks = tm // CHUNK
    row_i = lax.broadcasted_iota(I32, (CHUNK, CHUNK), 0)
    col_i = lax.broadcasted_iota(I32, (CHUNK, CHUNK), 1)
    causal = col_i <= row_i
    f_cols = []
    for g in range(GMLP_GROUPS):
        w_g = jnp.where(causal, ws_ref[g], 0.0).astype(BF16)
        rhs = jnp.concatenate(
            [vn[c * CHUNK:(c + 1) * CHUNK, g * GROUP_CH:(g + 1) * GROUP_CH] for c in range(n_chunks)],
            axis=1)
        fg = _dot(w_g, rhs)
        f_cols.append(jnp.concatenate(
            [fg[:, c * GROUP_CH:(c + 1) * GROUP_CH] for c in range(n_chunks)], axis=0))
    f = jnp.concatenate(f_cols, axis=1)
    gbias = jnp.concatenate([gbias_ref[...]] * n_chunks, axis=0)
    y_gmlp = gu * (f + gbias)
    merged = merged + g_gmlp * _dot(y_gmlp.astype(BF16), wg_ref[...])

    heads = []
    for h in range(XA_HEADS):
        sl = slice(h * XA_HEAD_DIM, (h + 1) * XA_HEAD_DIM)
        s = scores[h] * np.float32(XA_HEAD_DIM ** -0.5)
        p = jnp.exp(s - jnp.max(s, axis=-1, keepdims=True))
        p = p * (1.0 / jnp.sum(p, axis=-1, keepdims=True))
        heads.append(_dot(p.astype(BF16), v_ref[0, :, sl]))
    y_xa = jnp.concatenate(heads, axis=1)
    merged = merged + gate(2) * _dot(y_xa.astype(BF16), wx_ref[...])

    mix = _dot(merged.astype(BF16), wo_ref[...])
    x1 = _layer_norm(np.float32(DEEPNORM_ALPHA) * x + mix, ln1g_ref[...], ln1b_ref[...])
    x1_ref[0] = x1

    x1h = x1.astype(BF16)
    x1l = (x1 - x1h.astype(F32)).astype(BF16)
    both = _dot(x1h, wr_pair_ref[...])
    logit_ref[0] = (both + pltpu.roll(both, LANES - N_EXPERTS, axis=1) + _dot(x1l, wr_top_ref[...])
                    + br_ref[...])


def _mix_call(x, win, bgate, convw, ws, gbias, glng, glnb, k, v, wc, wg, wx, wo, ln1g, ln1b,
              wr_pair, wr_top, br):
    bsz, seq, d = x.shape
    tm = min(TM_MIX, seq)
    assert seq % tm == 0 and tm % CHUNK == 0
    tile = lambda b, s: (b, s, 0)
    per_batch = lambda b, s: (b, 0, 0)
    return pl.pallas_call(
        _mix_kernel,
        grid=(bsz, seq // tm),
        in_specs=[
            pl.BlockSpec((1, tm, d), tile),
            _const_spec((d, W_IN_COLS)),
            _const_spec((1, 3 * d)),
            _const_spec((3, d)),
            _const_spec((GMLP_GROUPS, CHUNK, CHUNK)),
            _const_spec((CHUNK, d)),
            _const_spec((1, d)),
            _const_spec((1, d)),
            pl.BlockSpec((1, MEM_LEN, d), per_batch),
            pl.BlockSpec((1, MEM_LEN, d), per_batch),
            _const_spec((d, d)),
            _const_spec((d, d)),
            _const_spec((d, d)),
            _const_spec((d, d)),
            _const_spec((1, d)),
            _const_spec((1, d)),
            _const_spec((d, LANES)),
            _const_spec((d, LANES)),
            _const_spec((1, LANES)),
        ],
        out_specs=[pl.BlockSpec((1, tm, d), tile), pl.BlockSpec((1, tm, LANES), tile)],
        out_shape=[jax.ShapeDtypeStruct((bsz, seq, d), F32),
                   jax.ShapeDtypeStruct((bsz, seq, LANES), F32)],
        scratch_shapes=[pltpu.VMEM((SUBLANES, d), F32)],
        compiler_params=pltpu.CompilerParams(
            dimension_semantics=("arbitrary", "arbitrary"), vmem_limit_bytes=VMEM_LIMIT_BYTES),
        name="mix",
    )(x, win, bgate, convw, ws, gbias, glng, glnb, k, v, wc, wg, wx, wo, ln1g, ln1b, wr_pair, wr_top, br)


def _split3(v):
    hi = v.astype(BF16).astype(F32)
    mid = (v - hi).astype(BF16).astype(F32)
    return hi, mid, (v - hi - mid).astype(BF16).astype(F32)


def _route_kernel(logit_ref, pos_ref, gtab_ref, cnt_ref):
    tm = logit_ref.shape[0]
    work = logit_ref[...].T[0:N_EXPERTS, :]
    e_io = lax.broadcasted_iota(I32, (N_EXPERTS, tm), 0)
    neg = np.float32(-np.inf)
    vals, sels, eids = [], [], []
    for _ in range(TOP_K):
        m = jnp.max(work, axis=0, keepdims=True)
        idx = jnp.min(jnp.where(work == m, e_io, LANES), axis=0, keepdims=True)
        sel = e_io == idx
        vals.append(m)
        sels.append(sel)
        eids.append(idx.astype(F32))
        work = jnp.where(sel, neg, work)
    ex = [jnp.exp(vk - vals[0]) for vk in vals]
    inv = 1.0 / (ex[0] + ex[1] + ex[2] + ex[3])
    gates = [e * inv for e in ex]

    oh = jnp.where(sels[0] | sels[1] | sels[2] | sels[3], 1.0, 0.0).astype(F32)
    cnt = jnp.sum(oh, axis=1, keepdims=True)
    t_r = lax.broadcasted_iota(I32, (tm, tm), 0)
    t_c = lax.broadcasted_iota(I32, (tm, tm), 1)
    before = jnp.where(t_r < t_c, 1.0, 0.0).astype(BF16)
    rank = _dot(oh.astype(BF16), before)
    units = jnp.floor((cnt + np.float32(SEG_ALIGN - 1)) * np.float32(1.0 / SEG_ALIGN))
    units_b = jnp.concatenate(
        [jnp.broadcast_to(units, (N_EXPERTS, LANES)), jnp.zeros((LANES - N_EXPERTS, LANES), F32)],
        axis=0).astype(BF16)
    e_r = lax.broadcasted_iota(I32, (N_EXPERTS, LANES), 0)
    e_c = lax.broadcasted_iota(I32, (N_EXPERTS, LANES), 1)
    below = jnp.where(e_c < e_r, 1.0, 0.0).astype(BF16)
    seg = _dot(below, units_b)[:, 0:1] * np.float32(SEG_ALIGN)
    slot = seg + rank

    pos_rows = [jnp.sum(jnp.where(s, slot, 0.0), axis=0, keepdims=True) for s in sels]
    zero_row = jnp.zeros((1, tm), F32)
    pos_ref[0] = jnp.concatenate(pos_rows + [zero_row] * (SUBLANES - TOP_K), axis=0).astype(I32)
    parts = [p for g in gates for p in _split3(g)]
    table = jnp.concatenate(parts + eids + [jnp.zeros((LANES - len(parts) - TOP_K, tm), F32)], axis=0)
    gtab_ref[...] = table.T.astype(BF16)
    cnt_ref[0] = jnp.broadcast_to(cnt, (N_EXPERTS, LANES)).astype(I32)


def _route_call(logits2d):
    n_tok = logits2d.shape[0]
    nt = n_tok // TM_SORT
    return pl.pallas_call(
        _route_kernel,
        grid=(nt,),
        in_specs=[pl.BlockSpec((TM_SORT, LANES), lambda i: (i, 0))],
        out_specs=[
            pl.BlockSpec((1, SUBLANES, TM_SORT), lambda i: (i, 0, 0)),
            pl.BlockSpec((TM_SORT, LANES), lambda i: (i, 0)),
            pl.BlockSpec((1, N_EXPERTS, LANES), lambda i: (i, 0, 0)),
        ],
        out_shape=[
            jax.ShapeDtypeStruct((nt, SUBLANES, TM_SORT), I32),
            jax.ShapeDtypeStruct((n_tok, LANES), BF16),
            jax.ShapeDtypeStruct((nt, N_EXPERTS, LANES), I32),
        ],
        compiler_params=pltpu.CompilerParams(dimension_semantics=("arbitrary",)),
        name="route",
    )(logits2d)


def _chunk_rows(buf, i):
    return buf.at[pl.ds(i * SEG_ALIGN, SEG_ALIGN)]


def _sort_kernel(slot_ref, x1_ref, pos_ref, gtab_ref, xs_hbm, sbuf, sems):
    j = pl.program_id(0)
    nt = pl.num_programs(0)
    tm = x1_ref.shape[0]
    cur = lax.rem(j, 2)
    buf = sbuf.at[cur]
    j_io = lax.broadcasted_iota(I32, (L_SORT, tm), 0).astype(I16)
    pos = pos_ref[0].astype(I16)
    hit = j_io == pos[0:1, :]
    for k in range(1, TOP_K):
        hit = hit | (j_io == pos[k:k + 1, :])
    onehot = jnp.where(hit, jnp.ones((), BF16), jnp.zeros((), BF16))
    rows = _dot(onehot, jnp.concatenate([x1_ref[...].astype(BF16), gtab_ref[...]], axis=1))
    buf[:, 0:D_PACK] = _pack_pair(rows[:, 0:D_PACK], rows[:, D_PACK:D_MODEL])
    buf[:, D_PACK:XS_COLS] = lax.bitcast_convert_type(rows[:, D_MODEL:D_MODEL + LANES], I32)

    for i in range(N_CHUNKS):
        pltpu.make_async_copy(
            _chunk_rows(buf, i), xs_hbm.at[slot_ref[j * N_CHUNKS + i]], sems.at[cur]).start()

    def drain(which):
        for _ in range(N_CHUNKS):
            pltpu.make_async_copy(
                _chunk_rows(sbuf.at[which], 0), xs_hbm.at[0], sems.at[which]).wait()

    @pl.when(j > 0)
    def _():
        drain(1 - cur)

    @pl.when(j == nt - 1)
    def _():
        drain(cur)


def _sort_call(slots, x1, pos, gtab):
    n_tok = x1.shape[0]
    nt = n_tok // TM_SORT
    grid_spec = pltpu.PrefetchScalarGridSpec(
        num_scalar_prefetch=1,
        grid=(nt,),
        in_specs=[
            pl.BlockSpec((TM_SORT, D_MODEL), lambda i, *_: (i, 0)),
            pl.BlockSpec((1, SUBLANES, TM_SORT), lambda i, *_: (i, 0, 0)),
            pl.BlockSpec((TM_SORT, LANES), lambda i, *_: (i, 0)),
        ],
        out_specs=pl.BlockSpec(memory_space=pl.ANY),
        scratch_shapes=[
            pltpu.VMEM((2, L_SORT, XS_COLS), I32),
            pltpu.SemaphoreType.DMA((2,)),
        ],
    )
    return pl.pallas_call(
        _sort_kernel,
        grid_spec=grid_spec,
        out_shape=jax.ShapeDtypeStruct((nt * N_CHUNKS, SEG_ALIGN, XS_COLS), I32),
        compiler_params=pltpu.CompilerParams(
            dimension_semantics=("arbitrary",), vmem_limit_bytes=VMEM_LIMIT_BYTES),
        name="sort",
    )(slots, x1, pos, gtab)


KIND_NOOP, KIND_FFN, KIND_ZERO = 0, 1, 2


def _expert_kernel(tile_ref, exp_ref, kind_ref, lo_ref, hi_ref, first_ref, fresh_ref, wslot_ref,
                   next_ref, xs_ref, wgu_hbm, bgu_ref, wd_hbm, bd_ref, out_ref,
                   wgu_b, wd_b, stage_gu, stage_d, wsems):
    w = pl.program_id(0)
    kind = kind_ref[w]

    @pl.when(kind == KIND_ZERO)
    def _():
        out_ref[...] = jnp.zeros(out_ref.shape, I32)

    def weight_copies(e, s):
        return (pltpu.make_async_copy(wgu_hbm.at[e], stage_gu.at[s], wsems.at[0, s]),
                pltpu.make_async_copy(wd_hbm.at[e], stage_d.at[s], wsems.at[1, s]))

    @pl.when((kind == KIND_FFN) & (fresh_ref[w] == 1))
    def _():
        e = exp_ref[w]
        s = wslot_ref[w]

        @pl.when(w == 0)
        def _():
            for cp in weight_copies(e, s):
                cp.start()

        for cp in weight_copies(e, s):
            cp.wait()
        wgu_b[...] = stage_gu[s].astype(BF16)
        wd_b[...] = stage_d[s].astype(BF16)

        @pl.when(next_ref[w] >= 0)
        def _():
            for cp in weight_copies(next_ref[w], 1 - s):
                cp.start()

    lo = lo_ref[w]
    hi = hi_ref[w]
    first = first_ref[w]
    is_ffn = kind == KIND_FFN
    whole = (lo == 0) & (hi == TR)

    def ffn(r0, nr):
        xs = _unpack_bf16(xs_ref[r0:r0 + nr, 0:D_PACK])
        gtab = lax.bitcast_convert_type(xs_ref[r0:r0 + nr, D_PACK:XS_COLS], F32)
        this_e = exp_ref[w].astype(F32)
        gate = jnp.zeros((nr, 1), F32)
        for k in range(TOP_K):
            g_k = gtab[:, 3 * k:3 * k + 1] + gtab[:, 3 * k + 1:3 * k + 2] + gtab[:, 3 * k + 2:3 * k + 3]
            gate = gate + jnp.where(gtab[:, GTAB_EID + k:GTAB_EID + k + 1] == this_e, g_k, 0.0)
        gu = _dot(xs, wgu_b[...]) + bgu_ref[0]
        g_lin = jnp.minimum(gu[:, :D_FF], np.float32(SWIGLU_LIMIT))
        u_lin = jnp.clip(gu[:, D_FF:], np.float32(-SWIGLU_LIMIT), np.float32(SWIGLU_LIMIT))
        act = (u_lin + 1.0) * (g_lin * _sigmoid(np.float32(SWIGLU_ALPHA) * g_lin))
        eo = (_dot(act.astype(BF16), wd_b[...]) + bd_ref[0]) * gate
        eo = eo.astype(BF16).astype(F32)
        return _pack_pair(eo[:, 0:D_PACK], eo[:, D_PACK:D_MODEL])

    @pl.when(is_ffn & whole)
    def _():
        out_ref[...] = ffn(0, TR)

    for r0 in range(0, TR, TR_SUB):
        touched = (lo < r0 + TR_SUB) & (hi > r0)
        part = is_ffn & jnp.logical_not(whole)

        @pl.when(part & touched)
        def _(r0=r0):
            eo = ffn(r0, TR_SUB)
            rows = r0 + lax.broadcasted_iota(I32, (TR_SUB, 1), 0)
            mine = (rows >= lo) & (rows < hi)

            @pl.when(first == 1)
            def _():
                out_ref[r0:r0 + TR_SUB, :] = jnp.where(mine, eo, 0)

            @pl.when(first == 0)
            def _():
                out_ref[r0:r0 + TR_SUB, :] = jnp.where(mine, eo, out_ref[r0:r0 + TR_SUB, :])

        @pl.when(part & jnp.logical_not(touched) & (first == 1))
        def _(r0=r0):
            out_ref[r0:r0 + TR_SUB, :] = jnp.zeros((TR_SUB, D_PACK), I32)


def _expert_call(md, xs, wgu, bgu, wd, bd):
    pmax = xs.shape[0]
    n_items = md["tile"].shape[0]
    grid_spec = pltpu.PrefetchScalarGridSpec(
        num_scalar_prefetch=9,
        grid=(n_items,),
        in_specs=[
            pl.BlockSpec((TR, XS_COLS), lambda w, t, e, *_: (t[w], 0)),
            pl.BlockSpec(memory_space=pl.ANY),
            pl.BlockSpec((1, 1, 2 * D_FF), lambda w, t, e, *_: (e[w], 0, 0)),
            pl.BlockSpec(memory_space=pl.ANY),
            pl.BlockSpec((1, 1, D_MODEL), lambda w, t, e, *_: (e[w], 0, 0)),
        ],
        out_specs=pl.BlockSpec((TR, D_PACK), lambda w, t, e, *_: (t[w], 0)),
        scratch_shapes=[
            pltpu.VMEM((D_MODEL, 2 * D_FF), BF16),
            pltpu.VMEM((D_FF, D_MODEL), BF16),
            pltpu.VMEM((2, D_MODEL, 2 * D_FF), F32),
            pltpu.VMEM((2, D_FF, D_MODEL), F32),
            pltpu.SemaphoreType.DMA((2, 2)),
        ],
    )
    return pl.pallas_call(
        _expert_kernel,
        grid_spec=grid_spec,
        out_shape=jax.ShapeDtypeStruct((pmax, D_PACK), I32),
        compiler_params=pltpu.CompilerParams(
            dimension_semantics=("arbitrary",), vmem_limit_bytes=VMEM_LIMIT_BYTES),
        name="experts",
    )(md["tile"], md["exp"], md["kind"], md["lo"], md["hi"], md["first"], md["fresh"],
      md["wslot"], md["next"], xs, wgu, bgu, wd, bd)


def _combine_kernel(slot_ref, x1_ref, post_ref, g_ref, b_ref, eo_hbm, out_ref, ebuf, sems):
    j = pl.program_id(0)
    nt = pl.num_programs(0)
    tm = x1_ref.shape[0]
    cur = lax.rem(j, 2)

    def fetch(tile, which):
        for i in range(N_CHUNKS):
            pltpu.make_async_copy(eo_hbm.at[slot_ref[tile * N_CHUNKS + i]],
                                  _chunk_rows(ebuf.at[which], i), sems.at[which]).start()

    @pl.when(j == 0)
    def _():
        fetch(0, 0)

    @pl.when(j + 1 < nt)
    def _():
        fetch(j + 1, 1 - cur)

    for _ in range(N_CHUNKS):
        pltpu.make_async_copy(eo_hbm.at[0], _chunk_rows(ebuf.at[cur], 0), sems.at[cur]).wait()

    l_io = lax.broadcasted_iota(I32, (tm, L_SORT), 1).astype(I16)
    post = post_ref[0].astype(I16)
    hit = l_io == post[:, 0:1]
    for k in range(1, TOP_K):
        hit = hit | (l_io == post[:, k:k + 1])
    onehot = jnp.where(hit, jnp.ones((), BF16), jnp.zeros((), BF16))
    y = _dot(onehot, _unpack_bf16(ebuf[cur]))
    out_ref[...] = _layer_norm(np.float32(DEEPNORM_ALPHA) * x1_ref[...] + y, g_ref[...], b_ref[...])


def _combine_call(slots, x1, post, g, b, eo):
    n_tok = x1.shape[0]
    nt = n_tok // TM_SORT
    grid_spec = pltpu.PrefetchScalarGridSpec(
        num_scalar_prefetch=1,
        grid=(nt,),
        in_specs=[
            pl.BlockSpec((TM_SORT, D_MODEL), lambda i, *_: (i, 0)),
            pl.BlockSpec((1, TM_SORT, SUBLANES), lambda i, *_: (i, 0, 0)),
            pl.BlockSpec((1, D_MODEL), lambda i, *_: (0, 0)),
            pl.BlockSpec((1, D_MODEL), lambda i, *_: (0, 0)),
            pl.BlockSpec(memory_space=pl.ANY),
        ],
        out_specs=pl.BlockSpec((TM_SORT, D_MODEL), lambda i, *_: (i, 0)),
        scratch_shapes=[pltpu.VMEM((2, L_SORT, D_PACK), I32), pltpu.SemaphoreType.DMA((2,))],
    )
    return pl.pallas_call(
        _combine_kernel,
        grid_spec=grid_spec,
        out_shape=jax.ShapeDtypeStruct((n_tok, D_MODEL), F32),
        compiler_params=pltpu.CompilerParams(
            dimension_semantics=("arbitrary",), vmem_limit_bytes=VMEM_LIMIT_BYTES),
        name="combine",
    )(slots, x1, post, g, b, eo)


def _excl_cumsum(a, axis):
    n = a.shape[axis]
    lower = (jnp.arange(n)[:, None] > jnp.arange(n)[None, :]).astype(F32)
    af = jnp.moveaxis(a.astype(F32), axis, 0).reshape(n, -1)
    out = jnp.dot(lower, af, precision=lax.Precision.HIGHEST)
    out = out.reshape((n,) + tuple(np.delete(np.array(a.shape), axis)))
    return jnp.moveaxis(out, 0, axis).astype(I32)


def _routing_metadata(cnt):
    nt = cnt.shape[0]
    c8 = (cnt + (SEG_ALIGN - 1)) // SEG_ALIGN * SEG_ALIGN
    seg = _excl_cumsum(c8, 1)
    tot8 = jnp.sum(c8, axis=1)
    n8 = jnp.sum(c8, axis=0)
    n_al = (n8 + (TR_SUB - 1)) // TR_SUB * TR_SUB
    roomy = jnp.sum(n_al) <= nt * L_SORT
    span = jnp.where(roomy, n_al, n8)
    base = _excl_cumsum(span, 0)
    goff = base[None, :] + _excl_cumsum(c8, 0)
    p_used = jnp.sum(span)
    gap = (span - n8) // SEG_ALIGN
    gap_start = _excl_cumsum(gap, 0)
    gap_end = gap_start + gap
    n_gap = jnp.sum(gap)

    r = (jnp.arange(N_CHUNKS, dtype=I32) * SEG_ALIGN)[None, :, None]
    in_seg = (r >= seg[:, None, :]) & (r < (seg + c8)[:, None, :])
    real_slot = jnp.sum(jnp.where(in_seg, goff[:, None, :] + r - seg[:, None, :], 0), axis=2) // SEG_ALIGN
    used = tot8 // SEG_ALIGN
    n_empty = N_CHUNKS - used
    ci = jnp.arange(N_CHUNKS, dtype=I32)[None, :]
    m = (_excl_cumsum(n_empty, 0)[:, None] + (ci - used[:, None]))[:, :, None]
    in_gap = (m >= gap_start[None, None, :]) & (m < gap_end[None, None, :])
    gap_slot = jnp.sum(jnp.where(in_gap, ((base + n8) // SEG_ALIGN - gap_start)[None, None, :] + m, 0),
                       axis=2)
    m = m[:, :, 0]
    empty_slot = jnp.where(m < n_gap, gap_slot, p_used // SEG_ALIGN + (m - n_gap))
    slots = jnp.where(ci < used[:, None], real_slot, empty_slot).reshape(-1).astype(I32)

    assert (nt * L_SORT) % TR == 0
    n_tiles = nt * L_SORT // TR
    n_items = n_tiles + N_EXPERTS
    e_lo, e_hi = base, base + n8
    t_first = e_lo // TR
    t_end = (e_hi + TR - 1) // TR
    w_e = jnp.where(n8 > 0, t_end - t_first, 0)
    w_start = _excl_cumsum(w_e, 0)
    w_cum = w_start + w_e
    w_total = jnp.sum(w_e)
    tiles_used = (p_used + TR - 1) // TR

    w = jnp.arange(n_items, dtype=I32)
    e_ids = jnp.arange(N_EXPERTS, dtype=I32)
    e_w = jnp.minimum(jnp.sum((w[:, None] >= w_cum[None, :]).astype(I32), axis=1), N_EXPERTS - 1)
    is_ffn = w < w_total
    last_e = jnp.max(jnp.where(n8 > 0, e_ids, 0))
    exp = jnp.where(is_ffn, e_w, last_e).astype(I32)
    pick = exp[:, None] == e_ids[None, :]
    look = lambda table: jnp.sum(jnp.where(pick, table[None, :], 0), axis=1)
    ffn_tile = look(t_first) + (w - look(w_start))
    zero_tile = tiles_used + (w - w_total)
    is_zero = (~is_ffn) & (zero_tile < n_tiles)
    tile = jnp.where(is_ffn, ffn_tile, jnp.where(is_zero, zero_tile, n_tiles - 1)).astype(I32)
    kind = jnp.where(is_ffn, KIND_FFN, jnp.where(is_zero, KIND_ZERO, KIND_NOOP)).astype(I32)
    lo = jnp.clip(look(e_lo) - tile * TR, 0, TR).astype(I32)
    hi = jnp.clip(look(e_hi) - tile * TR, 0, TR).astype(I32)
    prev_tile = jnp.concatenate([jnp.full((1,), -1, I32), tile[:-1]])
    first = (tile != prev_tile).astype(I32)
    prev_exp = jnp.concatenate([jnp.full((1,), -1, I32), exp[:-1]])
    fresh = (exp != prev_exp).astype(I32)
    used_e = n8 > 0
    later = (e_ids[None, :] > e_ids[:, None]) & used_e[None, :]
    order_e = jnp.sum(((e_ids[None, :] < e_ids[:, None]) & used_e[None, :]).astype(I32), axis=1)
    next_e = jnp.min(jnp.where(later, e_ids[None, :], N_EXPERTS), axis=1)
    next_e = jnp.where(next_e == N_EXPERTS, -1, next_e)
    wslot = (look(order_e) % 2).astype(I32)
    nxt = look(next_e).astype(I32)
    return dict(slots=slots, tile=tile, exp=exp, kind=kind, lo=lo, hi=hi, first=first, fresh=fresh,
                wslot=wslot, next=nxt)


def kernel(x, mem, w_in, b_gate, conv_w, gmlp_ws, gmlp_b, gmlp_ln_g, gmlp_ln_b, mem_ln_g,
           mem_ln_b, w_kv, w_conv_proj, w_gmlp_proj, w_xa_proj, w_out, ln1_g, ln1_b, w_router,
           b_router, w_gate_up, b_gate_up, w_down, b_down, ln2_g, ln2_b):
    bsz, seq, d = x.shape
    n_tok = bsz * seq
    assert d == D_MODEL and n_tok % TM_SORT == 0
    depth = w_in.shape[0]
    for l in range(depth):
        row = lambda a: a[l].reshape(1, -1)
        k, v = _kv_call(mem, row(mem_ln_g), row(mem_ln_b), w_kv[l].astype(BF16))
        gbias = jnp.repeat(gmlp_b[l].T, GROUP_CH, axis=1)
        wr_hi = w_router[l].astype(BF16)
        wr_lo = (w_router[l] - wr_hi.astype(F32)).astype(BF16)
        wr_pair = jnp.pad(jnp.concatenate([wr_hi, wr_lo], axis=1), ((0, 0), (0, LANES - 2 * N_EXPERTS)))
        wr_top = jnp.pad(wr_hi, ((0, 0), (0, LANES - N_EXPERTS)))
        br = jnp.pad(b_router[l], (0, LANES - N_EXPERTS)).reshape(1, LANES)
        x1, logits = _mix_call(
            x, w_in[l].astype(BF16), row(b_gate), conv_w[l], gmlp_ws[l], gbias,
            row(gmlp_ln_g), row(gmlp_ln_b), k, v, w_conv_proj[l].astype(BF16),
            w_gmlp_proj[l].astype(BF16), w_xa_proj[l].astype(BF16), w_out[l].astype(BF16),
            row(ln1_g), row(ln1_b), wr_pair, wr_top, br)
        x1 = x1.reshape(n_tok, d)
        pos, gtab, cnt = _route_call(logits.reshape(n_tok, LANES))
        md = _routing_metadata(cnt[:, :, 0])
        xs = _sort_call(md["slots"], x1, pos, gtab)
        eo = _expert_call(md, xs.reshape(-1, XS_COLS), w_gate_up[l], b_gate_up[l][:, None, :],
                          w_down[l], b_down[l][:, None, :])
        post = jnp.transpose(pos, (0, 2, 1))
        x = _combine_call(md["slots"], x1, post, row(ln2_g), row(ln2_b),
                          eo.reshape(-1, SEG_ALIGN, D_PACK)).reshape(bsz, seq, d)
    return x
```

```python
import jax
import jax.numpy as jnp
import numpy as np
from jax import lax
from jax.experimental import pallas as pl
from jax.experimental.pallas import tpu as pltpu

F32 = jnp.float32
BF16 = jnp.bfloat16
I32 = jnp.int32
I16 = jnp.int16

D_MODEL = 1024
CHUNK = 128
GMLP_GROUPS = 8
GROUP_CH = D_MODEL // GMLP_GROUPS
MEM_LEN = 256
XA_HEADS = 4
XA_HEAD_DIM = D_MODEL // XA_HEADS
N_EXPERTS = 32
TOP_K = 4
D_FF = D_MODEL
W_IN_COLS = 9 * D_MODEL
SWIGLU_LIMIT = 7.0
SWIGLU_ALPHA = 1.702
LN_EPS = 1e-5
DEEPNORM_ALPHA = 2.0 ** 0.25

LANES = 128
SUBLANES = 8
VMEM_LIMIT_BYTES = 60000 * 1024

TM_MIX = 512
TM_SORT = 512
SEG_ALIGN = SUBLANES
MXU_DIM = 256
L_SORT = -(-(TOP_K * TM_SORT + N_EXPERTS * (SEG_ALIGN - 1)) // MXU_DIM) * MXU_DIM
assert L_SORT >= TOP_K * TM_SORT + N_EXPERTS * (SEG_ALIGN - 1) and L_SORT % MXU_DIM == 0
assert L_SORT < 2 ** 15
N_CHUNKS = L_SORT // SEG_ALIGN
D_PACK = D_MODEL // 2
XS_COLS = D_PACK + LANES
GTAB_EID = 3 * TOP_K
TR = 512
TR_SUB = 256
assert TR % TR_SUB == 0


BF16_BITS = 16
TOP_HALF = np.int32(-(1 << BF16_BITS))


def _pack_pair(lo, hi):
    lo_bits = lax.shift_right_logical(lax.bitcast_convert_type(lo, I32), BF16_BITS)
    return lo_bits | lax.bitcast_convert_type(hi, I32)


def _unpack_bf16(words):
    lo = lax.bitcast_convert_type(lax.shift_left(words, BF16_BITS), F32)
    hi = lax.bitcast_convert_type(words & TOP_HALF, F32)
    return jnp.concatenate([lo.astype(BF16), hi.astype(BF16)], axis=1)


def _layer_norm(x, g, b):
    mu = jnp.mean(x, axis=-1, keepdims=True)
    xc = x - mu
    var = jnp.mean(xc * xc, axis=-1, keepdims=True)
    return xc * lax.rsqrt(var + LN_EPS) * g + b


def _gelu_tanh(x):
    c = np.float32(np.sqrt(2.0 / np.pi))
    ca = np.float32(np.sqrt(2.0 / np.pi) * 0.044715)
    half = 0.5 * x
    return half + half * jnp.tanh(x * (c + ca * (x * x)))


def _sigmoid(x):
    return 1.0 / (1.0 + jnp.exp(-x))


def _dot(a, b):
    return jnp.dot(a, b, preferred_element_type=F32)


def _const_spec(shape):
    n = len(shape)
    return pl.BlockSpec(shape, lambda *_: (0,) * n, pipeline_mode=pl.Buffered(1))


def _kv_kernel(mem_ref, g_ref, b_ref, wkv_ref, k_ref, v_ref):
    mem_n = _layer_norm(mem_ref[0], g_ref[...], b_ref[...])
    kv = _dot(mem_n.astype(BF16), wkv_ref[...])
    k_ref[0] = kv[:, :D_MODEL].astype(BF16)
    v_ref[0] = kv[:, D_MODEL:].astype(BF16)


def _kv_call(mem, g, b, wkv):
    bsz = mem.shape[0]
    return pl.pallas_call(
        _kv_kernel,
        grid=(bsz,),
        in_specs=[
            pl.BlockSpec((1, MEM_LEN, D_MODEL), lambda i: (i, 0, 0)),
            pl.BlockSpec((1, D_MODEL), lambda i: (0, 0)),
            pl.BlockSpec((1, D_MODEL), lambda i: (0, 0)),
            pl.BlockSpec((D_MODEL, 2 * D_MODEL), lambda i: (0, 0)),
        ],
        out_specs=[
            pl.BlockSpec((1, MEM_LEN, D_MODEL), lambda i: (i, 0, 0)),
            pl.BlockSpec((1, MEM_LEN, D_MODEL), lambda i: (i, 0, 0)),
        ],
        out_shape=[
            jax.ShapeDtypeStruct((bsz, MEM_LEN, D_MODEL), BF16),
            jax.ShapeDtypeStruct((bsz, MEM_LEN, D_MODEL), BF16),
        ],
        compiler_params=pltpu.CompilerParams(dimension_semantics=("arbitrary",)),
        name="kv",
    )(mem, g, b, wkv)


def _mix_kernel(x_ref, win_ref, bgate_ref, convw_ref, ws_ref, gbias_ref, glng_ref, glnb_ref,
                k_ref, v_ref, wc_ref, wg_ref, wx_ref, wo_ref, ln1g_ref, ln1b_ref,
                wr_pair_ref, wr_top_ref, br_ref, x1_ref, logit_ref, halo_ref):
    tm = x_ref.shape[1]
    d = D_MODEL
    x = x_ref[0]
    xb = x.astype(BF16)

    def proj(i):
        return _dot(xb, win_ref[:, i * d:(i + 1) * d])

    def gate(i):
        return _sigmoid(proj(6 + i) + bgate_ref[:, i * d:(i + 1) * d])

    @pl.when(pl.program_id(1) == 0)
    def _():
        halo_ref[...] = jnp.zeros((SUBLANES, d), F32)

    u = proj(1) * proj(2)
    halo = halo_ref[...]
    halo_ref[...] = u[tm - SUBLANES:tm, :]
    head_row = lax.broadcasted_iota(I32, (SUBLANES, d), 0)

    def shifted(k):
        body = pltpu.roll(u, k, axis=0)
        head = jnp.where(head_row < k, pltpu.roll(halo, k, axis=0), body[0:SUBLANES])
        return jnp.concatenate([head, body[SUBLANES:]], axis=0)

    conv = convw_ref[0:1, :] * shifted(2) + convw_ref[1:2, :] * shifted(1) + convw_ref[2:3, :] * u
    gu_lin = proj(3)
    gv_lin = proj(4)
    y_conv = (proj(0) * conv).astype(BF16)
    q = proj(5).astype(BF16)
    scores = []
    for h in range(XA_HEADS):
        sl = slice(h * XA_HEAD_DIM, (h + 1) * XA_HEAD_DIM)
        scores.append(lax.dot_general(q[:, sl], k_ref[0, :, sl], (((1,), (1,)), ((), ())),
                                      preferred_element_type=F32))
    g_conv = gate(0)
    g_gmlp = gate(1)
    merged = g_conv * _dot(y_conv, wc_ref[...])

    gu = _gelu_tanh(gu_lin)
    gv = _gelu_tanh(gv_lin)
    vn = _layer_norm(gv, glng_ref[...], glnb_ref[...]).astype(BF16)
    n_chunks = tm // CHUNK
    row_i = lax.broadcasted_iota(I32, (CHUNK, CHUNK), 0)
    col_i = lax.broadcasted_iota(I32, (CHUNK, CHUNK), 1)
    causal = col_i <= row_i
    f_cols = []
    for g in range(GMLP_GROUPS):
        w_g = jnp.where(causal, ws_ref[g], 0.0).astype(BF16)
        rhs = jnp.concatenate(
            [vn[c * CHUNK:(c + 1) * CHUNK, g * GROUP_CH:(g + 1) * GROUP_CH] for c in range(n_chunks)],
            axis=1)
        fg = _dot(w_g, rhs)
        f_cols.append(jnp.concatenate(
            [fg[:, c * GROUP_CH:(c + 1) * GROUP_CH] for c in range(n_chunks)], axis=0))
    f = jnp.concatenate(f_cols, axis=1)
    gbias = jnp.concatenate([gbias_ref[...]] * n_chunks, axis=0)
    y_gmlp = gu * (f + gbias)
    merged = merged + g_gmlp * _dot(y_gmlp.astype(BF16), wg_ref[...])

    heads = []
    for h in range(XA_HEADS):
        sl = slice(h * XA_HEAD_DIM, (h + 1) * XA_HEAD_DIM)
        s = scores[h] * np.float32(XA_HEAD_DIM ** -0.5)
        p = jnp.exp(s - jnp.max(s, axis=-1, keepdims=True))
        p = p * (1.0 / jnp.sum(p, axis=-1, keepdims=True))
        heads.append(_dot(p.astype(BF16), v_ref[0, :, sl]))
    y_xa = jnp.concatenate(heads, axis=1)
    merged = merged + gate(2) * _dot(y_xa.astype(BF16), wx_ref[...])

    mix = _dot(merged.astype(BF16), wo_ref[...])
    x1 = _layer_norm(np.float32(DEEPNORM_ALPHA) * x + mix, ln1g_ref[...], ln1b_ref[...])
    x1_ref[0] = x1

    x1h = x1.astype(BF16)
    x1l = (x1 - x1h.astype(F32)).astype(BF16)
    both = _dot(x1h, wr_pair_ref[...])
    logit_ref[0] = (both + pltpu.roll(both, LANES - N_EXPERTS, axis=1) + _dot(x1l, wr_top_ref[...])
                    + br_ref[...])


def _mix_call(x, win, bgate, convw, ws, gbias, glng, glnb, k, v, wc, wg, wx, wo, ln1g, ln1b,
              wr_pair, wr_top, br):
    bsz, seq, d = x.shape
    tm = min(TM_MIX, seq)
    assert seq % tm == 0 and tm % CHUNK == 0
    tile = lambda b, s: (b, s, 0)
    per_batch = lambda b, s: (b, 0, 0)
    return pl.pallas_call(
        _mix_kernel,
        grid=(bsz, seq // tm),
        in_specs=[
            pl.BlockSpec((1, tm, d), tile),
            _const_spec((d, W_IN_COLS)),
            _const_spec((1, 3 * d)),
            _const_spec((3, d)),
            _const_spec((GMLP_GROUPS, CHUNK, CHUNK)),
            _const_spec((CHUNK, d)),
            _const_spec((1, d)),
            _const_spec((1, d)),
            pl.BlockSpec((1, MEM_LEN, d), per_batch),
            pl.BlockSpec((1, MEM_LEN, d), per_batch),
            _const_spec((d, d)),
            _const_spec((d, d)),
            _const_spec((d, d)),
            _const_spec((d, d)),
            _const_spec((1, d)),
            _const_spec((1, d)),
            _const_spec((d, LANES)),
            _const_spec((d, LANES)),
            _const_spec((1, LANES)),
        ],
        out_specs=[pl.BlockSpec((1, tm, d), tile), pl.BlockSpec((1, tm, LANES), tile)],
        out_shape=[jax.ShapeDtypeStruct((bsz, seq, d), F32),
                   jax.ShapeDtypeStruct((bsz, seq, LANES), F32)],
        scratch_shapes=[pltpu.VMEM((SUBLANES, d), F32)],
        compiler_params=pltpu.CompilerParams(
            dimension_semantics=("arbitrary", "arbitrary"), vmem_limit_bytes=VMEM_LIMIT_BYTES),
        name="mix",
    )(x, win, bgate, convw, ws, gbias, glng, glnb, k, v, wc, wg, wx, wo, ln1g, ln1b, wr_pair, wr_top, br)


def _split3(v):
    hi = v.astype(BF16).astype(F32)
    mid = (v - hi).astype(BF16).astype(F32)
    return hi, mid, (v - hi - mid).astype(BF16).astype(F32)


def _route_kernel(logit_ref, pos_ref, gtab_ref, cnt_ref):
    tm = logit_ref.shape[0]
    work = logit_ref[...].T[0:N_EXPERTS, :]
    e_io = lax.broadcasted_iota(I32, (N_EXPERTS, tm), 0)
    neg = np.float32(-np.inf)
    vals, sels, eids = [], [], []
    for _ in range(TOP_K):
        m = jnp.max(work, axis=0, keepdims=True)
        idx = jnp.min(jnp.where(work == m, e_io, LANES), axis=0, keepdims=True)
        sel = e_io == idx
        vals.append(m)
        sels.append(sel)
        eids.append(idx.astype(F32))
        work = jnp.where(sel, neg, work)
    ex = [jnp.exp(vk - vals[0]) for vk in vals]
    inv = 1.0 / (ex[0] + ex[1] + ex[2] + ex[3])
    gates = [e * inv for e in ex]

    oh = jnp.where(sels[0] | sels[1] | sels[2] | sels[3], 1.0, 0.0).astype(F32)
    cnt = jnp.sum(oh, axis=1, keepdims=True)
    t_r = lax.broadcasted_iota(I32, (tm, tm), 0)
    t_c = lax.broadcasted_iota(I32, (tm, tm), 1)
    before = jnp.where(t_r < t_c, 1.0, 0.0).astype(BF16)
    rank = _dot(oh.astype(BF16), before)
    units = jnp.floor((cnt + np.float32(SEG_ALIGN - 1)) * np.float32(1.0 / SEG_ALIGN))
    units_b = jnp.concatenate(
        [jnp.broadcast_to(units, (N_EXPERTS, LANES)), jnp.zeros((LANES - N_EXPERTS, LANES), F32)],
        axis=0).astype(BF16)
    e_r = lax.broadcasted_iota(I32, (N_EXPERTS, LANES), 0)
    e_c = lax.broadcasted_iota(I32, (N_EXPERTS, LANES), 1)
    below = jnp.where(e_c < e_r, 1.0, 0.0).astype(BF16)
    seg = _dot(below, units_b)[:, 0:1] * np.float32(SEG_ALIGN)
    slot = seg + rank

    pos_rows = [jnp.sum(jnp.where(s, slot, 0.0), axis=0, keepdims=True) for s in sels]
    zero_row = jnp.zeros((1, tm), F32)
    pos_ref[0] = jnp.concatenate(pos_rows + [zero_row] * (SUBLANES - TOP_K), axis=0).astype(I32)
    parts = [p for g in gates for p in _split3(g)]
    table = jnp.concatenate(parts + eids + [jnp.zeros((LANES - len(parts) - TOP_K, tm), F32)], axis=0)
    gtab_ref[...] = table.T.astype(BF16)
    cnt_ref[0] = jnp.broadcast_to(cnt, (N_EXPERTS, LANES)).astype(I32)


def _route_call(logits2d):
    n_tok = logits2d.shape[0]
    nt = n_tok // TM_SORT
    return pl.pallas_call(
        _route_kernel,
        grid=(nt,),
        in_specs=[pl.BlockSpec((TM_SORT, LANES), lambda i: (i, 0))],
        out_specs=[
            pl.BlockSpec((1, SUBLANES, TM_SORT), lambda i: (i, 0, 0)),
            pl.BlockSpec((TM_SORT, LANES), lambda i: (i, 0)),
            pl.BlockSpec((1, N_EXPERTS, LANES), lambda i: (i, 0, 0)),
        ],
        out_shape=[
            jax.ShapeDtypeStruct((nt, SUBLANES, TM_SORT), I32),
            jax.ShapeDtypeStruct((n_tok, LANES), BF16),
            jax.ShapeDtypeStruct((nt, N_EXPERTS, LANES), I32),
        ],
        compiler_params=pltpu.CompilerParams(dimension_semantics=("arbitrary",)),
        name="route",
    )(logits2d)


def _chunk_rows(buf, i):
    return buf.at[pl.ds(i * SEG_ALIGN, SEG_ALIGN)]


def _sort_kernel(slot_ref, x1_ref, pos_ref, gtab_ref, xs_hbm, sbuf, sems):
    j = pl.program_id(0)
    nt = pl.num_programs(0)
    tm = x1_ref.shape[0]
    cur = lax.rem(j, 2)
    buf = sbuf.at[cur]
    j_io = lax.broadcasted_iota(I32, (L_SORT, tm), 0).astype(I16)
    pos = pos_ref[0].astype(I16)
    hit = j_io == pos[0:1, :]
    for k in range(1, TOP_K):
        hit = hit | (j_io == pos[k:k + 1, :])
    onehot = jnp.where(hit, jnp.ones((), BF16), jnp.zeros((), BF16))
    rows = _dot(onehot, jnp.concatenate([x1_ref[...].astype(BF16), gtab_ref[...]], axis=1))
    buf[:, 0:D_PACK] = _pack_pair(rows[:, 0:D_PACK], rows[:, D_PACK:D_MODEL])
    buf[:, D_PACK:XS_COLS] = lax.bitcast_convert_type(rows[:, D_MODEL:D_MODEL + LANES], I32)

    for i in range(N_CHUNKS):
        pltpu.make_async_copy(
            _chunk_rows(buf, i), xs_hbm.at[slot_ref[j * N_CHUNKS + i]], sems.at[cur]).start()

    def drain(which):
        for _ in range(N_CHUNKS):
            pltpu.make_async_copy(
                _chunk_rows(sbuf.at[which], 0), xs_hbm.at[0], sems.at[which]).wait()

    @pl.when(j > 0)
    def _():
        drain(1 - cur)

    @pl.when(j == nt - 1)
    def _():
        drain(cur)


def _sort_call(slots, x1, pos, gtab):
    n_tok = x1.shape[0]
    nt = n_tok // TM_SORT
    grid_spec = pltpu.PrefetchScalarGridSpec(
        num_scalar_prefetch=1,
        grid=(nt,),
        in_specs=[
            pl.BlockSpec((TM_SORT, D_MODEL), lambda i, *_: (i, 0)),
            pl.BlockSpec((1, SUBLANES, TM_SORT), lambda i, *_: (i, 0, 0)),
            pl.BlockSpec((TM_SORT, LANES), lambda i, *_: (i, 0)),
        ],
        out_specs=pl.BlockSpec(memory_space=pl.ANY),
        scratch_shapes=[
            pltpu.VMEM((2, L_SORT, XS_COLS), I32),
            pltpu.SemaphoreType.DMA((2,)),
        ],
    )
    return pl.pallas_call(
        _sort_kernel,
        grid_spec=grid_spec,
        out_shape=jax.ShapeDtypeStruct((nt * N_CHUNKS, SEG_ALIGN, XS_COLS), I32),
        compiler_params=pltpu.CompilerParams(
            dimension_semantics=("arbitrary",), vmem_limit_bytes=VMEM_LIMIT_BYTES),
        name="sort",
    )(slots, x1, pos, gtab)


KIND_NOOP, KIND_FFN, KIND_ZERO = 0, 1, 2


def _expert_kernel(tile_ref, exp_ref, kind_ref, lo_ref, hi_ref, first_ref, fresh_ref, wslot_ref,
                   next_ref, xs_ref, wgu_hbm, bgu_ref, wd_hbm, bd_ref, out_ref,
                   wgu_b, wd_b, stage_gu, stage_d, wsems):
    w = pl.program_id(0)
    kind = kind_ref[w]

    @pl.when(kind == KIND_ZERO)
    def _():
        out_ref[...] = jnp.zeros(out_ref.shape, I32)

    def weight_copies(e, s):
        return (pltpu.make_async_copy(wgu_hbm.at[e], stage_gu.at[s], wsems.at[0, s]),
                pltpu.make_async_copy(wd_hbm.at[e], stage_d.at[s], wsems.at[1, s]))

    @pl.when((kind == KIND_FFN) & (fresh_ref[w] == 1))
    def _():
        e = exp_ref[w]
        s = wslot_ref[w]

        @pl.when(w == 0)
        def _():
            for cp in weight_copies(e, s):
                cp.start()

        for cp in weight_copies(e, s):
            cp.wait()
        wgu_b[...] = stage_gu[s].astype(BF16)
        wd_b[...] = stage_d[s].astype(BF16)

        @pl.when(next_ref[w] >= 0)
        def _():
            for cp in weight_copies(next_ref[w], 1 - s):
                cp.start()

    lo = lo_ref[w]
    hi = hi_ref[w]
    first = first_ref[w]
    is_ffn = kind == KIND_FFN
    whole = (lo == 0) & (hi == TR)

    def ffn(r0, nr):
        xs = _unpack_bf16(xs_ref[r0:r0 + nr, 0:D_PACK])
        gtab = lax.bitcast_convert_type(xs_ref[r0:r0 + nr, D_PACK:XS_COLS], F32)
        this_e = exp_ref[w].astype(F32)
        gate = jnp.zeros((nr, 1), F32)
        for k in range(TOP_K):
            g_k = gtab[:, 3 * k:3 * k + 1] + gtab[:, 3 * k + 1:3 * k + 2] + gtab[:, 3 * k + 2:3 * k + 3]
            gate = gate + jnp.where(gtab[:, GTAB_EID + k:GTAB_EID + k + 1] == this_e, g_k, 0.0)
        gu = _dot(xs, wgu_b[...]) + bgu_ref[0]
        g_lin = jnp.minimum(gu[:, :D_FF], np.float32(SWIGLU_LIMIT))
        u_lin = jnp.clip(gu[:, D_FF:], np.float32(-SWIGLU_LIMIT), np.float32(SWIGLU_LIMIT))
        act = (u_lin + 1.0) * (g_lin * _sigmoid(np.float32(SWIGLU_ALPHA) * g_lin))
        eo = (_dot(act.astype(BF16), wd_b[...]) + bd_ref[0]) * gate
        eo = eo.astype(BF16).astype(F32)
        return _pack_pair(eo[:, 0:D_PACK], eo[:, D_PACK:D_MODEL])

    @pl.when(is_ffn & whole)
    def _():
        out_ref[...] = ffn(0, TR)

    for r0 in range(0, TR, TR_SUB):
        touched = (lo < r0 + TR_SUB) & (hi > r0)
        part = is_ffn & jnp.logical_not(whole)

        @pl.when(part & touched)
        def _(r0=r0):
            eo = ffn(r0, TR_SUB)
            rows = r0 + lax.broadcasted_iota(I32, (TR_SUB, 1), 0)
            mine = (rows >= lo) & (rows < hi)

            @pl.when(first == 1)
            def _():
                out_ref[r0:r0 + TR_SUB, :] = jnp.where(mine, eo, 0)

            @pl.when(first == 0)
            def _():
                out_ref[r0:r0 + TR_SUB, :] = jnp.where(mine, eo, out_ref[r0:r0 + TR_SUB, :])

        @pl.when(part & jnp.logical_not(touched) & (first == 1))
        def _(r0=r0):
            out_ref[r0:r0 + TR_SUB, :] = jnp.zeros((TR_SUB, D_PACK), I32)


def _expert_call(md, xs, wgu, bgu, wd, bd):
    pmax = xs.shape[0]
    n_items = md["tile"].shape[0]
    grid_spec = pltpu.PrefetchScalarGridSpec(
        num_scalar_prefetch=9,
        grid=(n_items,),
        in_specs=[
            pl.BlockSpec((TR, XS_COLS), lambda w, t, e, *_: (t[w], 0)),
            pl.BlockSpec(memory_space=pl.ANY),
            pl.BlockSpec((1, 1, 2 * D_FF), lambda w, t, e, *_: (e[w], 0, 0)),
            pl.BlockSpec(memory_space=pl.ANY),
            pl.BlockSpec((1, 1, D_MODEL), lambda w, t, e, *_: (e[w], 0, 0)),
        ],
        out_specs=pl.BlockSpec((TR, D_PACK), lambda w, t, e, *_: (t[w], 0)),
        scratch_shapes=[
            pltpu.VMEM((D_MODEL, 2 * D_FF), BF16),
            pltpu.VMEM((D_FF, D_MODEL), BF16),
            pltpu.VMEM((2, D_MODEL, 2 * D_FF), F32),
            pltpu.VMEM((2, D_FF, D_MODEL), F32),
            pltpu.SemaphoreType.DMA((2, 2)),
        ],
    )
    return pl.pallas_call(
        _expert_kernel,
        grid_spec=grid_spec,
        out_shape=jax.ShapeDtypeStruct((pmax, D_PACK), I32),
        compiler_params=pltpu.CompilerParams(
            dimension_semantics=("arbitrary",), vmem_limit_bytes=VMEM_LIMIT_BYTES),
        name="experts",
    )(md["tile"], md["exp"], md["kind"], md["lo"], md["hi"], md["first"], md["fresh"],
      md["wslot"], md["next"], xs, wgu, bgu, wd, bd)


def _combine_kernel(slot_ref, x1_ref, post_ref, g_ref, b_ref, eo_hbm, out_ref, ebuf, sems):
    j = pl.program_id(0)
    nt = pl.num_programs(0)
    tm = x1_ref.shape[0]
    cur = lax.rem(j, 2)

    def fetch(tile, which):
        for i in range(N_CHUNKS):
            pltpu.make_async_copy(eo_hbm.at[slot_ref[tile * N_CHUNKS + i]],
                                  _chunk_rows(ebuf.at[which], i), sems.at[which]).start()

    @pl.when(j == 0)
    def _():
        fetch(0, 0)

    @pl.when(j + 1 < nt)
    def _():
        fetch(j + 1, 1 - cur)

    for _ in range(N_CHUNKS):
        pltpu.make_async_copy(eo_hbm.at[0], _chunk_rows(ebuf.at[cur], 0), sems.at[cur]).wait()

    l_io = lax.broadcasted_iota(I32, (tm, L_SORT), 1).astype(I16)
    post = post_ref[0].astype(I16)
    hit = l_io == post[:, 0:1]
    for k in range(1, TOP_K):
        hit = hit | (l_io == post[:, k:k + 1])
    onehot = jnp.where(hit, jnp.ones((), BF16), jnp.zeros((), BF16))
    y = _dot(onehot, _unpack_bf16(ebuf[cur]))
    out_ref[...] = _layer_norm(np.float32(DEEPNORM_ALPHA) * x1_ref[...] + y, g_ref[...], b_ref[...])


def _combine_call(slots, x1, post, g, b, eo):
    n_tok = x1.shape[0]
    nt = n_tok // TM_SORT
    grid_spec = pltpu.PrefetchScalarGridSpec(
        num_scalar_prefetch=1,
        grid=(nt,),
        in_specs=[
            pl.BlockSpec((TM_SORT, D_MODEL), lambda i, *_: (i, 0)),
            pl.BlockSpec((1, TM_SORT, SUBLANES), lambda i, *_: (i, 0, 0)),
            pl.BlockSpec((1, D_MODEL), lambda i, *_: (0, 0)),
            pl.BlockSpec((1, D_MODEL), lambda i, *_: (0, 0)),
            pl.BlockSpec(memory_space=pl.ANY),
        ],
        out_specs=pl.BlockSpec((TM_SORT, D_MODEL), lambda i, *_: (i, 0)),
        scratch_shapes=[pltpu.VMEM((2, L_SORT, D_PACK), I32), pltpu.SemaphoreType.DMA((2,))],
    )
    return pl.pallas_call(
        _combine_kernel,
        grid_spec=grid_spec,
        out_shape=jax.ShapeDtypeStruct((n_tok, D_MODEL), F32),
        compiler_params=pltpu.CompilerParams(
            dimension_semantics=("arbitrary",), vmem_limit_bytes=VMEM_LIMIT_BYTES),
        name="combine",
    )(slots, x1, post, g, b, eo)


def _excl_cumsum(a, axis):
    n = a.shape[axis]
    lower = (jnp.arange(n)[:, None] > jnp.arange(n)[None, :]).astype(F32)
    af = jnp.moveaxis(a.astype(F32), axis, 0).reshape(n, -1)
    out = jnp.dot(lower, af, precision=lax.Precision.HIGHEST)
    out = out.reshape((n,) + tuple(np.delete(np.array(a.shape), axis)))
    return jnp.moveaxis(out, 0, axis).astype(I32)


def _routing_metadata(cnt):
    nt = cnt.shape[0]
    c8 = (cnt + (SEG_ALIGN - 1)) // SEG_ALIGN * SEG_ALIGN
    seg = _excl_cumsum(c8, 1)
    tot8 = jnp.sum(c8, axis=1)
    n8 = jnp.sum(c8, axis=0)
    n_al = (n8 + (TR_SUB - 1)) // TR_SUB * TR_SUB
    roomy = jnp.sum(n_al) <= nt * L_SORT
    span = jnp.where(roomy, n_al, n8)
    base = _excl_cumsum(span, 0)
    goff = base[None, :] + _excl_cumsum(c8, 0)
    p_used = jnp.sum(span)
    gap = (span - n8) // SEG_ALIGN
    gap_start = _excl_cumsum(gap, 0)
    gap_end = gap_start + gap
    n_gap = jnp.sum(gap)

    r = (jnp.arange(N_CHUNKS, dtype=I32) * SEG_ALIGN)[None, :, None]
    in_seg = (r >= seg[:, None, :]) & (r < (seg + c8)[:, None, :])
    real_slot = jnp.sum(jnp.where(in_seg, goff[:, None, :] + r - seg[:, None, :], 0), axis=2) // SEG_ALIGN
    used = tot8 // SEG_ALIGN
    n_empty = N_CHUNKS - used
    ci = jnp.arange(N_CHUNKS, dtype=I32)[None, :]
    m = (_excl_cumsum(n_empty, 0)[:, None] + (ci - used[:, None]))[:, :, None]
    in_gap = (m >= gap_start[None, None, :]) & (m < gap_end[None, None, :])
    gap_slot = jnp.sum(jnp.where(in_gap, ((base + n8) // SEG_ALIGN - gap_start)[None, None, :] + m, 0),
                       axis=2)
    m = m[:, :, 0]
    empty_slot = jnp.where(m < n_gap, gap_slot, p_used // SEG_ALIGN + (m - n_gap))
    slots = jnp.where(ci < used[:, None], real_slot, empty_slot).reshape(-1).astype(I32)

    assert (nt * L_SORT) % TR == 0
    n_tiles = nt * L_SORT // TR
    n_items = n_tiles + N_EXPERTS
    e_lo, e_hi = base, base + n8
    t_first = e_lo // TR
    t_end = (e_hi + TR - 1) // TR
    w_e = jnp.where(n8 > 0, t_end - t_first, 0)
    w_start = _excl_cumsum(w_e, 0)
    w_cum = w_start + w_e
    w_total = jnp.sum(w_e)
    tiles_used = (p_used + TR - 1) // TR

    w = jnp.arange(n_items, dtype=I32)
    e_ids = jnp.arange(N_EXPERTS, dtype=I32)
    e_w = jnp.minimum(jnp.sum((w[:, None] >= w_cum[None, :]).astype(I32), axis=1), N_EXPERTS - 1)
    is_ffn = w < w_total
    last_e = jnp.max(jnp.where(n8 > 0, e_ids, 0))
    exp = jnp.where(is_ffn, e_w, last_e).astype(I32)
    pick = exp[:, None] == e_ids[None, :]
    look = lambda table: jnp.sum(jnp.where(pick, table[None, :], 0), axis=1)
    ffn_tile = look(t_first) + (w - look(w_start))
    zero_tile = tiles_used + (w - w_total)
    is_zero = (~is_ffn) & (zero_tile < n_tiles)
    tile = jnp.where(is_ffn, ffn_tile, jnp.where(is_zero, zero_tile, n_tiles - 1)).astype(I32)
    kind = jnp.where(is_ffn, KIND_FFN, jnp.where(is_zero, KIND_ZERO, KIND_NOOP)).astype(I32)
    lo = jnp.clip(look(e_lo) - tile * TR, 0, TR).astype(I32)
    hi = jnp.clip(look(e_hi) - tile * TR, 0, TR).astype(I32)
    prev_tile = jnp.concatenate([jnp.full((1,), -1, I32), tile[:-1]])
    first = (tile != prev_tile).astype(I32)
    prev_exp = jnp.concatenate([jnp.full((1,), -1, I32), exp[:-1]])
    fresh = (exp != prev_exp).astype(I32)
    used_e = n8 > 0
    later = (e_ids[None, :] > e_ids[:, None]) & used_e[None, :]
    order_e = jnp.sum(((e_ids[None, :] < e_ids[:, None]) & used_e[None, :]).astype(I32), axis=1)
    next_e = jnp.min(jnp.where(later, e_ids[None, :], N_EXPERTS), axis=1)
    next_e = jnp.where(next_e == N_EXPERTS, -1, next_e)
    wslot = (look(order_e) % 2).astype(I32)
    nxt = look(next_e).astype(I32)
    return dict(slots=slots, tile=tile, exp=exp, kind=kind, lo=lo, hi=hi, first=first, fresh=fresh,
                wslot=wslot, next=nxt)


def kernel(x, mem, w_in, b_gate, conv_w, gmlp_ws, gmlp_b, gmlp_ln_g, gmlp_ln_b, mem_ln_g,
           mem_ln_b, w_kv, w_conv_proj, w_gmlp_proj, w_xa_proj, w_out, ln1_g, ln1_b, w_router,
           b_router, w_gate_up, b_gate_up, w_down, b_down, ln2_g, ln2_b):
    bsz, seq, d = x.shape
    n_tok = bsz * seq
    assert d == D_MODEL and n_tok % TM_SORT == 0
    depth = w_in.shape[0]
    for l in range(depth):
        row = lambda a: a[l].reshape(1, -1)
        k, v = _kv_call(mem, row(mem_ln_g), row(mem_ln_b), w_kv[l].astype(BF16))
        gbias = jnp.repeat(gmlp_b[l].T, GROUP_CH, axis=1)
        wr_hi = w_router[l].astype(BF16)
        wr_lo = (w_router[l] - wr_hi.astype(F32)).astype(BF16)
        wr_pair = jnp.pad(jnp.concatenate([wr_hi, wr_lo], axis=1), ((0, 0), (0, LANES - 2 * N_EXPERTS)))
        wr_top = jnp.pad(wr_hi, ((0, 0), (0, LANES - N_EXPERTS)))
        br = jnp.pad(b_router[l], (0, LANES - N_EXPERTS)).reshape(1, LANES)
        x1, logits = _mix_call(
            x, w_in[l].astype(BF16), row(b_gate), conv_w[l], gmlp_ws[l], gbias,
            row(gmlp_ln_g), row(gmlp_ln_b), k, v, w_conv_proj[l].astype(BF16),
            w_gmlp_proj[l].astype(BF16), w_xa_proj[l].astype(BF16), w_out[l].astype(BF16),
            row(ln1_g), row(ln1_b), wr_pair, wr_top, br)
        x1 = x1.reshape(n_tok, d)
        pos, gtab, cnt = _route_call(logits.reshape(n_tok, LANES))
        md = _routing_metadata(cnt[:, :, 0])
        xs = _sort_call(md["slots"], x1, pos, gtab)
        eo = _expert_call(md, xs.reshape(-1, XS_COLS), w_gate_up[l], b_gate_up[l][:, None, :],
                          w_down[l], b_down[l][:, None, :])
        post = jnp.transpose(pos, (0, 2, 1))
        x = _combine_call(md["slots"], x1, post, row(ln2_g), row(ln2_b),
                          eo.reshape(-1, SEG_ALIGN, D_PACK)).reshape(bsz, seq, d)
    return x
```

```python
import jax
import jax.numpy as jnp
import numpy as np
from jax import lax
from jax.experimental import pallas as pl
from jax.experimental.pallas import tpu as pltpu

F32 = jnp.float32
BF16 = jnp.bfloat16
I32 = jnp.int32
I16 = jnp.int16

D_MODEL = 1024
CHUNK = 128
GMLP_GROUPS = 8
GROUP_CH = D_MODEL // GMLP_GROUPS
MEM_LEN = 256
XA_HEADS = 4
XA_HEAD_DIM = D_MODEL // XA_HEADS
N_EXPERTS = 32
TOP_K = 4
D_FF = D_MODEL
W_IN_COLS = 9 * D_MODEL
SWIGLU_LIMIT = 7.0
SWIGLU_ALPHA = 1.702
LN_EPS = 1e-5
DEEPNORM_ALPHA = 2.0 ** 0.25

LANES = 128
SUBLANES = 8
VMEM_LIMIT_BYTES = 60000 * 1024

TM_MIX = 512
TM_SORT = 512
SEG_ALIGN = SUBLANES
MXU_DIM = 256
L_SORT = -(-(TOP_K * TM_SORT + N_EXPERTS * (SEG_ALIGN - 1)) // MXU_DIM) * MXU_DIM
assert L_SORT >= TOP_K * TM_SORT + N_EXPERTS * (SEG_ALIGN - 1) and L_SORT % MXU_DIM == 0
assert L_SORT < 2 ** 15
N_CHUNKS = L_SORT // SEG_ALIGN
SORT_PARTS = 4
assert N_CHUNKS % SORT_PARTS == 0
D_PACK = D_MODEL // 2
XS_COLS = D_PACK + LANES
GTAB_EID = 3 * TOP_K
TR = 512
TR_SUB = 256
assert TR % TR_SUB == 0


BF16_BITS = 16
TOP_HALF = np.int32(-(1 << BF16_BITS))


def _pack_pair(lo, hi):
    lo_bits = lax.shift_right_logical(lax.bitcast_convert_type(lo, I32), BF16_BITS)
    return lo_bits | lax.bitcast_convert_type(hi, I32)


def _unpack_bf16(words):
    lo = lax.bitcast_convert_type(lax.shift_left(words, BF16_BITS), F32)
    hi = lax.bitcast_convert_type(words & TOP_HALF, F32)
    return jnp.concatenate([lo.astype(BF16), hi.astype(BF16)], axis=1)


def _layer_norm(x, g, b):
    mu = jnp.mean(x, axis=-1, keepdims=True)
    xc = x - mu
    var = jnp.mean(xc * xc, axis=-1, keepdims=True)
    return xc * lax.rsqrt(var + LN_EPS) * g + b


def _gelu_tanh(x):
    c = np.float32(np.sqrt(2.0 / np.pi))
    ca = np.float32(np.sqrt(2.0 / np.pi) * 0.044715)
    half = 0.5 * x
    return half + half * jnp.tanh(x * (c + ca * (x * x)))


def _sigmoid(x):
    return 1.0 / (1.0 + jnp.exp(-x))


def _dot(a, b):
    return jnp.dot(a, b, preferred_element_type=F32)


def _const_spec(shape):
    n = len(shape)
    return pl.BlockSpec(shape, lambda *_: (0,) * n, pipeline_mode=pl.Buffered(1))


def _kv_kernel(mem_ref, g_ref, b_ref, wkv_ref, k_ref, v_ref):
    mem_n = _layer_norm(mem_ref[0], g_ref[...], b_ref[...])
    kv = _dot(mem_n.astype(BF16), wkv_ref[...])
    k_ref[0] = kv[:, :D_MODEL].astype(BF16)
    v_ref[0] = kv[:, D_MODEL:].astype(BF16)


def _kv_call(mem, g, b, wkv):
    bsz = mem.shape[0]
    return pl.pallas_call(
        _kv_kernel,
        grid=(bsz,),
        in_specs=[
            pl.BlockSpec((1, MEM_LEN, D_MODEL), lambda i: (i, 0, 0)),
            pl.BlockSpec((1, D_MODEL), lambda i: (0, 0)),
            pl.BlockSpec((1, D_MODEL), lambda i: (0, 0)),
            pl.BlockSpec((D_MODEL, 2 * D_MODEL), lambda i: (0, 0)),
        ],
        out_specs=[
            pl.BlockSpec((1, MEM_LEN, D_MODEL), lambda i: (i, 0, 0)),
            pl.BlockSpec((1, MEM_LEN, D_MODEL), lambda i: (i, 0, 0)),
        ],
        out_shape=[
            jax.ShapeDtypeStruct((bsz, MEM_LEN, D_MODEL), BF16),
            jax.ShapeDtypeStruct((bsz, MEM_LEN, D_MODEL), BF16),
        ],
        compiler_params=pltpu.CompilerParams(dimension_semantics=("arbitrary",)),
        name="kv",
    )(mem, g, b, wkv)


def _mix_kernel(x_ref, win_ref, bgate_ref, convw_ref, ws_ref, gbias_ref, glng_ref, glnb_ref,
                k_ref, v_ref, wc_ref, wg_ref, wx_ref, wo_ref, ln1g_ref, ln1b_ref,
                wr_pair_ref, wr_top_ref, br_ref, x1_ref, logit_ref, halo_ref):
    tm = x_ref.shape[1]
    d = D_MODEL
    x = x_ref[0]
    xb = x.astype(BF16)

    def proj(i):
        return _dot(xb, win_ref[:, i * d:(i + 1) * d])

    def gate(i):
        return _sigmoid(proj(6 + i) + bgate_ref[:, i * d:(i + 1) * d])

    @pl.when(pl.program_id(1) == 0)
    def _():
        halo_ref[...] = jnp.zeros((SUBLANES, d), F32)

    u = proj(1) * proj(2)
    halo = halo_ref[...]
    halo_ref[...] = u[tm - SUBLANES:tm, :]
    head_row = lax.broadcasted_iota(I32, (SUBLANES, d), 0)

    def shifted(k):
        body = pltpu.roll(u, k, axis=0)
        head = jnp.where(head_row < k, pltpu.roll(halo, k, axis=0), body[0:SUBLANES])
        return jnp.concatenate([head, body[SUBLANES:]], axis=0)

    conv = convw_ref[0:1, :] * shifted(2) + convw_ref[1:2, :] * shifted(1) + convw_ref[2:3, :] * u
    gu_lin = proj(3)
    gv_lin = proj(4)
    y_conv = (proj(0) * conv).astype(BF16)
    q = proj(5).astype(BF16)
    scores = []
    for h in range(XA_HEADS):
        sl = slice(h * XA_HEAD_DIM, (h + 1) * XA_HEAD_DIM)
        scores.append(lax.dot_general(q[:, sl], k_ref[0, :, sl], (((1,), (1,)), ((), ())),
                                      preferred_element_type=F32))
    g_conv = gate(0)
    g_gmlp = gate(1)
    merged = g_conv * _dot(y_conv, wc_ref[...])

    gu = _gelu_tanh(gu_lin)
    gv = _gelu_tanh(gv_lin)
    vn = _layer_norm(gv, glng_ref[...], glnb_ref[...]).astype(BF16)
    n_chunks = tm // CHUNK
    row_i = lax.broadcasted_iota(I32, (CHUNK, CHUNK), 0)
    col_i = lax.broadcasted_iota(I32, (CHUNK, CHUNK), 1)
    causal = col_i <= row_i
    f_cols = []
    for g in range(GMLP_GROUPS):
        w_g = jnp.where(causal, ws_ref[g], 0.0).astype(BF16)
        rhs = jnp.concatenate(
            [vn[c * CHUNK:(c + 1) * CHUNK, g * GROUP_CH:(g + 1) * GROUP_CH] for c in range(n_chunks)],
            axis=1)
        fg = _dot(w_g, rhs)
        f_cols.append(jnp.concatenate(
            [fg[:, c * GROUP_CH:(c + 1) * GROUP_CH] for c in range(n_chunks)], axis=0))
    f = jnp.concatenate(f_cols, axis=1)
    gbias = jnp.concatenate([gbias_ref[...]] * n_chunks, axis=0)
    y_gmlp = gu * (f + gbias)
    merged = merged + g_gmlp * _dot(y_gmlp.astype(BF16), wg_ref[...])

    heads = []
    for h in range(XA_HEADS):
        sl = slice(h * XA_HEAD_DIM, (h + 1) * XA_HEAD_DIM)
        s = scores[h] * np.float32(XA_HEAD_DIM ** -0.5)
        p = jnp.exp(s - jnp.max(s, axis=-1, keepdims=True))
        p = p * (1.0 / jnp.sum(p, axis=-1, keepdims=True))
        heads.append(_dot(p.astype(BF16), v_ref[0, :, sl]))
    y_xa = jnp.concatenate(heads, axis=1)
    merged = merged + gate(2) * _dot(y_xa.astype(BF16), wx_ref[...])

    mix = _dot(merged.astype(BF16), wo_ref[...])
    x1 = _layer_norm(np.float32(DEEPNORM_ALPHA) * x + mix, ln1g_ref[...], ln1b_ref[...])
    x1_ref[0] = x1

    x1h = x1.astype(BF16)
    x1l = (x1 - x1h.astype(F32)).astype(BF16)
    both = _dot(x1h, wr_pair_ref[...])
    logit_ref[0] = (both + pltpu.roll(both, LANES - N_EXPERTS, axis=1) + _dot(x1l, wr_top_ref[...])
                    + br_ref[...])


def _mix_call(x, win, bgate, convw, ws, gbias, glng, glnb, k, v, wc, wg, wx, wo, ln1g, ln1b,
              wr_pair, wr_top, br):
    bsz, seq, d = x.shape
    tm = min(TM_MIX, seq)
    assert seq % tm == 0 and tm % CHUNK == 0
    tile = lambda b, s: (b, s, 0)
    per_batch = lambda b, s: (b, 0, 0)
    return pl.pallas_call(
        _mix_kernel,
        grid=(bsz, seq // tm),
        in_specs=[
            pl.BlockSpec((1, tm, d), tile),
            _const_spec((d, W_IN_COLS)),
            _const_spec((1, 3 * d)),
            _const_spec((3, d)),
            _const_spec((GMLP_GROUPS, CHUNK, CHUNK)),
            _const_spec((CHUNK, d)),
            _const_spec((1, d)),
            _const_spec((1, d)),
            pl.BlockSpec((1, MEM_LEN, d), per_batch),
            pl.BlockSpec((1, MEM_LEN, d), per_batch),
            _const_spec((d, d)),
            _const_spec((d, d)),
            _const_spec((d, d)),
            _const_spec((d, d)),
            _const_spec((1, d)),
            _const_spec((1, d)),
            _const_spec((d, LANES)),
            _const_spec((d, LANES)),
            _const_spec((1, LANES)),
        ],
        out_specs=[pl.BlockSpec((1, tm, d), tile), pl.BlockSpec((1, tm, LANES), tile)],
        out_shape=[jax.ShapeDtypeStruct((bsz, seq, d), F32),
                   jax.ShapeDtypeStruct((bsz, seq, LANES), F32)],
        scratch_shapes=[pltpu.VMEM((SUBLANES, d), F32)],
        compiler_params=pltpu.CompilerParams(
            dimension_semantics=("arbitrary", "arbitrary"), vmem_limit_bytes=VMEM_LIMIT_BYTES),
        name="mix",
    )(x, win, bgate, convw, ws, gbias, glng, glnb, k, v, wc, wg, wx, wo, ln1g, ln1b, wr_pair, wr_top, br)


def _split3(v):
    hi = v.astype(BF16).astype(F32)
    mid = (v - hi).astype(BF16).astype(F32)
    return hi, mid, (v - hi - mid).astype(BF16).astype(F32)


def _route_kernel(logit_ref, pos_ref, gtab_ref, cnt_ref):
    tm = logit_ref.shape[0]
    work = logit_ref[...].T[0:N_EXPERTS, :]
    e_io = lax.broadcasted_iota(I32, (N_EXPERTS, tm), 0)
    neg = np.float32(-np.inf)
    vals, sels, eids = [], [], []
    for _ in range(TOP_K):
        m = jnp.max(work, axis=0, keepdims=True)
        idx = jnp.min(jnp.where(work == m, e_io, LANES), axis=0, keepdims=True)
        sel = e_io == idx
        vals.append(m)
        sels.append(sel)
        eids.append(idx.astype(F32))
        work = jnp.where(sel, neg, work)
    ex = [jnp.exp(vk - vals[0]) for vk in vals]
    inv = 1.0 / (ex[0] + ex[1] + ex[2] + ex[3])
    gates = [e * inv for e in ex]

    oh = jnp.where(sels[0] | sels[1] | sels[2] | sels[3], 1.0, 0.0).astype(F32)
    cnt = jnp.sum(oh, axis=1, keepdims=True)
    t_r = lax.broadcasted_iota(I32, (tm, tm), 0)
    t_c = lax.broadcasted_iota(I32, (tm, tm), 1)
    before = jnp.where(t_r < t_c, 1.0, 0.0).astype(BF16)
    rank = _dot(oh.astype(BF16), before)
    units = jnp.floor((cnt + np.float32(SEG_ALIGN - 1)) * np.float32(1.0 / SEG_ALIGN))
    units_b = jnp.concatenate(
        [jnp.broadcast_to(units, (N_EXPERTS, LANES)), jnp.zeros((LANES - N_EXPERTS, LANES), F32)],
        axis=0).astype(BF16)
    e_r = lax.broadcasted_iota(I32, (N_EXPERTS, LANES), 0)
    e_c = lax.broadcasted_iota(I32, (N_EXPERTS, LANES), 1)
    below = jnp.where(e_c < e_r, 1.0, 0.0).astype(BF16)
    seg = _dot(below, units_b)[:, 0:1] * np.float32(SEG_ALIGN)
    slot = seg + rank

    pos_rows = [jnp.sum(jnp.where(s, slot, 0.0), axis=0, keepdims=True) for s in sels]
    zero_row = jnp.zeros((1, tm), F32)
    pos_ref[0] = jnp.concatenate(pos_rows + [zero_row] * (SUBLANES - TOP_K), axis=0).astype(I32)
    parts = [p for g in gates for p in _split3(g)]
    table = jnp.concatenate(parts + eids + [jnp.zeros((LANES - len(parts) - TOP_K, tm), F32)], axis=0)
    gtab_ref[...] = table.T.astype(BF16)
    cnt_ref[0] = jnp.broadcast_to(cnt, (N_EXPERTS, LANES)).astype(I32)


def _route_call(logits2d):
    n_tok = logits2d.shape[0]
    nt = n_tok // TM_SORT
    return pl.pallas_call(
        _route_kernel,
        grid=(nt,),
        in_specs=[pl.BlockSpec((TM_SORT, LANES), lambda i: (i, 0))],
        out_specs=[
            pl.BlockSpec((1, SUBLANES, TM_SORT), lambda i: (i, 0, 0)),
            pl.BlockSpec((TM_SORT, LANES), lambda i: (i, 0)),
            pl.BlockSpec((1, N_EXPERTS, LANES), lambda i: (i, 0, 0)),
        ],
        out_shape=[
            jax.ShapeDtypeStruct((nt, SUBLANES, TM_SORT), I32),
            jax.ShapeDtypeStruct((n_tok, LANES), BF16),
            jax.ShapeDtypeStruct((nt, N_EXPERTS, LANES), I32),
        ],
        compiler_params=pltpu.CompilerParams(dimension_semantics=("arbitrary",)),
        name="route",
    )(logits2d)


def _chunk_rows(buf, i):
    return buf.at[pl.ds(i * SEG_ALIGN, SEG_ALIGN)]


def _sort_kernel(slot_ref, x1_ref, pos_ref, gtab_ref, xs_hbm, sbuf, sems):
    j = pl.program_id(0)
    nt = pl.num_programs(0)
    tm = x1_ref.shape[0]
    cur = lax.rem(j, 2)
    buf = sbuf.at[cur]
    pos = pos_ref[0].astype(I16)
    rhs = jnp.concatenate([x1_ref[...].astype(BF16), gtab_ref[...]], axis=1)
    rows_per_part = L_SORT // SORT_PARTS
    chunks_per_part = N_CHUNKS // SORT_PARTS
    for part in range(SORT_PARTS):
        r0 = part * rows_per_part
        j_io = (lax.broadcasted_iota(I32, (rows_per_part, tm), 0) + r0).astype(I16)
        hit = j_io == pos[0:1, :]
        for k in range(1, TOP_K):
            hit = hit | (j_io == pos[k:k + 1, :])
        onehot = jnp.where(hit, jnp.ones((), BF16), jnp.zeros((), BF16))
        rows = _dot(onehot, rhs)
        buf[r0:r0 + rows_per_part, 0:D_PACK] = _pack_pair(rows[:, 0:D_PACK], rows[:, D_PACK:D_MODEL])
        buf[r0:r0 + rows_per_part, D_PACK:XS_COLS] = lax.bitcast_convert_type(
            rows[:, D_MODEL:D_MODEL + LANES], I32)
        for i in range(part * chunks_per_part, (part + 1) * chunks_per_part):
            pltpu.make_async_copy(
                _chunk_rows(buf, i), xs_hbm.at[slot_ref[j * N_CHUNKS + i]], sems.at[cur]).start()

    def drain(which):
        for _ in range(N_CHUNKS):
            pltpu.make_async_copy(
                _chunk_rows(sbuf.at[which], 0), xs_hbm.at[0], sems.at[which]).wait()

    @pl.when(j > 0)
    def _():
        drain(1 - cur)

    @pl.when(j == nt - 1)
    def _():
        drain(cur)


def _sort_call(slots, x1, pos, gtab):
    n_tok = x1.shape[0]
    nt = n_tok // TM_SORT
    grid_spec = pltpu.PrefetchScalarGridSpec(
        num_scalar_prefetch=1,
        grid=(nt,),
        in_specs=[
            pl.BlockSpec((TM_SORT, D_MODEL), lambda i, *_: (i, 0)),
            pl.BlockSpec((1, SUBLANES, TM_SORT), lambda i, *_: (i, 0, 0)),
            pl.BlockSpec((TM_SORT, LANES), lambda i, *_: (i, 0)),
        ],
        out_specs=pl.BlockSpec(memory_space=pl.ANY),
        scratch_shapes=[
            pltpu.VMEM((2, L_SORT, XS_COLS), I32),
            pltpu.SemaphoreType.DMA((2,)),
        ],
    )
    return pl.pallas_call(
        _sort_kernel,
        grid_spec=grid_spec,
        out_shape=jax.ShapeDtypeStruct((nt * N_CHUNKS, SEG_ALIGN, XS_COLS), I32),
        compiler_params=pltpu.CompilerParams(
            dimension_semantics=("arbitrary",), vmem_limit_bytes=VMEM_LIMIT_BYTES),
        name="sort",
    )(slots, x1, pos, gtab)


KIND_NOOP, KIND_FFN, KIND_ZERO = 0, 1, 2


def _expert_kernel(tile_ref, exp_ref, kind_ref, lo_ref, hi_ref, first_ref, fresh_ref, wslot_ref,
                   next_ref, xs_ref, wgu_hbm, bgu_ref, wd_hbm, bd_ref, out_ref,
                   wgu_b, wd_b, stage_gu, stage_d, wsems):
    w = pl.program_id(0)
    kind = kind_ref[w]

    @pl.when(kind == KIND_ZERO)
    def _():
        out_ref[...] = jnp.zeros(out_ref.shape, I32)

    def weight_copies(e, s):
        return (pltpu.make_async_copy(wgu_hbm.at[e], stage_gu.at[s], wsems.at[0, s]),
                pltpu.make_async_copy(wd_hbm.at[e], stage_d.at[s], wsems.at[1, s]))

    @pl.when((kind == KIND_FFN) & (fresh_ref[w] == 1))
    def _():
        e = exp_ref[w]
        s = wslot_ref[w]

        @pl.when(w == 0)
        def _():
            for cp in weight_copies(e, s):
                cp.start()

        for cp in weight_copies(e, s):
            cp.wait()
        wgu_b[...] = stage_gu[s].astype(BF16)
        wd_b[...] = stage_d[s].astype(BF16)

        @pl.when(next_ref[w] >= 0)
        def _():
            for cp in weight_copies(next_ref[w], 1 - s):
                cp.start()

    lo = lo_ref[w]
    hi = hi_ref[w]
    first = first_ref[w]
    is_ffn = kind == KIND_FFN
    whole = (lo == 0) & (hi == TR)

    def ffn(r0, nr):
        xs = _unpack_bf16(xs_ref[r0:r0 + nr, 0:D_PACK])
        gtab = lax.bitcast_convert_type(xs_ref[r0:r0 + nr, D_PACK:XS_COLS], F32)
        this_e = exp_ref[w].astype(F32)
        gate = jnp.zeros((nr, 1), F32)
        for k in range(TOP_K):
            g_k = gtab[:, 3 * k:3 * k + 1] + gtab[:, 3 * k + 1:3 * k + 2] + gtab[:, 3 * k + 2:3 * k + 3]
            gate = gate + jnp.where(gtab[:, GTAB_EID + k:GTAB_EID + k + 1] == this_e, g_k, 0.0)
        gu = _dot(xs, wgu_b[...]) + bgu_ref[0]
        g_lin = jnp.minimum(gu[:, :D_FF], np.float32(SWIGLU_LIMIT))
        u_lin = jnp.clip(gu[:, D_FF:], np.float32(-SWIGLU_LIMIT), np.float32(SWIGLU_LIMIT))
        act = (u_lin + 1.0) * (g_lin * _sigmoid(np.float32(SWIGLU_ALPHA) * g_lin))
        eo = (_dot(act.astype(BF16), wd_b[...]) + bd_ref[0]) * gate
        eo = eo.astype(BF16).astype(F32)
        return _pack_pair(eo[:, 0:D_PACK], eo[:, D_PACK:D_MODEL])

    @pl.when(is_ffn & whole)
    def _():
        out_ref[...] = ffn(0, TR)

    for r0 in range(0, TR, TR_SUB):
        touched = (lo < r0 + TR_SUB) & (hi > r0)
        part = is_ffn & jnp.logical_not(whole)

        @pl.when(part & touched)
        def _(r0=r0):
            eo = ffn(r0, TR_SUB)
            rows = r0 + lax.broadcasted_iota(I32, (TR_SUB, 1), 0)
            mine = (rows >= lo) & (rows < hi)

            @pl.when(first == 1)
            def _():
                out_ref[r0:r0 + TR_SUB, :] = jnp.where(mine, eo, 0)

            @pl.when(first == 0)
            def _():
                out_ref[r0:r0 + TR_SUB, :] = jnp.where(mine, eo, out_ref[r0:r0 + TR_SUB, :])

        @pl.when(part & jnp.logical_not(touched) & (first == 1))
        def _(r0=r0):
            out_ref[r0:r0 + TR_SUB, :] = jnp.zeros((TR_SUB, D_PACK), I32)


def _expert_call(md, xs, wgu, bgu, wd, bd):
    pmax = xs.shape[0]
    n_items = md["tile"].shape[0]
    grid_spec = pltpu.PrefetchScalarGridSpec(
        num_scalar_prefetch=9,
        grid=(n_items,),
        in_specs=[
            pl.BlockSpec((TR, XS_COLS), lambda w, t, e, *_: (t[w], 0)),
            pl.BlockSpec(memory_space=pl.ANY),
            pl.BlockSpec((1, 1, 2 * D_FF), lambda w, t, e, *_: (e[w], 0, 0)),
            pl.BlockSpec(memory_space=pl.ANY),
            pl.BlockSpec((1, 1, D_MODEL), lambda w, t, e, *_: (e[w], 0, 0)),
        ],
        out_specs=pl.BlockSpec((TR, D_PACK), lambda w, t, e, *_: (t[w], 0)),
        scratch_shapes=[
            pltpu.VMEM((D_MODEL, 2 * D_FF), BF16),
            pltpu.VMEM((D_FF, D_MODEL), BF16),
            pltpu.VMEM((2, D_MODEL, 2 * D_FF), F32),
            pltpu.VMEM((2, D_FF, D_MODEL), F32),
            pltpu.SemaphoreType.DMA((2, 2)),
        ],
    )
    return pl.pallas_call(
        _expert_kernel,
        grid_spec=grid_spec,
        out_shape=jax.ShapeDtypeStruct((pmax, D_PACK), I32),
        compiler_params=pltpu.CompilerParams(
            dimension_semantics=("arbitrary",), vmem_limit_bytes=VMEM_LIMIT_BYTES),
        name="experts",
    )(md["tile"], md["exp"], md["kind"], md["lo"], md["hi"], md["first"], md["fresh"],
      md["wslot"], md["next"], xs, wgu, bgu, wd, bd)


def _combine_kernel(slot_ref, x1_ref, post_ref, g_ref, b_ref, eo_hbm, out_ref, ebuf, sems):
    j = pl.program_id(0)
    nt = pl.num_programs(0)
    tm = x1_ref.shape[0]
    cur = lax.rem(j, 2)

    def fetch(tile, which):
        for i in range(N_CHUNKS):
            pltpu.make_async_copy(eo_hbm.at[slot_ref[tile * N_CHUNKS + i]],
                                  _chunk_rows(ebuf.at[which], i), sems.at[which]).start()

    @pl.when(j == 0)
    def _():
        fetch(0, 0)

    @pl.when(j + 1 < nt)
    def _():
        fetch(j + 1, 1 - cur)

    for _ in range(N_CHUNKS):
        pltpu.make_async_copy(eo_hbm.at[0], _chunk_rows(ebuf.at[cur], 0), sems.at[cur]).wait()

    l_io = lax.broadcasted_iota(I32, (tm, L_SORT), 1).astype(I16)
    post = post_ref[0].astype(I16)
    hit = l_io == post[:, 0:1]
    for k in range(1, TOP_K):
        hit = hit | (l_io == post[:, k:k + 1])
    onehot = jnp.where(hit, jnp.ones((), BF16), jnp.zeros((), BF16))
    y = _dot(onehot, _unpack_bf16(ebuf[cur]))
    out_ref[...] = _layer_norm(np.float32(DEEPNORM_ALPHA) * x1_ref[...] + y, g_ref[...], b_ref[...])


def _combine_call(slots, x1, post, g, b, eo):
    n_tok = x1.shape[0]
    nt = n_tok // TM_SORT
    grid_spec = pltpu.PrefetchScalarGridSpec(
        num_scalar_prefetch=1,
        grid=(nt,),
        in_specs=[
            pl.BlockSpec((TM_SORT, D_MODEL), lambda i, *_: (i, 0)),
            pl.BlockSpec((1, TM_SORT, SUBLANES), lambda i, *_: (i, 0, 0)),
            pl.BlockSpec((1, D_MODEL), lambda i, *_: (0, 0)),
            pl.BlockSpec((1, D_MODEL), lambda i, *_: (0, 0)),
            pl.BlockSpec(memory_space=pl.ANY),
        ],
        out_specs=pl.BlockSpec((TM_SORT, D_MODEL), lambda i, *_: (i, 0)),
        scratch_shapes=[pltpu.VMEM((2, L_SORT, D_PACK), I32), pltpu.SemaphoreType.DMA((2,))],
    )
    return pl.pallas_call(
        _combine_kernel,
        grid_spec=grid_spec,
        out_shape=jax.ShapeDtypeStruct((n_tok, D_MODEL), F32),
        compiler_params=pltpu.CompilerParams(
            dimension_semantics=("arbitrary",), vmem_limit_bytes=VMEM_LIMIT_BYTES),
        name="combine",
    )(slots, x1, post, g, b, eo)


def _excl_cumsum(a, axis):
    n = a.shape[axis]
    lower = (jnp.arange(n)[:, None] > jnp.arange(n)[None, :]).astype(F32)
    af = jnp.moveaxis(a.astype(F32), axis, 0).reshape(n, -1)
    out = jnp.dot(lower, af, precision=lax.Precision.HIGHEST)
    out = out.reshape((n,) + tuple(np.delete(np.array(a.shape), axis)))
    return jnp.moveaxis(out, 0, axis).astype(I32)


def _routing_metadata(cnt):
    nt = cnt.shape[0]
    c8 = (cnt + (SEG_ALIGN - 1)) // SEG_ALIGN * SEG_ALIGN
    seg = _excl_cumsum(c8, 1)
    tot8 = jnp.sum(c8, axis=1)
    n8 = jnp.sum(c8, axis=0)
    n_al = (n8 + (TR_SUB - 1)) // TR_SUB * TR_SUB
    roomy = jnp.sum(n_al) <= nt * L_SORT
    span = jnp.where(roomy, n_al, n8)
    base = _excl_cumsum(span, 0)
    goff = base[None, :] + _excl_cumsum(c8, 0)
    p_used = jnp.sum(span)
    gap = (span - n8) // SEG_ALIGN
    gap_start = _excl_cumsum(gap, 0)
    gap_end = gap_start + gap
    n_gap = jnp.sum(gap)

    r = (jnp.arange(N_CHUNKS, dtype=I32) * SEG_ALIGN)[None, :, None]
    in_seg = (r >= seg[:, None, :]) & (r < (seg + c8)[:, None, :])
    real_slot = jnp.sum(jnp.where(in_seg, goff[:, None, :] + r - seg[:, None, :], 0), axis=2) // SEG_ALIGN
    used = tot8 // SEG_ALIGN
    n_empty = N_CHUNKS - used
    ci = jnp.arange(N_CHUNKS, dtype=I32)[None, :]
    m = (_excl_cumsum(n_empty, 0)[:, None] + (ci - used[:, None]))[:, :, None]
    in_gap = (m >= gap_start[None, None, :]) & (m < gap_end[None, None, :])
    gap_slot = jnp.sum(jnp.where(in_gap, ((base + n8) // SEG_ALIGN - gap_start)[None, None, :] + m, 0),
                       axis=2)
    m = m[:, :, 0]
    empty_slot = jnp.where(m < n_gap, gap_slot, p_used // SEG_ALIGN + (m - n_gap))
    slots = jnp.where(ci < used[:, None], real_slot, empty_slot).reshape(-1).astype(I32)

    assert (nt * L_SORT) % TR == 0
    n_tiles = nt * L_SORT // TR
    n_items = n_tiles + N_EXPERTS
    e_lo, e_hi = base, base + n8
    t_first = e_lo // TR
    t_end = (e_hi + TR - 1) // TR
    w_e = jnp.where(n8 > 0, t_end - t_first, 0)
    w_start = _excl_cumsum(w_e, 0)
    w_cum = w_start + w_e
    w_total = jnp.sum(w_e)
    tiles_used = (p_used + TR - 1) // TR

    w = jnp.arange(n_items, dtype=I32)
    e_ids = jnp.arange(N_EXPERTS, dtype=I32)
    e_w = jnp.minimum(jnp.sum((w[:, None] >= w_cum[None, :]).astype(I32), axis=1), N_EXPERTS - 1)
    is_ffn = w < w_total
    last_e = jnp.max(jnp.where(n8 > 0, e_ids, 0))
    exp = jnp.where(is_ffn, e_w, last_e).astype(I32)
    pick = exp[:, None] == e_ids[None, :]
    look = lambda table: jnp.sum(jnp.where(pick, table[None, :], 0), axis=1)
    ffn_tile = look(t_first) + (w - look(w_start))
    zero_tile = tiles_used + (w - w_total)
    is_zero = (~is_ffn) & (zero_tile < n_tiles)
    tile = jnp.where(is_ffn, ffn_tile, jnp.where(is_zero, zero_tile, n_tiles - 1)).astype(I32)
    kind = jnp.where(is_ffn, KIND_FFN, jnp.where(is_zero, KIND_ZERO, KIND_NOOP)).astype(I32)
    lo = jnp.clip(look(e_lo) - tile * TR, 0, TR).astype(I32)
    hi = jnp.clip(look(e_hi) - tile * TR, 0, TR).astype(I32)
    prev_tile = jnp.concatenate([jnp.full((1,), -1, I32), tile[:-1]])
    first = (tile != prev_tile).astype(I32)
    prev_exp = jnp.concatenate([jnp.full((1,), -1, I32), exp[:-1]])
    fresh = (exp != prev_exp).astype(I32)
    used_e = n8 > 0
    later = (e_ids[None, :] > e_ids[:, None]) & used_e[None, :]
    order_e = jnp.sum(((e_ids[None, :] < e_ids[:, None]) & used_e[None, :]).astype(I32), axis=1)
    next_e = jnp.min(jnp.where(later, e_ids[None, :], N_EXPERTS), axis=1)
    next_e = jnp.where(next_e == N_EXPERTS, -1, next_e)
    wslot = (look(order_e) % 2).astype(I32)
    nxt = look(next_e).astype(I32)
    return dict(slots=slots, tile=tile, exp=exp, kind=kind, lo=lo, hi=hi, first=first, fresh=fresh,
                wslot=wslot, next=nxt)


def kernel(x, mem, w_in, b_gate, conv_w, gmlp_ws, gmlp_b, gmlp_ln_g, gmlp_ln_b, mem_ln_g,
           mem_ln_b, w_kv, w_conv_proj, w_gmlp_proj, w_xa_proj, w_out, ln1_g, ln1_b, w_router,
           b_router, w_gate_up, b_gate_up, w_down, b_down, ln2_g, ln2_b):
    bsz, seq, d = x.shape
    n_tok = bsz * seq
    assert d == D_MODEL and n_tok % TM_SORT == 0
    depth = w_in.shape[0]
    for l in range(depth):
        row = lambda a: a[l].reshape(1, -1)
        k, v = _kv_call(mem, row(mem_ln_g), row(mem_ln_b), w_kv[l].astype(BF16))
        gbias = jnp.repeat(gmlp_b[l].T, GROUP_CH, axis=1)
        wr_hi = w_router[l].astype(BF16)
        wr_lo = (w_router[l] - wr_hi.astype(F32)).astype(BF16)
        wr_pair = jnp.pad(jnp.concatenate([wr_hi, wr_lo], axis=1), ((0, 0), (0, LANES - 2 * N_EXPERTS)))
        wr_top = jnp.pad(wr_hi, ((0, 0), (0, LANES - N_EXPERTS)))
        br = jnp.pad(b_router[l], (0, LANES - N_EXPERTS)).reshape(1, LANES)
        x1, logits = _mix_call(
            x, w_in[l].astype(BF16), row(b_gate), conv_w[l], gmlp_ws[l], gbias,
            row(gmlp_ln_g), row(gmlp_ln_b), k, v, w_conv_proj[l].astype(BF16),
            w_gmlp_proj[l].astype(BF16), w_xa_proj[l].astype(BF16), w_out[l].astype(BF16),
            row(ln1_g), row(ln1_b), wr_pair, wr_top, br)
        x1 = x1.reshape(n_tok, d)
        pos, gtab, cnt = _route_call(logits.reshape(n_tok, LANES))
        md = _routing_metadata(cnt[:, :, 0])
        xs = _sort_call(md["slots"], x1, pos, gtab)
        eo = _expert_call(md, xs.reshape(-1, XS_COLS), w_gate_up[l], b_gate_up[l][:, None, :],
                          w_down[l], b_down[l][:, None, :])
        post = jnp.transpose(pos, (0, 2, 1))
        x = _combine_call(md["slots"], x1, post, row(ln2_g), row(ln2_b),
                          eo.reshape(-1, SEG_ALIGN, D_PACK)).reshape(bsz, seq, d)
    return x
```

```python
import jax
import jax.numpy as jnp
import numpy as np
from jax import lax
from jax.experimental import pallas as pl
from jax.experimental.pallas import tpu as pltpu

F32 = jnp.float32
BF16 = jnp.bfloat16
I32 = jnp.int32
I16 = jnp.int16

D_MODEL = 1024
CHUNK = 128
GMLP_GROUPS = 8
GROUP_CH = D_MODEL // GMLP_GROUPS
MEM_LEN = 256
XA_HEADS = 4
XA_HEAD_DIM = D_MODEL // XA_HEADS
N_EXPERTS = 32
TOP_K = 4
D_FF = D_MODEL
W_IN_COLS = 9 * D_MODEL
SWIGLU_LIMIT = 7.0
SWIGLU_ALPHA = 1.702
LN_EPS = 1e-5
DEEPNORM_ALPHA = 2.0 ** 0.25

LANES = 128
SUBLANES = 8
VMEM_LIMIT_BYTES = 60000 * 1024

TM_MIX = 512
TM_SORT = 512
SEG_ALIGN = SUBLANES
MXU_DIM = 256
L_SORT = -(-(TOP_K * TM_SORT + N_EXPERTS * (SEG_ALIGN - 1)) // MXU_DIM) * MXU_DIM
assert L_SORT >= TOP_K * TM_SORT + N_EXPERTS * (SEG_ALIGN - 1) and L_SORT % MXU_DIM == 0
assert L_SORT < 2 ** 15
N_CHUNKS = L_SORT // SEG_ALIGN
SORT_PARTS = 4
assert N_CHUNKS % SORT_PARTS == 0
D_PACK = D_MODEL // 2
XS_COLS = D_PACK + LANES
GTAB_EID = 3 * TOP_K
TR = 512
TR_SUB = 256
assert TR % TR_SUB == 0


BF16_BITS = 16
TOP_HALF = np.int32(-(1 << BF16_BITS))


def _pack_pair(lo, hi):
    lo_bits = lax.shift_right_logical(lax.bitcast_convert_type(lo, I32), BF16_BITS)
    return lo_bits | lax.bitcast_convert_type(hi, I32)


def _unpack_bf16(words):
    lo = lax.bitcast_convert_type(lax.shift_left(words, BF16_BITS), F32)
    hi = lax.bitcast_convert_type(words & TOP_HALF, F32)
    return jnp.concatenate([lo.astype(BF16), hi.astype(BF16)], axis=1)


def _layer_norm(x, g, b):
    mu = jnp.mean(x, axis=-1, keepdims=True)
    xc = x - mu
    var = jnp.mean(xc * xc, axis=-1, keepdims=True)
    return xc * lax.rsqrt(var + LN_EPS) * g + b


def _gelu_tanh(x):
    c = np.float32(np.sqrt(2.0 / np.pi))
    ca = np.float32(np.sqrt(2.0 / np.pi) * 0.044715)
    half = 0.5 * x
    return half + half * jnp.tanh(x * (c + ca * (x * x)))


def _sigmoid(x):
    return 1.0 / (1.0 + jnp.exp(-x))


def _dot(a, b):
    return jnp.dot(a, b, preferred_element_type=F32)


def _const_spec(shape):
    n = len(shape)
    return pl.BlockSpec(shape, lambda *_: (0,) * n, pipeline_mode=pl.Buffered(1))


def _kv_kernel(mem_ref, g_ref, b_ref, wkv_ref, k_ref, v_ref):
    mem_n = _layer_norm(mem_ref[0], g_ref[...], b_ref[...])
    kv = _dot(mem_n.astype(BF16), wkv_ref[...])
    k_ref[0] = kv[:, :D_MODEL].astype(BF16)
    v_ref[0] = kv[:, D_MODEL:].astype(BF16)


def _kv_call(mem, g, b, wkv):
    bsz = mem.shape[0]
    return pl.pallas_call(
        _kv_kernel,
        grid=(bsz,),
        in_specs=[
            pl.BlockSpec((1, MEM_LEN, D_MODEL), lambda i: (i, 0, 0)),
            pl.BlockSpec((1, D_MODEL), lambda i: (0, 0)),
            pl.BlockSpec((1, D_MODEL), lambda i: (0, 0)),
            pl.BlockSpec((D_MODEL, 2 * D_MODEL), lambda i: (0, 0)),
        ],
        out_specs=[
            pl.BlockSpec((1, MEM_LEN, D_MODEL), lambda i: (i, 0, 0)),
            pl.BlockSpec((1, MEM_LEN, D_MODEL), lambda i: (i, 0, 0)),
        ],
        out_shape=[
            jax.ShapeDtypeStruct((bsz, MEM_LEN, D_MODEL), BF16),
            jax.ShapeDtypeStruct((bsz, MEM_LEN, D_MODEL), BF16),
        ],
        compiler_params=pltpu.CompilerParams(dimension_semantics=("arbitrary",)),
        name="kv",
    )(mem, g, b, wkv)


def _mix_kernel(x_ref, win_ref, bgate_ref, convw_ref, ws_ref, gbias_ref, glng_ref, glnb_ref,
                k_ref, v_ref, wc_ref, wg_ref, wx_ref, wo_ref, ln1g_ref, ln1b_ref,
                wr_pair_ref, wr_top_ref, br_ref, x1_ref, logit_ref, halo_ref):
    tm = x_ref.shape[1]
    d = D_MODEL
    x = x_ref[0]
    xb = x.astype(BF16)

    def proj(i):
        return _dot(xb, win_ref[:, i * d:(i + 1) * d])

    def gate(i):
        return _sigmoid(proj(6 + i) + bgate_ref[:, i * d:(i + 1) * d])

    @pl.when(pl.program_id(1) == 0)
    def _():
        halo_ref[...] = jnp.zeros((SUBLANES, d), F32)

    u = proj(1) * proj(2)
    halo = halo_ref[...]
    halo_ref[...] = u[tm - SUBLANES:tm, :]
    head_row = lax.broadcasted_iota(I32, (SUBLANES, d), 0)

    def shifted(k):
        body = pltpu.roll(u, k, axis=0)
        head = jnp.where(head_row < k, pltpu.roll(halo, k, axis=0), body[0:SUBLANES])
        return jnp.concatenate([head, body[SUBLANES:]], axis=0)

    conv = convw_ref[0:1, :] * shifted(2) + convw_ref[1:2, :] * shifted(1) + convw_ref[2:3, :] * u
    gu_lin = proj(3)
    gv_lin = proj(4)
    y_conv = (proj(0) * conv).astype(BF16)
    q = proj(5).astype(BF16)
    scores = []
    for h in range(XA_HEADS):
        sl = slice(h * XA_HEAD_DIM, (h + 1) * XA_HEAD_DIM)
        scores.append(lax.dot_general(q[:, sl], k_ref[0, :, sl], (((1,), (1,)), ((), ())),
                                      preferred_element_type=F32))
    g_conv = gate(0)
    g_gmlp = gate(1)
    merged = g_conv * _dot(y_conv, wc_ref[...])

    gu = _gelu_tanh(gu_lin)
    gv = _gelu_tanh(gv_lin)
    vn = _layer_norm(gv, glng_ref[...], glnb_ref[...]).astype(BF16)
    n_chunks = tm // CHUNK
    row_i = lax.broadcasted_iota(I32, (CHUNK, CHUNK), 0)
    col_i = lax.broadcasted_iota(I32, (CHUNK, CHUNK), 1)
    causal = col_i <= row_i
    f_cols = []
    for g in range(GMLP_GROUPS):
        w_g = jnp.where(causal, ws_ref[g], 0.0).astype(BF16)
        rhs = jnp.concatenate(
            [vn[c * CHUNK:(c + 1) * CHUNK, g * GROUP_CH:(g + 1) * GROUP_CH] for c in range(n_chunks)],
            axis=1)
        fg = _dot(w_g, rhs)
        f_cols.append(jnp.concatenate(
            [fg[:, c * GROUP_CH:(c + 1) * GROUP_CH] for c in range(n_chunks)], axis=0))
    f = jnp.concatenate(f_cols, axis=1)
    gbias = jnp.concatenate([gbias_ref[...]] * n_chunks, axis=0)
    y_gmlp = gu * (f + gbias)
    merged = merged + g_gmlp * _dot(y_gmlp.astype(BF16), wg_ref[...])

    heads = []
    for h in range(XA_HEADS):
        sl = slice(h * XA_HEAD_DIM, (h + 1) * XA_HEAD_DIM)
        s = scores[h] * np.float32(XA_HEAD_DIM ** -0.5)
        p = jnp.exp(s - jnp.max(s, axis=-1, keepdims=True))
        p = p * (1.0 / jnp.sum(p, axis=-1, keepdims=True))
        heads.append(_dot(p.astype(BF16), v_ref[0, :, sl]))
    y_xa = jnp.concatenate(heads, axis=1)
    merged = merged + gate(2) * _dot(y_xa.astype(BF16), wx_ref[...])

    mix = _dot(merged.astype(BF16), wo_ref[...])
    x1 = _layer_norm(np.float32(DEEPNORM_ALPHA) * x + mix, ln1g_ref[...], ln1b_ref[...])
    x1_ref[0] = x1

    x1h = x1.astype(BF16)
    x1l = (x1 - x1h.astype(F32)).astype(BF16)
    both = _dot(x1h, wr_pair_ref[...])
    logit_ref[0] = (both + pltpu.roll(both, LANES - N_EXPERTS, axis=1) + _dot(x1l, wr_top_ref[...])
                    + br_ref[...])


def _mix_call(x, win, bgate, convw, ws, gbias, glng, glnb, k, v, wc, wg, wx, wo, ln1g, ln1b,
              wr_pair, wr_top, br):
    bsz, seq, d = x.shape
    tm = min(TM_MIX, seq)
    assert seq % tm == 0 and tm % CHUNK == 0
    tile = lambda b, s: (b, s, 0)
    per_batch = lambda b, s: (b, 0, 0)
    return pl.pallas_call(
        _mix_kernel,
        grid=(bsz, seq // tm),
        in_specs=[
            pl.BlockSpec((1, tm, d), tile),
            _const_spec((d, W_IN_COLS)),
            _const_spec((1, 3 * d)),
            _const_spec((3, d)),
            _const_spec((GMLP_GROUPS, CHUNK, CHUNK)),
            _const_spec((CHUNK, d)),
            _const_spec((1, d)),
            _const_spec((1, d)),
            pl.BlockSpec((1, MEM_LEN, d), per_batch),
            pl.BlockSpec((1, MEM_LEN, d), per_batch),
            _const_spec((d, d)),
            _const_spec((d, d)),
            _const_spec((d, d)),
            _const_spec((d, d)),
            _const_spec((1, d)),
            _const_spec((1, d)),
            _const_spec((d, LANES)),
            _const_spec((d, LANES)),
            _const_spec((1, LANES)),
        ],
        out_specs=[pl.BlockSpec((1, tm, d), tile), pl.BlockSpec((1, tm, LANES), tile)],
        out_shape=[jax.ShapeDtypeStruct((bsz, seq, d), F32),
                   jax.ShapeDtypeStruct((bsz, seq, LANES), F32)],
        scratch_shapes=[pltpu.VMEM((SUBLANES, d), F32)],
        compiler_params=pltpu.CompilerParams(
            dimension_semantics=("arbitrary", "arbitrary"), vmem_limit_bytes=VMEM_LIMIT_BYTES),
        name="mix",
    )(x, win, bgate, convw, ws, gbias, glng, glnb, k, v, wc, wg, wx, wo, ln1g, ln1b, wr_pair, wr_top, br)


def _split3(v):
    hi = v.astype(BF16).astype(F32)
    mid = (v - hi).astype(BF16).astype(F32)
    return hi, mid, (v - hi - mid).astype(BF16).astype(F32)


def _route_kernel(logit_ref, pos_ref, gtab_ref, cnt_ref):
    tm = logit_ref.shape[0]
    work = logit_ref[...].T[0:N_EXPERTS, :]
    e_io = lax.broadcasted_iota(I32, (N_EXPERTS, tm), 0)
    neg = np.float32(-np.inf)
    vals, sels, eids = [], [], []
    for _ in range(TOP_K):
        m = jnp.max(work, axis=0, keepdims=True)
        idx = jnp.min(jnp.where(work == m, e_io, LANES), axis=0, keepdims=True)
        sel = e_io == idx
        vals.append(m)
        sels.append(sel)
        eids.append(idx.astype(F32))
        work = jnp.where(sel, neg, work)
    ex = [jnp.exp(vk - vals[0]) for vk in vals]
    inv = 1.0 / (ex[0] + ex[1] + ex[2] + ex[3])
    gates = [e * inv for e in ex]

    oh = jnp.where(sels[0] | sels[1] | sels[2] | sels[3], 1.0, 0.0).astype(F32)
    cnt = jnp.sum(oh, axis=1, keepdims=True)
    t_r = lax.broadcasted_iota(I32, (tm, tm), 0)
    t_c = lax.broadcasted_iota(I32, (tm, tm), 1)
    before = jnp.where(t_r < t_c, 1.0, 0.0).astype(BF16)
    rank = _dot(oh.astype(BF16), before)
    units = jnp.floor((cnt + np.float32(SEG_ALIGN - 1)) * np.float32(1.0 / SEG_ALIGN))
    units_b = jnp.concatenate(
        [jnp.broadcast_to(units, (N_EXPERTS, LANES)), jnp.zeros((LANES - N_EXPERTS, LANES), F32)],
        axis=0).astype(BF16)
    e_r = lax.broadcasted_iota(I32, (N_EXPERTS, LANES), 0)
    e_c = lax.broadcasted_iota(I32, (N_EXPERTS, LANES), 1)
    below = jnp.where(e_c < e_r, 1.0, 0.0).astype(BF16)
    seg = _dot(below, units_b)[:, 0:1] * np.float32(SEG_ALIGN)
    slot = seg + rank

    pos_rows = [jnp.sum(jnp.where(s, slot, 0.0), axis=0, keepdims=True) for s in sels]
    zero_row = jnp.zeros((1, tm), F32)
    pos_ref[0] = jnp.concatenate(pos_rows + [zero_row] * (SUBLANES - TOP_K), axis=0).astype(I32)
    parts = [p for g in gates for p in _split3(g)]
    table = jnp.concatenate(parts + eids + [jnp.zeros((LANES - len(parts) - TOP_K, tm), F32)], axis=0)
    gtab_ref[...] = table.T.astype(BF16)
    cnt_ref[0] = jnp.broadcast_to(cnt, (N_EXPERTS, LANES)).astype(I32)


def _route_call(logits2d):
    n_tok = logits2d.shape[0]
    nt = n_tok // TM_SORT
    return pl.pallas_call(
        _route_kernel,
        grid=(nt,),
        in_specs=[pl.BlockSpec((TM_SORT, LANES), lambda i: (i, 0))],
        out_specs=[
            pl.BlockSpec((1, SUBLANES, TM_SORT), lambda i: (i, 0, 0)),
            pl.BlockSpec((TM_SORT, LANES), lambda i: (i, 0)),
            pl.BlockSpec((1, N_EXPERTS, LANES), lambda i: (i, 0, 0)),
        ],
        out_shape=[
            jax.ShapeDtypeStruct((nt, SUBLANES, TM_SORT), I32),
            jax.ShapeDtypeStruct((n_tok, LANES), BF16),
            jax.ShapeDtypeStruct((nt, N_EXPERTS, LANES), I32),
        ],
        compiler_params=pltpu.CompilerParams(dimension_semantics=("arbitrary",)),
        name="route",
    )(logits2d)


def _chunk_rows(buf, i):
    return buf.at[pl.ds(i * SEG_ALIGN, SEG_ALIGN)]


def _sort_kernel(slot_ref, x1_ref, pos_ref, gtab_ref, xs_hbm, sbuf, sems):
    j = pl.program_id(0)
    nt = pl.num_programs(0)
    tm = x1_ref.shape[0]
    cur = lax.rem(j, 2)
    buf = sbuf.at[cur]
    pos = pos_ref[0].astype(I16)
    rhs = jnp.concatenate([x1_ref[...].astype(BF16), gtab_ref[...]], axis=1)
    rows_per_part = L_SORT // SORT_PARTS
    chunks_per_part = N_CHUNKS // SORT_PARTS
    for part in range(SORT_PARTS):
        r0 = part * rows_per_part
        j_io = (lax.broadcasted_iota(I32, (rows_per_part, tm), 0) + r0).astype(I16)
        hit = j_io == pos[0:1, :]
        for k in range(1, TOP_K):
            hit = hit | (j_io == pos[k:k + 1, :])
        onehot = jnp.where(hit, jnp.ones((), BF16), jnp.zeros((), BF16))
        rows = _dot(onehot, rhs)
        buf[r0:r0 + rows_per_part, 0:D_PACK] = _pack_pair(rows[:, 0:D_PACK], rows[:, D_PACK:D_MODEL])
        buf[r0:r0 + rows_per_part, D_PACK:XS_COLS] = lax.bitcast_convert_type(
            rows[:, D_MODEL:D_MODEL + LANES], I32)
        for i in range(part * chunks_per_part, (part + 1) * chunks_per_part):
            pltpu.make_async_copy(
                _chunk_rows(buf, i), xs_hbm.at[slot_ref[j * N_CHUNKS + i]], sems.at[cur]).start()

    def drain(which):
        for _ in range(N_CHUNKS):
            pltpu.make_async_copy(
                _chunk_rows(sbuf.at[which], 0), xs_hbm.at[0], sems.at[which]).wait()

    @pl.when(j > 0)
    def _():
        drain(1 - cur)

    @pl.when(j == nt - 1)
    def _():
        drain(cur)


def _sort_call(slots, x1, pos, gtab):
    n_tok = x1.shape[0]
    nt = n_tok // TM_SORT
    grid_spec = pltpu.PrefetchScalarGridSpec(
        num_scalar_prefetch=1,
        grid=(nt,),
        in_specs=[
            pl.BlockSpec((TM_SORT, D_MODEL), lambda i, *_: (i, 0)),
            pl.BlockSpec((1, SUBLANES, TM_SORT), lambda i, *_: (i, 0, 0)),
            pl.BlockSpec((TM_SORT, LANES), lambda i, *_: (i, 0)),
        ],
        out_specs=pl.BlockSpec(memory_space=pl.ANY),
        scratch_shapes=[
            pltpu.VMEM((2, L_SORT, XS_COLS), I32),
            pltpu.SemaphoreType.DMA((2,)),
        ],
    )
    return pl.pallas_call(
        _sort_kernel,
        grid_spec=grid_spec,
        out_shape=jax.ShapeDtypeStruct((nt * N_CHUNKS, SEG_ALIGN, XS_COLS), I32),
        compiler_params=pltpu.CompilerParams(
            dimension_semantics=("arbitrary",), vmem_limit_bytes=VMEM_LIMIT_BYTES),
        name="sort",
    )(slots, x1, pos, gtab)


KIND_NOOP, KIND_FFN, KIND_ZERO = 0, 1, 2


def _expert_kernel(tile_ref, exp_ref, kind_ref, lo_ref, hi_ref, first_ref, fresh_ref, wslot_ref,
                   next_ref, xs_ref, wgu_hbm, bgu_ref, wd_hbm, bd_ref, out_ref,
                   wgu_b, wd_b, stage_gu, stage_d, wsems):
    w = pl.program_id(0)
    kind = kind_ref[w]

    @pl.when(kind == KIND_ZERO)
    def _():
        out_ref[...] = jnp.zeros(out_ref.shape, I32)

    def weight_copies(e, s):
        return (pltpu.make_async_copy(wgu_hbm.at[e], stage_gu.at[s], wsems.at[0, s]),
                pltpu.make_async_copy(wd_hbm.at[e], stage_d.at[s], wsems.at[1, s]))

    @pl.when((kind == KIND_FFN) & (fresh_ref[w] == 1))
    def _():
        e = exp_ref[w]
        s = wslot_ref[w]

        @pl.when(w == 0)
        def _():
            for cp in weight_copies(e, s):
                cp.start()

        for cp in weight_copies(e, s):
            cp.wait()
        wgu_b[...] = stage_gu[s].astype(BF16)
        wd_b[...] = stage_d[s].astype(BF16)

        @pl.when(next_ref[w] >= 0)
        def _():
            for cp in weight_copies(next_ref[w], 1 - s):
                cp.start()

    lo = lo_ref[w]
    hi = hi_ref[w]
    first = first_ref[w]
    is_ffn = kind == KIND_FFN
    whole = (lo == 0) & (hi == TR)

    def ffn(r0, nr):
        xs = _unpack_bf16(xs_ref[r0:r0 + nr, 0:D_PACK])
        gtab = lax.bitcast_convert_type(xs_ref[r0:r0 + nr, D_PACK:XS_COLS], F32)
        this_e = exp_ref[w].astype(F32)
        gate = jnp.zeros((nr, 1), F32)
        for k in range(TOP_K):
            g_k = gtab[:, 3 * k:3 * k + 1] + gtab[:, 3 * k + 1:3 * k + 2] + gtab[:, 3 * k + 2:3 * k + 3]
            gate = gate + jnp.where(gtab[:, GTAB_EID + k:GTAB_EID + k + 1] == this_e, g_k, 0.0)
        gu = _dot(xs, wgu_b[...]) + bgu_ref[0]
        g_lin = jnp.minimum(gu[:, :D_FF], np.float32(SWIGLU_LIMIT))
        u_lin = jnp.clip(gu[:, D_FF:], np.float32(-SWIGLU_LIMIT), np.float32(SWIGLU_LIMIT))
        act = (u_lin + 1.0) * (g_lin * _sigmoid(np.float32(SWIGLU_ALPHA) * g_lin))
        eo = (_dot(act.astype(BF16), wd_b[...]) + bd_ref[0]) * gate
        eo = eo.astype(BF16).astype(F32)
        return _pack_pair(eo[:, 0:D_PACK], eo[:, D_PACK:D_MODEL])

    @pl.when(is_ffn & whole)
    def _():
        out_ref[...] = ffn(0, TR)

    for r0 in range(0, TR, TR_SUB):
        touched = (lo < r0 + TR_SUB) & (hi > r0)
        part = is_ffn & jnp.logical_not(whole)

        @pl.when(part & touched)
        def _(r0=r0):
            eo = ffn(r0, TR_SUB)
            rows = r0 + lax.broadcasted_iota(I32, (TR_SUB, 1), 0)
            mine = (rows >= lo) & (rows < hi)

            @pl.when(first == 1)
            def _():
                out_ref[r0:r0 + TR_SUB, :] = jnp.where(mine, eo, 0)

            @pl.when(first == 0)
            def _():
                out_ref[r0:r0 + TR_SUB, :] = jnp.where(mine, eo, out_ref[r0:r0 + TR_SUB, :])

        @pl.when(part & jnp.logical_not(touched) & (first == 1))
        def _(r0=r0):
            out_ref[r0:r0 + TR_SUB, :] = jnp.zeros((TR_SUB, D_PACK), I32)


def _expert_call(md, xs, wgu, bgu, wd, bd):
    pmax = xs.shape[0]
    n_items = md["tile"].shape[0]
    grid_spec = pltpu.PrefetchScalarGridSpec(
        num_scalar_prefetch=9,
        grid=(n_items,),
        in_specs=[
            pl.BlockSpec((TR, XS_COLS), lambda w, t, e, *_: (t[w], 0)),
            pl.BlockSpec(memory_space=pl.ANY),
            pl.BlockSpec((1, 1, 2 * D_FF), lambda w, t, e, *_: (e[w], 0, 0)),
            pl.BlockSpec(memory_space=pl.ANY),
            pl.BlockSpec((1, 1, D_MODEL), lambda w, t, e, *_: (e[w], 0, 0)),
        ],
        out_specs=pl.BlockSpec((TR, D_PACK), lambda w, t, e, *_: (t[w], 0)),
        scratch_shapes=[
            pltpu.VMEM((D_MODEL, 2 * D_FF), BF16),
            pltpu.VMEM((D_FF, D_MODEL), BF16),
            pltpu.VMEM((2, D_MODEL, 2 * D_FF), F32),
            pltpu.VMEM((2, D_FF, D_MODEL), F32),
            pltpu.SemaphoreType.DMA((2, 2)),
        ],
    )
    return pl.pallas_call(
        _expert_kernel,
        grid_spec=grid_spec,
        out_shape=jax.ShapeDtypeStruct((pmax, D_PACK), I32),
        compiler_params=pltpu.CompilerParams(
            dimension_semantics=("arbitrary",), vmem_limit_bytes=VMEM_LIMIT_BYTES),
        name="experts",
    )(md["tile"], md["exp"], md["kind"], md["lo"], md["hi"], md["first"], md["fresh"],
      md["wslot"], md["next"], xs, wgu, bgu, wd, bd)


def _combine_kernel(slot_ref, x1_ref, post_ref, g_ref, b_ref, eo_hbm, out_ref, ebuf, sems):
    j = pl.program_id(0)
    nt = pl.num_programs(0)
    tm = x1_ref.shape[0]
    cur = lax.rem(j, 2)

    def fetch(tile, which):
        for i in range(N_CHUNKS):
            pltpu.make_async_copy(eo_hbm.at[slot_ref[tile * N_CHUNKS + i]],
                                  _chunk_rows(ebuf.at[which], i), sems.at[which]).start()

    @pl.when(j == 0)
    def _():
        fetch(0, 0)

    @pl.when(j + 1 < nt)
    def _():
        fetch(j + 1, 1 - cur)

    for _ in range(N_CHUNKS):
        pltpu.make_async_copy(eo_hbm.at[0], _chunk_rows(ebuf.at[cur], 0), sems.at[cur]).wait()

    l_io = lax.broadcasted_iota(I32, (tm, L_SORT), 1).astype(I16)
    post = post_ref[0].astype(I16)
    hit = l_io == post[:, 0:1]
    for k in range(1, TOP_K):
        hit = hit | (l_io == post[:, k:k + 1])
    onehot = jnp.where(hit, jnp.ones((), BF16), jnp.zeros((), BF16))
    cols = D_PACK // (SORT_PARTS // 2)
    y_blocks = []
    for part in range(SORT_PARTS):
        upper, cb = divmod(part, SORT_PARTS // 2)
        w = ebuf[cur, :, cb * cols:(cb + 1) * cols]
        vals = (w & TOP_HALF) if upper else lax.shift_left(w, BF16_BITS)
        y_blocks.append(_dot(onehot, lax.bitcast_convert_type(vals, F32).astype(BF16)))
    y = jnp.concatenate(y_blocks, axis=1)
    out_ref[...] = _layer_norm(np.float32(DEEPNORM_ALPHA) * x1_ref[...] + y, g_ref[...], b_ref[...])


def _combine_call(slots, x1, post, g, b, eo):
    n_tok = x1.shape[0]
    nt = n_tok // TM_SORT
    grid_spec = pltpu.PrefetchScalarGridSpec(
        num_scalar_prefetch=1,
        grid=(nt,),
        in_specs=[
            pl.BlockSpec((TM_SORT, D_MODEL), lambda i, *_: (i, 0)),
            pl.BlockSpec((1, TM_SORT, SUBLANES), lambda i, *_: (i, 0, 0)),
            pl.BlockSpec((1, D_MODEL), lambda i, *_: (0, 0)),
            pl.BlockSpec((1, D_MODEL), lambda i, *_: (0, 0)),
            pl.BlockSpec(memory_space=pl.ANY),
        ],
        out_specs=pl.BlockSpec((TM_SORT, D_MODEL), lambda i, *_: (i, 0)),
        scratch_shapes=[pltpu.VMEM((2, L_SORT, D_PACK), I32), pltpu.SemaphoreType.DMA((2,))],
    )
    return pl.pallas_call(
        _combine_kernel,
        grid_spec=grid_spec,
        out_shape=jax.ShapeDtypeStruct((n_tok, D_MODEL), F32),
        compiler_params=pltpu.CompilerParams(
            dimension_semantics=("arbitrary",), vmem_limit_bytes=VMEM_LIMIT_BYTES),
        name="combine",
    )(slots, x1, post, g, b, eo)


def _excl_cumsum(a, axis):
    n = a.shape[axis]
    lower = (jnp.arange(n)[:, None] > jnp.arange(n)[None, :]).astype(F32)
    af = jnp.moveaxis(a.astype(F32), axis, 0).reshape(n, -1)
    out = jnp.dot(lower, af, precision=lax.Precision.HIGHEST)
    out = out.reshape((n,) + tuple(np.delete(np.array(a.shape), axis)))
    return jnp.moveaxis(out, 0, axis).astype(I32)


def _routing_metadata(cnt):
    nt = cnt.shape[0]
    c8 = (cnt + (SEG_ALIGN - 1)) // SEG_ALIGN * SEG_ALIGN
    seg = _excl_cumsum(c8, 1)
    tot8 = jnp.sum(c8, axis=1)
    n8 = jnp.sum(c8, axis=0)
    n_al = (n8 + (TR_SUB - 1)) // TR_SUB * TR_SUB
    roomy = jnp.sum(n_al) <= nt * L_SORT
    span = jnp.where(roomy, n_al, n8)
    base = _excl_cumsum(span, 0)
    goff = base[None, :] + _excl_cumsum(c8, 0)
    p_used = jnp.sum(span)
    gap = (span - n8) // SEG_ALIGN
    gap_start = _excl_cumsum(gap, 0)
    gap_end = gap_start + gap
    n_gap = jnp.sum(gap)

    r = (jnp.arange(N_CHUNKS, dtype=I32) * SEG_ALIGN)[None, :, None]
    in_seg = (r >= seg[:, None, :]) & (r < (seg + c8)[:, None, :])
    real_slot = jnp.sum(jnp.where(in_seg, goff[:, None, :] + r - seg[:, None, :], 0), axis=2) // SEG_ALIGN
    used = tot8 // SEG_ALIGN
    n_empty = N_CHUNKS - used
    ci = jnp.arange(N_CHUNKS, dtype=I32)[None, :]
    m = (_excl_cumsum(n_empty, 0)[:, None] + (ci - used[:, None]))[:, :, None]
    in_gap = (m >= gap_start[None, None, :]) & (m < gap_end[None, None, :])
    gap_slot = jnp.sum(jnp.where(in_gap, ((base + n8) // SEG_ALIGN - gap_start)[None, None, :] + m, 0),
                       axis=2)
    m = m[:, :, 0]
    empty_slot = jnp.where(m < n_gap, gap_slot, p_used // SEG_ALIGN + (m - n_gap))
    slots = jnp.where(ci < used[:, None], real_slot, empty_slot).reshape(-1).astype(I32)

    assert (nt * L_SORT) % TR == 0
    n_tiles = nt * L_SORT // TR
    n_items = n_tiles + N_EXPERTS
    e_lo, e_hi = base, base + n8
    t_first = e_lo // TR
    t_end = (e_hi + TR - 1) // TR
    w_e = jnp.where(n8 > 0, t_end - t_first, 0)
    w_start = _excl_cumsum(w_e, 0)
    w_cum = w_start + w_e
    w_total = jnp.sum(w_e)
    tiles_used = (p_used + TR - 1) // TR

    w = jnp.arange(n_items, dtype=I32)
    e_ids = jnp.arange(N_EXPERTS, dtype=I32)
    e_w = jnp.minimum(jnp.sum((w[:, None] >= w_cum[None, :]).astype(I32), axis=1), N_EXPERTS - 1)
    is_ffn = w < w_total
    last_e = jnp.max(jnp.where(n8 > 0, e_ids, 0))
    exp = jnp.where(is_ffn, e_w, last_e).astype(I32)
    pick = exp[:, None] == e_ids[None, :]
    look = lambda table: jnp.sum(jnp.where(pick, table[None, :], 0), axis=1)
    ffn_tile = look(t_first) + (w - look(w_start))
    zero_tile = tiles_used + (w - w_total)
    is_zero = (~is_ffn) & (zero_tile < n_tiles)
    tile = jnp.where(is_ffn, ffn_tile, jnp.where(is_zero, zero_tile, n_tiles - 1)).astype(I32)
    kind = jnp.where(is_ffn, KIND_FFN, jnp.where(is_zero, KIND_ZERO, KIND_NOOP)).astype(I32)
    lo = jnp.clip(look(e_lo) - tile * TR, 0, TR).astype(I32)
    hi = jnp.clip(look(e_hi) - tile * TR, 0, TR).astype(I32)
    prev_tile = jnp.concatenate([jnp.full((1,), -1, I32), tile[:-1]])
    first = (tile != prev_tile).astype(I32)
    prev_exp = jnp.concatenate([jnp.full((1,), -1, I32), exp[:-1]])
    fresh = (exp != prev_exp).astype(I32)
    used_e = n8 > 0
    later = (e_ids[None, :] > e_ids[:, None]) & used_e[None, :]
    order_e = jnp.sum(((e_ids[None, :] < e_ids[:, None]) & used_e[None, :]).astype(I32), axis=1)
    next_e = jnp.min(jnp.where(later, e_ids[None, :], N_EXPERTS), axis=1)
    next_e = jnp.where(next_e == N_EXPERTS, -1, next_e)
    wslot = (look(order_e) % 2).astype(I32)
    nxt = look(next_e).astype(I32)
    return dict(slots=slots, tile=tile, exp=exp, kind=kind, lo=lo, hi=hi, first=first, fresh=fresh,
                wslot=wslot, next=nxt)


def kernel(x, mem, w_in, b_gate, conv_w, gmlp_ws, gmlp_b, gmlp_ln_g, gmlp_ln_b, mem_ln_g,
           mem_ln_b, w_kv, w_conv_proj, w_gmlp_proj, w_xa_proj, w_out, ln1_g, ln1_b, w_router,
           b_router, w_gate_up, b_gate_up, w_down, b_down, ln2_g, ln2_b):
    bsz, seq, d = x.shape
    n_tok = bsz * seq
    assert d == D_MODEL and n_tok % TM_SORT == 0
    depth = w_in.shape[0]
    for l in range(depth):
        row = lambda a: a[l].reshape(1, -1)
        k, v = _kv_call(mem, row(mem_ln_g), row(mem_ln_b), w_kv[l].astype(BF16))
        gbias = jnp.repeat(gmlp_b[l].T, GROUP_CH, axis=1)
        wr_hi = w_router[l].astype(BF16)
        wr_lo = (w_router[l] - wr_hi.astype(F32)).astype(BF16)
        wr_pair = jnp.pad(jnp.concatenate([wr_hi, wr_lo], axis=1), ((0, 0), (0, LANES - 2 * N_EXPERTS)))
        wr_top = jnp.pad(wr_hi, ((0, 0), (0, LANES - N_EXPERTS)))
        br = jnp.pad(b_router[l], (0, LANES - N_EXPERTS)).reshape(1, LANES)
        x1, logits = _mix_call(
            x, w_in[l].astype(BF16), row(b_gate), conv_w[l], gmlp_ws[l], gbias,
            row(gmlp_ln_g), row(gmlp_ln_b), k, v, w_conv_proj[l].astype(BF16),
            w_gmlp_proj[l].astype(BF16), w_xa_proj[l].astype(BF16), w_out[l].astype(BF16),
            row(ln1_g), row(ln1_b), wr_pair, wr_top, br)
        x1 = x1.reshape(n_tok, d)
        pos, gtab, cnt = _route_call(logits.reshape(n_tok, LANES))
        md = _routing_metadata(cnt[:, :, 0])
        xs = _sort_call(md["slots"], x1, pos, gtab)
        eo = _expert_call(md, xs.reshape(-1, XS_COLS), w_gate_up[l], b_gate_up[l][:, None, :],
                          w_down[l], b_down[l][:, None, :])
        post = jnp.transpose(pos, (0, 2, 1))
        x = _combine_call(md["slots"], x1, post, row(ln2_g), row(ln2_b),
                          eo.reshape(-1, SEG_ALIGN, D_PACK)).reshape(bsz, seq, d)
    return x
```
